```python
import jax, jax.numpy as jnp
from jax import lax
import numpy as np

D_MODEL = 1024
BATCH = 8
SEQ = 2048
DEPTH = 2
DEC_BATCH = 32
DEC_SEQ = 1
PAST_LEN = 8192
PAGE_SIZE = 128

N_HEADS = 8
HEAD_DIM = 64
ATTN_WIDTH = N_HEADS * HEAD_DIM
D_RNN = D_MODEL
N_LRU_BLOCKS = 8
LRU_BLOCK = D_RNN // N_LRU_BLOCKS
CONV_WIDTH = 4
LRU_C = 8.0
D_FF = 7 * D_MODEL // 2
N_EXPERTS = 8
TOP_K = 2
D_PLE = 256
Q_BLOCK = 128
EPS = 1e-6
NEG_INF = -1e30
SPLIT_SIZES = (ATTN_WIDTH, ATTN_WIDTH, ATTN_WIDTH, N_HEADS, D_RNN, D_RNN, D_MODEL, D_MODEL)
N_IN = sum(SPLIT_SIZES)
N_DENSE = (DEPTH + 1) // 2
N_MOE = DEPTH // 2

kernel_name = "fox_rglru_gated_hybrid_step"


def rmsnorm(x, g):
    xf = x.astype(jnp.float32)
    y = xf * lax.rsqrt(jnp.mean(xf * xf, axis=-1, keepdims=True) + EPS) * g.astype(jnp.float32)
    return y.astype(x.dtype)


def fox_attend(q, k, v, cq, ck, q_pos, k_pos):
    s = jnp.einsum('bqhd,bkhd->bhqk', q, k).astype(jnp.float32) * (HEAD_DIM ** -0.5)
    s = s + jnp.transpose(cq, (0, 2, 1))[:, :, :, None] - jnp.transpose(ck, (0, 2, 1))[:, :, None, :]
    mask = k_pos[None, :] <= q_pos[:, None]
    s = jnp.where(mask[None, None], s, NEG_INF)
    p = jax.nn.softmax(s, axis=-1).astype(v.dtype)
    return jnp.einsum('bhqk,bkhd->bqhd', p, v)


def attn_prompt(q, k, v, logf):
    B, T, H, Dh = q.shape
    nb = T // Q_BLOCK
    c = jnp.cumsum(logf, axis=1)
    pos = jnp.arange(T)
    qb = q.reshape(B, nb, Q_BLOCK, H, Dh).swapaxes(0, 1)
    cb = c.reshape(B, nb, Q_BLOCK, H).swapaxes(0, 1)
    pb = pos.reshape(nb, Q_BLOCK)

    def one(blk):
        qi, ci, pi = blk
        return fox_attend(qi, k, v, ci, c, pi, pos)

    o = lax.map(one, (qb, cb, pb))
    return o.swapaxes(0, 1).reshape(B, T, H * Dh)


def attn_with_past(q, k, v, logf, past_k, past_v, past_logf):
    B, T, H, Dh = q.shape
    k_all = jnp.concatenate([past_k, k], axis=1)
    v_all = jnp.concatenate([past_v, v], axis=1)
    c = jnp.cumsum(jnp.concatenate([past_logf.astype(jnp.float32), logf], axis=1), axis=1)
    L = c.shape[1]
    past = L - T
    q_pos = past + jnp.arange(T)
    k_pos = jnp.arange(L)
    o = fox_attend(q, k_all, v_all, c[:, past:], c, q_pos, k_pos)
    return o.reshape(B, T, H * Dh)


def lru_combine(e1, e2):
    a1, b1 = e1
    a2, b2 = e2
    return a1 * a2, a2 * b1 + b2


def rg_lru_branch(xr, gr, conv_prev, h0, conv_w, conv_b, w_rg, b_rg, w_ig, b_ig, lam):
    B, T, C = xr.shape
    xp = jnp.concatenate([conv_prev.astype(xr.dtype), xr], axis=1)
    xc = conv_b + sum(xp[:, j:j + T] * conv_w[j] for j in range(CONV_WIDTH))
    new_conv = xp[:, T:]
    xb = xc.reshape(B, T, N_LRU_BLOCKS, LRU_BLOCK)
    r = jax.nn.sigmoid((jnp.einsum('btnc,ncd->btnd', xb, w_rg).reshape(B, T, C) + b_rg).astype(jnp.float32))
    i = jax.nn.sigmoid((jnp.einsum('btnc,ncd->btnd', xb, w_ig).reshape(B, T, C) + b_ig).astype(jnp.float32))
    log_a = -LRU_C * r * jax.nn.softplus(-lam.astype(jnp.float32))
    a = jnp.exp(log_a)
    bx = jnp.sqrt(-jnp.expm1(2.0 * log_a)) * (i * xc.astype(jnp.float32))
    A, Bc = lax.associative_scan(lru_combine, (a, bx), axis=1)
    h = A * h0.astype(jnp.float32)[:, None] + Bc
    y = (h * jax.nn.gelu(gr.astype(jnp.float32))).astype(xr.dtype)
    return y, new_conv, h[:, -1]


def swiglu(x, wg, wu, wd):
    return (jax.nn.silu(x @ wg) * (x @ wu)) @ wd


def moe_ffn(xn, router, wg, wu, wd):
    B, T, D = xn.shape
    xt = xn.reshape(B * T, D)
    logits = (xt @ router).astype(jnp.float32)
    top_v, top_i = lax.top_k(logits, TOP_K)
    w = jax.nn.softmax(top_v, axis=-1)
    gates = jnp.sum(jax.nn.one_hot(top_i, N_EXPERTS, dtype=jnp.float32) * w[..., None], axis=1)
    out = jnp.zeros_like(xt)
    for e in range(N_EXPERTS):
        out = out + gates[:, e:e + 1].astype(xt.dtype) * swiglu(xt, wg[e], wu[e], wd[e])
    return out.reshape(B, T, D)


def trunk_layer(x, p_l, l, lw, past, conv_prev, h0):
    B, T, _ = x.shape
    xn = rmsnorm(x, lw['g_mix'])
    proj = xn @ lw['w_in']
    idx = list(np.cumsum(SPLIT_SIZES)[:-1])
    q, k, v, fl, xr, gr, ga, gb = jnp.split(proj, idx, axis=-1)
    q = q.reshape(B, T, N_HEADS, HEAD_DIM)
    k = k.reshape(B, T, N_HEADS, HEAD_DIM)
    v = v.reshape(B, T, N_HEADS, HEAD_DIM)
    logf = jax.nn.log_sigmoid(fl.astype(jnp.float32) + lw['b_f'].astype(jnp.float32))
    if past is None:
        attn = attn_prompt(q, k, v, logf)
    else:
        attn = attn_with_past(q, k, v, logf, past[0], past[1], past[2])
    y_rnn, conv_new, h_new = rg_lru_branch(xr, gr, conv_prev, h0, lw['conv_w'], lw['conv_b'],
                                           lw['w_rg'], lw['b_rg'], lw['w_ig'], lw['b_ig'], lw['lam'])
    merged = jax.nn.sigmoid(ga) * (attn @ lw['w_a_up']) + jax.nn.sigmoid(gb) * (y_rnn @ lw['w_b_up'])
    x = x + merged @ lw['w_out']
    xn = rmsnorm(x, lw['g_ffn'])
    if l % 2 == 0:
        x = x + swiglu(xn, lw['wg'], lw['wu'], lw['wd'])
    else:
        x = x + moe_ffn(xn, lw['router'], lw['wg'], lw['wu'], lw['wd'])
    gate = jax.nn.sigmoid(rmsnorm(x, lw['g_ple']) @ lw['w_ple_gate'])
    x = x + gate * (p_l @ lw['w_ple_proj'])
    return x, k, v, logf, conv_new, h_new


def setup_inputs(seed: int = 0) -> dict:
    key = jax.random.key(seed)
    ks = iter(jax.random.split(key, 64))
    f32 = jnp.float32
    nrm = lambda shape, s: jax.random.normal(next(ks), shape, f32) * s
    n_pages = PAST_LEN // PAGE_SIZE
    n_used = DEC_BATCH * n_pages
    n_pool = n_used + max(1, n_used // 4)
    u = jax.random.uniform(next(ks), (DEPTH, D_RNN), f32, 0.9, 0.999)
    s_lam = u ** (1.0 / LRU_C)
    lam = jnp.log(s_lam) - jnp.log1p(-s_lam)
    page_table = jax.random.permutation(next(ks), n_pool)[:n_used].reshape(DEC_BATCH, n_pages).astype(jnp.int32)
    return {
        'x_prompt': nrm((BATCH, SEQ, D_MODEL), 1.0),
        'x_sample': nrm((DEC_BATCH, DEC_SEQ, D_MODEL), 1.0),
        'p_prompt': nrm((DEPTH, BATCH, SEQ, D_PLE), 1.0),
        'p_sample': nrm((DEPTH, DEC_BATCH, DEC_SEQ, D_PLE), 1.0),
        'cache_k': nrm((DEPTH, n_pool, PAGE_SIZE, N_HEADS, HEAD_DIM), 1.0),
        'cache_v': nrm((DEPTH, n_pool, PAGE_SIZE, N_HEADS, HEAD_DIM), 1.0),
        'cache_logf': jax.nn.log_sigmoid(nrm((DEPTH, n_pool, PAGE_SIZE, N_HEADS), 1.0) + 2.0),
        'state_conv': nrm((DEPTH, DEC_BATCH, CONV_WIDTH - 1, D_RNN), 1.0),
        'state_h': nrm((DEPTH, DEC_BATCH, D_RNN), 0.5),
        'page_table': page_table,
        'g_mix': 1.0 + nrm((DEPTH, D_MODEL), 0.02),
        'w_in': nrm((DEPTH, D_MODEL, N_IN), D_MODEL ** -0.5),
        'b_f': 2.0 + nrm((DEPTH, N_HEADS), 0.1),
        'conv_w': nrm((DEPTH, CONV_WIDTH, D_RNN), CONV_WIDTH ** -0.5),
        'conv_b': nrm((DEPTH, D_RNN), 0.01),
        'w_rg': nrm((DEPTH, N_LRU_BLOCKS, LRU_BLOCK, LRU_BLOCK), LRU_BLOCK ** -0.5),
        'b_rg': nrm((DEPTH, D_RNN), 0.01),
        'w_ig': nrm((DEPTH, N_LRU_BLOCKS, LRU_BLOCK, LRU_BLOCK), LRU_BLOCK ** -0.5),
        'b_ig': nrm((DEPTH, D_RNN), 0.01),
        'lru_lambda': lam,
        'w_a_up': nrm((DEPTH, ATTN_WIDTH, D_MODEL), ATTN_WIDTH ** -0.5),
        'w_b_up': nrm((DEPTH, D_RNN, D_MODEL), D_RNN ** -0.5),
        'w_out': nrm((DEPTH, D_MODEL, D_MODEL), D_MODEL ** -0.5),
        'g_ffn': 1.0 + nrm((DEPTH, D_MODEL), 0.02),
        'dense_wg': nrm((N_DENSE, D_MODEL, D_FF), D_MODEL ** -0.5),
        'dense_wu': nrm((N_DENSE, D_MODEL, D_FF), D_MODEL ** -0.5),
        'dense_wd': nrm((N_DENSE, D_FF, D_MODEL), D_FF ** -0.5),
        'moe_router': nrm((N_MOE, D_MODEL, N_EXPERTS), D_MODEL ** -0.5),
        'moe_wg': nrm((N_MOE, N_EXPERTS, D_MODEL, D_FF), D_MODEL ** -0.5),
        'moe_wu': nrm((N_MOE, N_EXPERTS, D_MODEL, D_FF), D_MODEL ** -0.5),
        'moe_wd': nrm((N_MOE, N_EXPERTS, D_FF, D_MODEL), D_FF ** -0.5),
        'g_ple': 1.0 + nrm((DEPTH, D_MODEL), 0.02),
        'w_ple_gate': nrm((DEPTH, D_MODEL, D_MODEL), D_MODEL ** -0.5),
        'w_ple_proj': nrm((DEPTH, D_PLE, D_MODEL), D_PLE ** -0.5),
        'g_final': 1.0 + nrm((D_MODEL,), 0.02),
    }


def reference(x_prompt, x_sample, p_prompt, p_sample, cache_k, cache_v, cache_logf, state_conv, state_h,
              page_table, g_mix, w_in, b_f, conv_w, conv_b, w_rg, b_rg, w_ig, b_ig, lru_lambda,
              w_a_up, w_b_up, w_out, g_ffn, dense_wg, dense_wu, dense_wd, moe_router, moe_wg, moe_wu,
              moe_wd, g_ple, w_ple_gate, w_ple_proj, g_final):
    xp, xs = x_prompt, x_sample
    DB = x_sample.shape[0]
    kp, vp, lp, cp, hp = [], [], [], [], []
    ksm, vsm, lsm, csm, hsm = [], [], [], [], []
    for l in range(DEPTH):
        lw = {'g_mix': g_mix[l], 'w_in': w_in[l], 'b_f': b_f[l], 'conv_w': conv_w[l], 'conv_b': conv_b[l],
              'w_rg': w_rg[l], 'b_rg': b_rg[l], 'w_ig': w_ig[l], 'b_ig': b_ig[l], 'lam': lru_lambda[l],
              'w_a_up': w_a_up[l], 'w_b_up': w_b_up[l], 'w_out': w_out[l], 'g_ffn': g_ffn[l],
              'g_ple': g_ple[l], 'w_ple_gate': w_ple_gate[l], 'w_ple_proj': w_ple_proj[l]}
        m = l // 2
        if l % 2 == 0:
            lw['wg'], lw['wu'], lw['wd'] = dense_wg[m], dense_wu[m], dense_wd[m]
        else:
            lw['router'] = moe_router[m]
            lw['wg'], lw['wu'], lw['wd'] = moe_wg[m], moe_wu[m], moe_wd[m]
        conv0 = jnp.zeros((xp.shape[0], CONV_WIDTH - 1, D_RNN), xp.dtype)
        h0 = jnp.zeros((xp.shape[0], D_RNN), jnp.float32)
        xp, k1, v1, f1, c1, h1 = trunk_layer(xp, p_prompt[l], l, lw, None, conv0, h0)
        kp.append(k1); vp.append(v1); lp.append(f1); cp.append(c1); hp.append(h1)
        past_k = cache_k[l, page_table].reshape(DB, -1, N_HEADS, HEAD_DIM)
        past_v = cache_v[l, page_table].reshape(DB, -1, N_HEADS, HEAD_DIM)
        past_f = cache_logf[l, page_table].reshape(DB, -1, N_HEADS)
        xs, k2, v2, f2, c2, h2 = trunk_layer(xs, p_sample[l], l, lw, (past_k, past_v, past_f),
                                             state_conv[l], state_h[l])
        ksm.append(k2); vsm.append(v2); lsm.append(f2); csm.append(c2); hsm.append(h2)
    y_prompt = rmsnorm(xp, g_final)
    y_sample = rmsnorm(xs, g_final)
    return (y_prompt, y_sample,
            jnp.stack(kp), jnp.stack(vp), jnp.stack(lp), jnp.stack(cp), jnp.stack(hp),
            jnp.stack(ksm), jnp.stack(vsm), jnp.stack(lsm), jnp.stack(csm), jnp.stack(hsm))
```

```python
import functools

import jax
import jax.numpy as jnp
from jax import lax
from jax.experimental import pallas as pl
from jax.experimental.pallas import tpu as pltpu

BF16 = jnp.bfloat16
F32 = jnp.float32

EPS = 1e-6
NEG_INF = -1e30
LRU_C = 8.0
N_HEADS = 8
HEAD_DIM = 64
ATTN_WIDTH = N_HEADS * HEAD_DIM
N_LRU_BLOCKS = 8
CONV_WIDTH = 4
N_EXPERTS = 8

LANES = 128
SUBLANES = 8
VMEM_LIMIT_BYTES = 56 * 1024 * 1024

PROJ_TILE = 256
ATTN_TILE = 512
LRU_TILE = 256
ROW_TILE = 512
FFN_ROW_TILE = 1024
FFN_COL_TILE = 512
DECODE_PAGES = 8


def _cparams(*sem):
    return pltpu.CompilerParams(dimension_semantics=sem, vmem_limit_bytes=VMEM_LIMIT_BYTES)


def _full(shape):
    zeros = (0,) * len(shape)
    return pl.BlockSpec(shape, lambda *_: zeros)


def _rmsnorm(x, g):
    return x * lax.rsqrt(jnp.mean(x * x, axis=-1, keepdims=True) + EPS) * g


def _log_sigmoid(z):
    return jnp.minimum(z, 0.0) - jnp.log1p(jnp.exp(-jnp.abs(z)))


def _softplus(z):
    return jnp.maximum(z, 0.0) + jnp.log1p(jnp.exp(-jnp.abs(z)))


def _dot(a, b):
    return jnp.dot(a, b, preferred_element_type=F32)


def _dot_nt(a, b):
    return lax.dot_general(a, b, (((1,), (1,)), ((), ())), preferred_element_type=F32)


def _exact_dot(x, w01):
    hi = x.astype(BF16)
    r1 = x - hi.astype(F32)
    mid = r1.astype(BF16)
    lo = (r1 - mid.astype(F32)).astype(BF16)
    return _dot(hi, w01) + _dot(mid, w01) + _dot(lo, w01)


def _proj_prompt_kernel(x_ref, g_ref, wq_ref, wkt_ref, wvt_ref, wft_ref, bf_ref, w4_ref, tri_ref,
                        q8_ref, kt_ref, vt_ref, ktb_ref, vtb_ref, lft_ref, ct_ref,
                        xr_ref, gr_ref, ga_ref, gb_ref, carry_ref):
    t = pl.program_id(1)

    @pl.when(t == 0)
    def _():
        carry_ref[...] = jnp.zeros_like(carry_ref)

    xn = _rmsnorm(x_ref[0], g_ref[...]).astype(BF16)
    q = _dot(xn, wq_ref[...])
    for h in range(N_HEADS):
        q8_ref[0, h] = q[:, h * HEAD_DIM:(h + 1) * HEAD_DIM].astype(BF16)
    kt = _dot_nt(wkt_ref[...], xn)
    kt_ref[0] = kt
    ktb_ref[0] = kt.astype(BF16)
    vt = _dot_nt(wvt_ref[...], xn)
    vt_ref[0] = vt
    vtb_ref[0] = vt.astype(BF16)
    ft = _dot_nt(wft_ref[...], xn)
    lf = _log_sigmoid(ft + bf_ref[...])[:N_HEADS]
    lft_ref[0] = lf
    ct = _exact_dot(lf, tri_ref[...]) + carry_ref[:, :1]
    ct_ref[0] = ct
    carry_ref[...] = jnp.broadcast_to(ct[:, -1:], carry_ref.shape)
    d = xr_ref.shape[-1]
    xr_ref[0] = _dot(xn, w4_ref[:, 0 * d:1 * d])
    gr_ref[0] = _dot(xn, w4_ref[:, 1 * d:2 * d])
    ga_ref[0] = _dot(xn, w4_ref[:, 2 * d:3 * d])
    gb_ref[0] = _dot(xn, w4_ref[:, 3 * d:4 * d])


def _proj_prompt(x, g, wq, wkt, wvt, wft, bf_col, w4):
    b, t, d = x.shape
    tm = min(PROJ_TILE, t)
    nt = t // tm
    tri = (lax.broadcasted_iota(jnp.int32, (tm, tm), 0)
           <= lax.broadcasted_iota(jnp.int32, (tm, tm), 1)).astype(BF16)
    a = ATTN_WIDTH
    seq = lambda w: pl.BlockSpec((1, tm, w), lambda i, j: (i, j, 0))
    tr = lambda r: pl.BlockSpec((1, r, tm), lambda i, j: (i, 0, j))
    out_shape = [
        jax.ShapeDtypeStruct((b, N_HEADS, t, HEAD_DIM), BF16),
        jax.ShapeDtypeStruct((b, a, t), F32), jax.ShapeDtypeStruct((b, a, t), F32),
        jax.ShapeDtypeStruct((b, a, t), BF16), jax.ShapeDtypeStruct((b, a, t), BF16),
        jax.ShapeDtypeStruct((b, N_HEADS, t), F32), jax.ShapeDtypeStruct((b, N_HEADS, t), F32),
    ] + [jax.ShapeDtypeStruct((b, t, d), F32)] * 4
    out_specs = [
        pl.BlockSpec((1, N_HEADS, tm, HEAD_DIM), lambda i, j: (i, 0, j, 0)),
        tr(a), tr(a), tr(a), tr(a), tr(N_HEADS), tr(N_HEADS),
        seq(d), seq(d), seq(d), seq(d),
    ]
    return pl.pallas_call(
        _proj_prompt_kernel,
        grid=(b, nt),
        in_specs=[seq(d), _full(g.shape), _full(wq.shape), _full(wkt.shape), _full(wvt.shape),
                  _full(wft.shape), _full(bf_col.shape), _full(w4.shape), _full(tri.shape)],
        out_specs=out_specs,
        out_shape=out_shape,
        scratch_shapes=[pltpu.VMEM((N_HEADS, LANES), F32)],
        compiler_params=_cparams("arbitrary", "arbitrary"),
        name="proj_prompt",
    )(x, g, wq, wkt, wvt, wft, bf_col, w4, tri)


def _proj_sample_kernel(x_ref, g_ref, wq_ref, wk_ref, wv_ref, wf_ref, bf_ref, w4_ref,
                        q_ref, k_ref, v_ref, lf_ref, xr_ref, gr_ref, ga_ref, gb_ref):
    xn = _rmsnorm(x_ref[...], g_ref[...]).astype(BF16)
    q_ref[...] = _dot(xn, wq_ref[...])
    k_ref[...] = _dot(xn, wk_ref[...])
    v_ref[...] = _dot(xn, wv_ref[...])
    lf_ref[...] = _log_sigmoid(_dot(xn, wf_ref[...]) + bf_ref[...])
    d = xr_ref.shape[-1]
    xr_ref[...] = _dot(xn, w4_ref[:, 0 * d:1 * d])
    gr_ref[...] = _dot(xn, w4_ref[:, 1 * d:2 * d])
    ga_ref[...] = _dot(xn, w4_ref[:, 2 * d:3 * d])
    gb_ref[...] = _dot(xn, w4_ref[:, 3 * d:4 * d])


def _proj_sample(x, g, wq, wk, wv, wf, bf_row, w4):
    m, d = x.shape
    a = ATTN_WIDTH
    shapes = [(m, a), (m, a), (m, a), (m, LANES), (m, d), (m, d), (m, d), (m, d)]
    return pl.pallas_call(
        _proj_sample_kernel,
        grid=(1,),
        in_specs=[_full(v.shape) for v in (x, g, wq, wk, wv, wf, bf_row, w4)],
        out_specs=[_full(s) for s in shapes],
        out_shape=[jax.ShapeDtypeStruct(s, F32) for s in shapes],
        compiler_params=_cparams("arbitrary"),
        name="proj_sample",
    )(x, g, wq, wk, wv, wf, bf_row, w4)


def _attn_prompt_kernel(qi_ref, ki_ref, q_ref, kt_ref, vt_ref, c_ref, ct_ref, o_ref,
                        m_ref, l_ref, acc_ref):
    step = pl.program_id(1)
    qi = qi_ref[step]
    ki = ki_ref[step]
    ta = q_ref.shape[2]

    @pl.when(ki == 0)
    def _():
        m_ref[...] = jnp.full_like(m_ref, NEG_INF)
        l_ref[...] = jnp.zeros_like(l_ref)
        acc_ref[...] = jnp.zeros_like(acc_ref)

    row = qi * ta + lax.broadcasted_iota(jnp.int32, (ta, ta), 0)
    col = ki * ta + lax.broadcasted_iota(jnp.int32, (ta, ta), 1)
    mask = col <= row
    scale = HEAD_DIM ** -0.5
    for h in range(N_HEADS):
        hs = slice(h * HEAD_DIM, (h + 1) * HEAD_DIM)
        s = _dot(q_ref[0, h], kt_ref[0, hs, :]) * scale
        s = s + c_ref[0, :, h:h + 1] - ct_ref[0, h:h + 1, :]
        s = jnp.where(mask, s, NEG_INF)
        m_old = m_ref[h]
        m_new = jnp.maximum(m_old, jnp.max(s, axis=-1, keepdims=True))
        p = jnp.exp(s - m_new)
        alpha = jnp.exp(m_old - m_new)
        l_ref[h] = alpha * l_ref[h] + jnp.sum(p, axis=-1, keepdims=True)
        acc_ref[h] = alpha * acc_ref[h] + _dot_nt(p.astype(BF16), vt_ref[0, hs, :])
        m_ref[h] = m_new

    @pl.when(ki == qi)
    def _():
        for h in range(N_HEADS):
            o_ref[0, :, h * HEAD_DIM:(h + 1) * HEAD_DIM] = (acc_ref[h] / l_ref[h]).astype(o_ref.dtype)


def _attn_prompt(q8, ktb, vtb, c, ct):
    b, _, t, _ = q8.shape
    ta = min(ATTN_TILE, t)
    nq = t // ta
    pairs = [(i, j) for i in range(nq) for j in range(i + 1)]
    qi_tab = jnp.asarray([p[0] for p in pairs], jnp.int32)
    ki_tab = jnp.asarray([p[1] for p in pairs], jnp.int32)
    a = ATTN_WIDTH
    grid_spec = pltpu.PrefetchScalarGridSpec(
        num_scalar_prefetch=2,
        grid=(b, len(pairs)),
        in_specs=[
            pl.BlockSpec((1, N_HEADS, ta, HEAD_DIM), lambda i, s, qt, kt: (i, 0, qt[s], 0)),
            pl.BlockSpec((1, a, ta), lambda i, s, qt, kt: (i, 0, kt[s])),
            pl.BlockSpec((1, a, ta), lambda i, s, qt, kt: (i, 0, kt[s])),
            pl.BlockSpec((1, ta, N_HEADS), lambda i, s, qt, kt: (i, qt[s], 0)),
            pl.BlockSpec((1, N_HEADS, ta), lambda i, s, qt, kt: (i, 0, kt[s])),
        ],
        out_specs=pl.BlockSpec((1, ta, a), lambda i, s, qt, kt: (i, qt[s], 0)),
        scratch_shapes=[pltpu.VMEM((N_HEADS, ta, 1), F32), pltpu.VMEM((N_HEADS, ta, 1), F32),
                        pltpu.VMEM((N_HEADS, ta, HEAD_DIM), F32)],
    )
    return pl.pallas_call(
        _attn_prompt_kernel,
        grid_spec=grid_spec,
        out_shape=jax.ShapeDtypeStruct((b, t, a), BF16),
        compiler_params=_cparams("arbitrary", "arbitrary"),
        name="attn_prompt",
    )(qi_tab, ki_tab, q8, ktb, vtb, c, ct)


def _attn_decode_kernel(n_pg, pt_ref, q_ref, kn_ref, vn_ref, lfn_ref, w_ref, *rest):
    k_refs = rest[0:n_pg]
    v_refs = rest[n_pg:2 * n_pg]
    lt_refs = rest[2 * n_pg:3 * n_pg]
    o_ref = rest[3 * n_pg]
    m_ref, l_ref, c_ref, acc_ref = rest[3 * n_pg + 1:]
    c_step = pl.program_id(1)
    scale = HEAD_DIM ** -0.5
    lane = lax.broadcasted_iota(jnp.int32, (1, LANES), 1)

    @pl.when(c_step == 0)
    def _():
        for h in range(N_HEADS):
            hs = slice(h * HEAD_DIM, (h + 1) * HEAD_DIM)
            s0 = jnp.sum(q_ref[0, hs, :] * kn_ref[0, hs, :], axis=0, keepdims=True) * scale
            m_ref[h:h + 1, :] = jnp.broadcast_to(s0, (1, LANES))
            l_ref[h:h + 1, :] = jnp.where(lane == 0, 1.0, 0.0)
            acc_ref[hs, :] = jnp.where(lane == 0, vn_ref[0, hs, :], 0.0)
            c_ref[h:h + 1, :] = jnp.broadcast_to(lfn_ref[0, h:h + 1, :], (1, LANES))

    carry = c_ref[...]
    bias = []
    for i in range(n_pg):
        r = _exact_dot(lt_refs[i][0], w_ref[...])
        bias.append(r[:, :LANES] + carry)
        carry = carry + r[:, LANES:]
    c_ref[...] = carry

    for h in range(N_HEADS):
        hs = slice(h * HEAD_DIM, (h + 1) * HEAD_DIM)
        qh = q_ref[0, hs, :]
        s = [jnp.sum(k_refs[i][h] * qh, axis=0, keepdims=True) * scale + bias[i][h:h + 1, :]
             for i in range(n_pg)]
        m_old = m_ref[h:h + 1, :]
        m_tile = s[0]
        for i in range(1, n_pg):
            m_tile = jnp.maximum(m_tile, s[i])
        m_new = jnp.maximum(m_old, jnp.max(m_tile, axis=1, keepdims=True))
        alpha = jnp.exp(m_old - m_new)
        l_new = alpha * l_ref[h:h + 1, :]
        acc = alpha * acc_ref[hs, :]
        for i in range(n_pg):
            p = jnp.exp(s[i] - m_new)
            l_new = l_new + p
            acc = acc + v_refs[i][h] * p
        m_ref[h:h + 1, :] = m_new
        l_ref[h:h + 1, :] = l_new
        acc_ref[hs, :] = acc

    @pl.when(c_step == pl.num_programs(1) - 1)
    def _():
        for h in range(N_HEADS):
            hs = slice(h * HEAD_DIM, (h + 1) * HEAD_DIM)
            den = jnp.sum(l_ref[h:h + 1, :], axis=1, keepdims=True)
            o_ref[0, hs, :] = jnp.sum(acc_ref[hs, :], axis=1, keepdims=True) / den


def _attn_decode(layer, page_table, q_col, kn_col, vn_col, lfn_col, kt_pool, vt_pool, lt_pool, n_pool):
    db, n_pages = page_table.shape
    n_pg = min(DECODE_PAGES, n_pages)
    n_chunks = n_pages // n_pg
    page = kt_pool.shape[-1]
    base = layer * n_pool
    rowi = lax.broadcasted_iota(jnp.int32, (page, 2 * page), 0)
    coli = lax.broadcasted_iota(jnp.int32, (page, 2 * page), 1)
    w = ((rowi > coli) | (coli >= page)).astype(BF16)

    def page_of(slot):
        def index(b, c, pt):
            return base + pt[b * n_pages + (n_pages - 1 - (c * n_pg + slot))]
        return index

    col = lambda r: pl.BlockSpec((1, r, 1), lambda b, c, pt: (b, 0, 0))
    def page_specs(block):
        return [pl.BlockSpec(block, (lambda b, c, pt, f=page_of(i): (f(b, c, pt), 0, 0))) for i in range(n_pg)]
    grid_spec = pltpu.PrefetchScalarGridSpec(
        num_scalar_prefetch=1,
        grid=(db, n_chunks),
        in_specs=[col(ATTN_WIDTH), col(ATTN_WIDTH), col(ATTN_WIDTH), col(N_HEADS),
                  pl.BlockSpec(w.shape, lambda b, c, pt: (0, 0))]
        + page_specs((N_HEADS, HEAD_DIM, page)) + page_specs((N_HEADS, HEAD_DIM, page))
        + page_specs((1, N_HEADS, page)),
        out_specs=col(ATTN_WIDTH),
        scratch_shapes=[pltpu.VMEM((N_HEADS, LANES), F32), pltpu.VMEM((N_HEADS, LANES), F32),
                        pltpu.VMEM((N_HEADS, LANES), F32), pltpu.VMEM((ATTN_WIDTH, LANES), F32)],
    )
    return pl.pallas_call(
        functools.partial(_attn_decode_kernel, n_pg),
        grid_spec=grid_spec,
        out_shape=jax.ShapeDtypeStruct((db, ATTN_WIDTH, 1), F32),
        compiler_params=_cparams("arbitrary", "arbitrary"),
        name="attn_decode",
    )(page_table.reshape(-1), q_col, kn_col, vn_col, lfn_col, w,
      *([kt_pool] * n_pg), *([vt_pool] * n_pg), *([lt_pool] * n_pg))


def _lru_gates(xc, wrg_ref, brg_ref, wig_ref, big_ref, lam_ref, store):
    xcb = xc.astype(BF16)
    blk = xc.shape[-1] // N_LRU_BLOCKS
    for n in range(N_LRU_BLOCKS):
        cs = slice(n * blk, (n + 1) * blk)
        r = jax.nn.sigmoid(_dot(xcb[:, cs], wrg_ref[n]) + brg_ref[:, cs])
        i = jax.nn.sigmoid(_dot(xcb[:, cs], wig_ref[n]) + big_ref[:, cs])
        log_a = -LRU_C * r * _softplus(-lam_ref[:, cs])
        a = jnp.exp(log_a)
        th = jnp.tanh(log_a)
        mult = jnp.sqrt(-2.0 * th / (1.0 - th))
        store(cs, a, mult * (i * xc[:, cs]))


def _lru_prompt_kernel(xr_ref, gr_ref, cp_ref, h0_ref, cw_ref, cb_ref, wrg_ref, brg_ref, wig_ref,
                       big_ref, lam_ref, y_ref, tail_ref, hl_ref, buf_ref, a_ref, b_ref, h_ref, hc_ref):
    t = pl.program_id(1)
    tt = xr_ref.shape[1]
    halo = SUBLANES

    @pl.when(t == 0)
    def _():
        buf_ref[0:halo, :] = cp_ref[0]
        hc_ref[...] = h0_ref[0]

    x = xr_ref[0]
    buf_ref[halo:, :] = x
    taps = [buf_ref[pl.ds(halo - (CONV_WIDTH - 1) + j, tt), :] * cw_ref[j:j + 1, :]
            for j in range(CONV_WIDTH - 1)]
    taps.append(x * cw_ref[CONV_WIDTH - 1:CONV_WIDTH, :])
    acc = taps[0]
    for term in taps[1:]:
        acc = acc + term
    xc = cb_ref[...] + acc

    def store(cs, a, b):
        a_ref[:, cs] = a
        b_ref[:, cs] = b

    _lru_gates(xc, wrg_ref, brg_ref, wig_ref, big_ref, lam_ref, store)

    def group(g, h):
        r0 = pl.multiple_of(g * SUBLANES, SUBLANES)
        a8 = a_ref[pl.ds(r0, SUBLANES), :]
        b8 = b_ref[pl.ds(r0, SUBLANES), :]
        rows = []
        for r in range(SUBLANES):
            h = a8[r:r + 1, :] * h + b8[r:r + 1, :]
            rows.append(h)
        h_ref[pl.ds(r0, SUBLANES), :] = jnp.concatenate(rows, axis=0)
        return h

    h_last = lax.fori_loop(0, tt // SUBLANES, group, hc_ref[...])
    hc_ref[...] = h_last
    hl_ref[0] = h_last
    y_ref[0] = (h_ref[...] * jax.nn.gelu(gr_ref[0])).astype(y_ref.dtype)
    tail = buf_ref[tt:tt + halo, :]
    tail_ref[0] = tail
    buf_ref[0:halo, :] = tail


def _lru_prompt(xr, gr, conv_prev8, h0, cw, cb, wrg, brg, wig, big, lam):
    b, t, d = xr.shape
    tt = min(LRU_TILE, t)
    seq = pl.BlockSpec((1, tt, d), lambda i, j: (i, j, 0))
    per_b = lambda r: pl.BlockSpec((1, r, d), lambda i, j: (i, 0, 0))
    return pl.pallas_call(
        _lru_prompt_kernel,
        grid=(b, t // tt),
        in_specs=[seq, seq, per_b(SUBLANES), per_b(1)] + [_full(v.shape) for v in (cw, cb, wrg, brg, wig, big, lam)],
        out_specs=[seq, per_b(SUBLANES), per_b(1)],
        out_shape=[jax.ShapeDtypeStruct((b, t, d), BF16), jax.ShapeDtypeStruct((b, SUBLANES, d), F32),
                   jax.ShapeDtypeStruct((b, 1, d), F32)],
        scratch_shapes=[pltpu.VMEM((tt + SUBLANES, d), F32), pltpu.VMEM((tt, d), F32),
                        pltpu.VMEM((tt, d), F32), pltpu.VMEM((tt, d), F32), pltpu.VMEM((1, d), F32)],
        compiler_params=_cparams("arbitrary", "arbitrary"),
        name="lru_prompt",
    )(xr, gr, conv_prev8, h0, cw, cb, wrg, brg, wig, big, lam)


def _lru_sample_kernel(xr_ref, gr_ref, c0_ref, c1_ref, c2_ref, h0_ref, cw_ref, cb_ref, wrg_ref, brg_ref,
                       wig_ref, big_ref, lam_ref, y_ref, hn_ref):
    x = xr_ref[...]
    acc = c0_ref[...] * cw_ref[0:1, :]
    acc = acc + c1_ref[...] * cw_ref[1:2, :]
    acc = acc + c2_ref[...] * cw_ref[2:3, :]
    acc = acc + x * cw_ref[3:4, :]
    xc = cb_ref[...] + acc

    def store(cs, a, b):
        hn_ref[:, cs] = a * h0_ref[:, cs] + b

    _lru_gates(xc, wrg_ref, brg_ref, wig_ref, big_ref, lam_ref, store)
    y_ref[...] = (hn_ref[...] * jax.nn.gelu(gr_ref[...])).astype(y_ref.dtype)


def _lru_sample(xr, gr, c0, c1, c2, h0, cw, cb, wrg, brg, wig, big, lam):
    m, d = xr.shape
    args = (xr, gr, c0, c1, c2, h0, cw, cb, wrg, brg, wig, big, lam)
    return pl.pallas_call(
        _lru_sample_kernel,
        grid=(1,),
        in_specs=[_full(v.shape) for v in args],
        out_specs=[_full((m, d)), _full((m, d))],
        out_shape=[jax.ShapeDtypeStruct((m, d), BF16), jax.ShapeDtypeStruct((m, d), F32)],
        compiler_params=_cparams("arbitrary"),
        name="lru_sample",
    )(*args)


def _merge_kernel(attn_ref, y_ref, ga_ref, gb_ref, x_ref, wa_ref, wb_ref, wo_ref, g_ref, x1_ref, xn_ref):
    merged = (jax.nn.sigmoid(ga_ref[...]) * _dot(attn_ref[...].astype(BF16), wa_ref[...])
              + jax.nn.sigmoid(gb_ref[...]) * _dot(y_ref[...], wb_ref[...]))
    x1 = x_ref[...] + _dot(merged.astype(BF16), wo_ref[...])
    x1_ref[...] = x1
    xn_ref[...] = _rmsnorm(x1, g_ref[...]).astype(xn_ref.dtype)


def _merge(attn, y, ga, gb, x, wa, wb, wo, g):
    m, d = x.shape
    tm = min(ROW_TILE, m)
    row = lambda w: pl.BlockSpec((tm, w), lambda i: (i, 0))
    return pl.pallas_call(
        _merge_kernel,
        grid=(m // tm,),
        in_specs=[row(attn.shape[1]), row(d), row(d), row(d), row(d)] + [_full(v.shape) for v in (wa, wb, wo, g)],
        out_specs=[row(d), row(d)],
        out_shape=[jax.ShapeDtypeStruct((m, d), F32), jax.ShapeDtypeStruct((m, d), BF16)],
        compiler_params=_cparams("arbitrary"),
        name="merge",
    )(attn, y, ga, gb, x, wa, wb, wo, g)


def _ffn_kernel(xn_ref, x1_ref, wg_ref, wu_ref, wd_ref, o_ref, acc_ref):
    f = pl.program_id(1)

    @pl.when(f == 0)
    def _():
        acc_ref[...] = jnp.zeros_like(acc_ref)

    xn = xn_ref[...]
    hid = jax.nn.silu(_dot(xn, wg_ref[...])) * _dot(xn, wu_ref[...])
    acc_ref[...] += _dot(hid.astype(BF16), wd_ref[...])

    @pl.when(f == pl.num_programs(1) - 1)
    def _():
        o_ref[...] = x1_ref[...] + acc_ref[...]


def _ffn(xn, x1, wg, wu, wd):
    m, d = x1.shape
    ff = wg.shape[1]
    tm = min(FFN_ROW_TILE, m)
    tf = FFN_COL_TILE
    row = pl.BlockSpec((tm, d), lambda i, f: (i, 0))
    return pl.pallas_call(
        _ffn_kernel,
        grid=(m // tm, ff // tf),
        in_specs=[row, row, pl.BlockSpec((d, tf), lambda i, f: (0, f)), pl.BlockSpec((d, tf), lambda i, f: (0, f)),
                  pl.BlockSpec((tf, d), lambda i, f: (f, 0))],
        out_specs=row,
        out_shape=jax.ShapeDtypeStruct((m, d), F32),
        scratch_shapes=[pltpu.VMEM((tm, d), F32)],
        compiler_params=_cparams("arbitrary", "arbitrary"),
        name="ffn_dense",
    )(xn, x1, wg, wu, wd)


def _router_kernel(xn_ref, wr_ref, gates_ref):
    logits = _dot(xn_ref[...], wr_ref[...])
    lane = lax.broadcasted_iota(jnp.int32, logits.shape, 1)
    logits = jnp.where(lane < N_EXPERTS, logits, -jnp.inf)
    m1 = jnp.max(logits, axis=-1, keepdims=True)
    i1 = jnp.min(jnp.where(logits == m1, lane, LANES), axis=-1, keepdims=True)
    rest = jnp.where(lane == i1, -jnp.inf, logits)
    m2 = jnp.max(rest, axis=-1, keepdims=True)
    i2 = jnp.min(jnp.where(rest == m2, lane, LANES), axis=-1, keepdims=True)
    e2 = jnp.exp(m2 - m1)
    den = 1.0 + e2
    gates_ref[...] = jnp.where(lane == i1, 1.0 / den, 0.0) + jnp.where(lane == i2, e2 / den, 0.0)


def _router(xn, wr_pad):
    m, d = xn.shape
    tm = min(ROW_TILE, m)
    return pl.pallas_call(
        _router_kernel,
        grid=(m // tm,),
        in_specs=[pl.BlockSpec((tm, d), lambda i: (i, 0)), _full(wr_pad.shape)],
        out_specs=pl.BlockSpec((tm, LANES), lambda i: (i, 0)),
        out_shape=jax.ShapeDtypeStruct((m, LANES), F32),
        compiler_params=_cparams("arbitrary"),
        name="router",
    )(xn, wr_pad)


def _moe_kernel(xn_ref, x1_ref, gates_ref, wg_ref, wu_ref, wd_ref, o_ref, acc_ref):
    e = pl.program_id(1)
    f = pl.program_id(2)

    @pl.when((e == 0) & (f == 0))
    def _():
        acc_ref[...] = jnp.zeros_like(acc_ref)

    xn = xn_ref[...]
    hid = jax.nn.silu(_dot(xn, wg_ref[0])) * _dot(xn, wu_ref[0])
    gates = gates_ref[...]
    lane = lax.broadcasted_iota(jnp.int32, gates.shape, 1)
    gate = jnp.sum(jnp.where(lane == e, gates, 0.0), axis=-1, keepdims=True)
    acc_ref[...] += gate * _dot(hid.astype(BF16), wd_ref[0])

    @pl.when((e == pl.num_programs(1) - 1) & (f == pl.num_programs(2) - 1))
    def _():
        o_ref[...] = x1_ref[...] + acc_ref[...]


def _moe(xn, x1, gates, wg, wu, wd):
    m, d = x1.shape
    n_e, _, ff = wg.shape
    tm = min(FFN_ROW_TILE, m)
    tf = FFN_COL_TILE
    row = lambda w: pl.BlockSpec((tm, w), lambda i, e, f: (i, 0))
    return pl.pallas_call(
        _moe_kernel,
        grid=(m // tm, n_e, ff // tf),
        in_specs=[row(d), row(d), row(LANES),
                  pl.BlockSpec((1, d, tf), lambda i, e, f: (e, 0, f)),
                  pl.BlockSpec((1, d, tf), lambda i, e, f: (e, 0, f)),
                  pl.BlockSpec((1, tf, d), lambda i, e, f: (e, f, 0))],
        out_specs=row(d),
        out_shape=jax.ShapeDtypeStruct((m, d), F32),
        scratch_shapes=[pltpu.VMEM((tm, d), F32)],
        compiler_params=_cparams("arbitrary", "arbitrary", "arbitrary"),
        name="moe_dense",
    )(xn, x1, gates, wg, wu, wd)


def _ple_kernel(final, x_ref, p_ref, g_ref, wgate_ref, wproj_ref, gf_ref, o_ref):
    x = x_ref[...]
    gate = jax.nn.sigmoid(_dot(_rmsnorm(x, g_ref[...]).astype(BF16), wgate_ref[...]))
    x3 = x + gate * _dot(p_ref[...].astype(BF16), wproj_ref[...])
    o_ref[...] = _rmsnorm(x3, gf_ref[...]) if final else x3


def _ple(x, p, g, wgate, wproj, g_final, final):
    m, d = x.shape
    tm = min(ROW_TILE, m)
    row = lambda w: pl.BlockSpec((tm, w), lambda i: (i, 0))
    return pl.pallas_call(
        functools.partial(_ple_kernel, final),
        grid=(m // tm,),
        in_specs=[row(d), row(p.shape[1])] + [_full(v.shape) for v in (g, wgate, wproj, g_final)],
        out_specs=row(d),
        out_shape=jax.ShapeDtypeStruct((m, d), F32),
        compiler_params=_cparams("arbitrary"),
        name="ple",
    )(x, p, g, wgate, wproj, g_final)


def _row(v):
    return v.reshape(1, -1)


def kernel(x_prompt, x_sample, p_prompt, p_sample, cache_k, cache_v, cache_logf, state_conv, state_h, page_table, g_mix, w_in, b_f, conv_w, conv_b, w_rg, b_rg, w_ig, b_ig, lru_lambda, w_a_up, w_b_up, w_out, g_ffn, dense_wg, dense_wu, dense_wd, moe_router, moe_wg, moe_wu, moe_wd, g_ple, w_ple_gate, w_ple_proj, g_final):
    depth = g_mix.shape[0]
    b, t, d = x_prompt.shape
    db = x_sample.shape[0]
    n_pool, page = cache_k.shape[1], cache_k.shape[2]
    a = ATTN_WIDTH
    kt_pool = jnp.transpose(cache_k, (0, 1, 3, 4, 2)).reshape(depth * n_pool * N_HEADS, HEAD_DIM, page)
    vt_pool = jnp.transpose(cache_v, (0, 1, 3, 4, 2)).reshape(depth * n_pool * N_HEADS, HEAD_DIM, page)
    lt_pool = jnp.transpose(cache_logf, (0, 1, 3, 2)).reshape(depth * n_pool, N_HEADS, page)

    xp = x_prompt
    xs = x_sample.reshape(db, d)
    outs = {name: [] for name in ("kp", "vp", "lp", "cp", "hp", "ks", "vs", "ls", "cs", "hs")}
    for l in range(depth):
        wl = w_in[l]
        wq = wl[:, 0:a].astype(BF16)
        wk = wl[:, a:2 * a].astype(BF16)
        wv = wl[:, 2 * a:3 * a].astype(BF16)
        wf = jnp.pad(wl[:, 3 * a:3 * a + N_HEADS], ((0, 0), (0, LANES - N_HEADS))).astype(BF16)
        w4 = wl[:, 3 * a + N_HEADS:].astype(BF16)
        bf_row = jnp.pad(b_f[l], (0, LANES - N_HEADS)).reshape(1, LANES)
        g1 = _row(g_mix[l])
        lru_w = (conv_w[l], _row(conv_b[l]), w_rg[l].astype(BF16), _row(b_rg[l]), w_ig[l].astype(BF16),
                 _row(b_ig[l]), _row(lru_lambda[l]))
        wa = w_a_up[l].astype(BF16)
        wb = w_b_up[l].astype(BF16)
        wo = w_out[l].astype(BF16)
        g2 = _row(g_ffn[l])
        g3 = _row(g_ple[l])
        wgate = w_ple_gate[l].astype(BF16)
        wproj = w_ple_proj[l].astype(BF16)
        gf = _row(g_final)
        final = l == depth - 1
        mi = l // 2
        if l % 2 == 0:
            ffn_w = (dense_wg[mi].astype(BF16), dense_wu[mi].astype(BF16), dense_wd[mi].astype(BF16))
        else:
            wr_pad = jnp.pad(moe_router[mi], ((0, 0), (0, LANES - N_EXPERTS))).astype(BF16)
            ffn_w = (moe_wg[mi].astype(BF16), moe_wu[mi].astype(BF16), moe_wd[mi].astype(BF16))

        def channel_mix(xn, x1):
            if l % 2 == 0:
                return _ffn(xn, x1, *ffn_w)
            return _moe(xn, x1, _router(xn, wr_pad), *ffn_w)

        q8, kt, vt, ktb, vtb, lft, ct, xr, gr, ga, gb = _proj_prompt(
            xp, g1, wq, wk.T, wv.T, wf.T, bf_row.reshape(LANES, 1), w4)
        attn = _attn_prompt(q8, ktb, vtb, jnp.transpose(ct, (0, 2, 1)), ct)
        y, tail, hl = _lru_prompt(xr, gr, jnp.zeros((b, SUBLANES, d), F32), jnp.zeros((b, 1, d), F32), *lru_w)
        m = b * t
        x1, xn = _merge(attn.reshape(m, a), y.reshape(m, d), ga.reshape(m, d), gb.reshape(m, d),
                        xp.reshape(m, d), wa, wb, wo, g2)
        x2 = channel_mix(xn, x1)
        xp = _ple(x2, p_prompt[l].reshape(m, -1), g3, wgate, wproj, gf, final).reshape(b, t, d)
        outs["kp"].append(jnp.transpose(kt.reshape(b, N_HEADS, HEAD_DIM, t), (0, 3, 1, 2)))
        outs["vp"].append(jnp.transpose(vt.reshape(b, N_HEADS, HEAD_DIM, t), (0, 3, 1, 2)))
        outs["lp"].append(jnp.transpose(lft, (0, 2, 1)))
        outs["cp"].append(tail[:, SUBLANES - (CONV_WIDTH - 1):, :])
        outs["hp"].append(hl.reshape(b, d))

        qs, ks, vs, lfs, xrs, grs, gas, gbs = _proj_sample(xs, g1, wq, wk, wv, wf, bf_row, w4)
        lf8 = lfs[:, :N_HEADS]
        attn_s = _attn_decode(l, page_table, qs.reshape(db, a, 1), ks.reshape(db, a, 1), vs.reshape(db, a, 1),
                              lf8.reshape(db, N_HEADS, 1), kt_pool, vt_pool, lt_pool, n_pool).reshape(db, a)
        sc = state_conv[l]
        ys, hn = _lru_sample(xrs, grs, sc[:, 0], sc[:, 1], sc[:, 2], state_h[l], *lru_w)
        x1s, xns = _merge(attn_s, ys, gas, gbs, xs, wa, wb, wo, g2)
        x2s = channel_mix(xns, x1s)
        xs = _ple(x2s, p_sample[l].reshape(db, -1), g3, wgate, wproj, gf, final)
        outs["ks"].append(ks.reshape(db, 1, N_HEADS, HEAD_DIM))
        outs["vs"].append(vs.reshape(db, 1, N_HEADS, HEAD_DIM))
        outs["ls"].append(lf8.reshape(db, 1, N_HEADS))
        outs["cs"].append(jnp.stack([sc[:, 1], sc[:, 2], xrs], axis=1))
        outs["hs"].append(hn)

    st = lambda name: jnp.stack(outs[name])
    return (xp, xs.reshape(db, 1, d), st("kp"), st("vp"), st("lp"), st("cp"), st("hp"),
            st("ks"), st("vs"), st("ls"), st("cs"), st("hs"))
```

```python
import functools

import jax
import jax.numpy as jnp
from jax import lax
from jax.experimental import pallas as pl
from jax.experimental.pallas import tpu as pltpu

BF16 = jnp.bfloat16
F32 = jnp.float32

EPS = 1e-6
NEG_INF = -1e30
LRU_C = 8.0
N_HEADS = 8
HEAD_DIM = 64
ATTN_WIDTH = N_HEADS * HEAD_DIM
N_LRU_BLOCKS = 8
CONV_WIDTH = 4
N_EXPERTS = 8

LANES = 128
SUBLANES = 8
VMEM_LIMIT_BYTES = 56 * 1024 * 1024

PROJ_TILE = 256
ATTN_TILE = 512
LRU_TILE = 256
ROW_TILE = 512
FFN_ROW_TILE = 1024
FFN_COL_TILE = 896
MOE_ROW_TILE = 512
MOE_COL_TILE = 1792
DECODE_PAGES = 8


def _cparams(*sem):
    return pltpu.CompilerParams(dimension_semantics=sem, vmem_limit_bytes=VMEM_LIMIT_BYTES)


def _full(shape):
    zeros = (0,) * len(shape)
    return pl.BlockSpec(shape, lambda *_: zeros)


def _rmsnorm(x, g):
    return x * lax.rsqrt(jnp.mean(x * x, axis=-1, keepdims=True) + EPS) * g


def _log_sigmoid(z):
    return jnp.minimum(z, 0.0) - jnp.log1p(jnp.exp(-jnp.abs(z)))


def _softplus(z):
    return jnp.maximum(z, 0.0) + jnp.log1p(jnp.exp(-jnp.abs(z)))


def _dot(a, b):
    return jnp.dot(a, b, preferred_element_type=F32)


def _dot_nt(a, b):
    return lax.dot_general(a, b, (((1,), (1,)), ((), ())), preferred_element_type=F32)


def _exact_dot(x, w01):
    hi = x.astype(BF16)
    r1 = x - hi.astype(F32)
    mid = r1.astype(BF16)
    lo = (r1 - mid.astype(F32)).astype(BF16)
    return _dot(hi, w01) + _dot(mid, w01) + _dot(lo, w01)


def _split3(x):
    p1 = x.astype(BF16).astype(F32)
    r1 = x - p1
    p2 = r1.astype(BF16).astype(F32)
    p3 = (r1 - p2).astype(BF16).astype(F32)
    return p1, p2, p3


def _proj_prompt_kernel(x_ref, g_ref, wq_ref, wkt_ref, wvt_ref, wf_ref, wft_ref, bfr_ref, bfc_ref, w4_ref,
                        triu_ref, tril_ref,
                        qa_ref, kta_ref, kt_ref, vt_ref, vtb_ref, lft_ref,
                        xr_ref, gr_ref, ga_ref, gb_ref, carry_t_ref, carry_c_ref):
    t = pl.program_id(1)

    @pl.when(t == 0)
    def _():
        carry_t_ref[...] = jnp.zeros_like(carry_t_ref)
        carry_c_ref[...] = jnp.zeros_like(carry_c_ref)

    tm = x_ref.shape[1]
    xn = _rmsnorm(x_ref[0], g_ref[...]).astype(BF16)
    lft = _log_sigmoid(_dot_nt(wft_ref[...], xn) + bfc_ref[...])[:N_HEADS]
    lft_ref[0] = lft
    lfc = _log_sigmoid(_dot(xn, wf_ref[...]) + bfr_ref[...])
    ct = _exact_dot(lft, triu_ref[...]) + carry_t_ref[:, :1]
    carry_t_ref[...] = jnp.broadcast_to(ct[:, -1:], carry_t_ref.shape)
    tril = tril_ref[...]
    cc = carry_c_ref[0:1, :]
    for piece in _split3(lfc):
        cc = cc + _dot(tril, piece.astype(BF16))
    carry_c_ref[...] = jnp.broadcast_to(cc[-1:, :], carry_c_ref.shape)

    q = _dot(xn, wq_ref[...]) * (HEAD_DIM ** -0.5)
    kt = _dot_nt(wkt_ref[...], xn)
    kt_ref[0] = kt
    lane = lax.broadcasted_iota(jnp.int32, (tm, HEAD_DIM), 1)
    sub = lax.broadcasted_iota(jnp.int32, (HEAD_DIM, tm), 0)
    for h in range(N_HEADS):
        hs = slice(h * HEAD_DIM, (h + 1) * HEAD_DIM)
        q1, q2, q3 = _split3(cc[:, h:h + 1])
        q_extra = jnp.where(lane == 0, q1, jnp.where(lane == 1, q2, jnp.where(lane == 2, q3,
                            jnp.where(lane < 6, 1.0, 0.0))))
        qa_ref[0, h] = jnp.concatenate([q[:, hs], q_extra], axis=1).astype(BF16)
        k1, k2, k3 = _split3(ct[h:h + 1, :])
        k_extra = jnp.where(sub < 3, 1.0, jnp.where(sub == 3, -k1, jnp.where(sub == 4, -k2,
                            jnp.where(sub == 5, -k3, 0.0))))
        kta_ref[0, 2 * h * HEAD_DIM:2 * (h + 1) * HEAD_DIM, :] = jnp.concatenate(
            [kt[hs, :], k_extra], axis=0).astype(BF16)
    vt = _dot_nt(wvt_ref[...], xn)
    vt_ref[0] = vt
    vtb_ref[0] = vt.astype(BF16)
    d = xr_ref.shape[-1]
    xr_ref[0] = _dot(xn, w4_ref[:, 0 * d:1 * d])
    gr_ref[0] = _dot(xn, w4_ref[:, 1 * d:2 * d])
    ga_ref[0] = _dot(xn, w4_ref[:, 2 * d:3 * d])
    gb_ref[0] = _dot(xn, w4_ref[:, 3 * d:4 * d])


def _proj_prompt(x, g, wq, wkt, wvt, wf, bf_row, w4):
    b, t, d = x.shape
    tm = min(PROJ_TILE, t)
    nt = t // tm
    ri = lax.broadcasted_iota(jnp.int32, (tm, tm), 0)
    ci = lax.broadcasted_iota(jnp.int32, (tm, tm), 1)
    triu = (ri <= ci).astype(BF16)
    tril = (ri >= ci).astype(BF16)
    a = ATTN_WIDTH
    seq = lambda w: pl.BlockSpec((1, tm, w), lambda i, j: (i, j, 0))
    tr = lambda r: pl.BlockSpec((1, r, tm), lambda i, j: (i, 0, j))
    out_shape = [
        jax.ShapeDtypeStruct((b, N_HEADS, t, 2 * HEAD_DIM), BF16),
        jax.ShapeDtypeStruct((b, 2 * a, t), BF16),
        jax.ShapeDtypeStruct((b, a, t), F32), jax.ShapeDtypeStruct((b, a, t), F32),
        jax.ShapeDtypeStruct((b, a, t), BF16),
        jax.ShapeDtypeStruct((b, N_HEADS, t), F32),
    ] + [jax.ShapeDtypeStruct((b, t, d), F32)] * 4
    out_specs = [
        pl.BlockSpec((1, N_HEADS, tm, 2 * HEAD_DIM), lambda i, j: (i, 0, j, 0)),
        tr(2 * a), tr(a), tr(a), tr(a), tr(N_HEADS),
        seq(d), seq(d), seq(d), seq(d),
    ]
    args = (x, g, wq, wkt, wvt, wf, wf.T, bf_row, bf_row.reshape(LANES, 1), w4, triu, tril)
    return pl.pallas_call(
        _proj_prompt_kernel,
        grid=(b, nt),
        in_specs=[seq(d)] + [_full(v.shape) for v in args[1:]],
        out_specs=out_specs,
        out_shape=out_shape,
        scratch_shapes=[pltpu.VMEM((N_HEADS, LANES), F32), pltpu.VMEM((N_HEADS, LANES), F32)],
        compiler_params=_cparams("arbitrary", "arbitrary"),
        name="proj_prompt",
    )(*args)


def _proj_sample_kernel(x_ref, g_ref, wq_ref, wk_ref, wv_ref, wf_ref, bf_ref, w4_ref,
                        q_ref, k_ref, v_ref, lf_ref, xr_ref, gr_ref, ga_ref, gb_ref):
    xn = _rmsnorm(x_ref[...], g_ref[...]).astype(BF16)
    q_ref[...] = _dot(xn, wq_ref[...])
    k_ref[...] = _dot(xn, wk_ref[...])
    v_ref[...] = _dot(xn, wv_ref[...])
    lf_ref[...] = _log_sigmoid(_dot(xn, wf_ref[...]) + bf_ref[...])
    d = xr_ref.shape[-1]
    xr_ref[...] = _dot(xn, w4_ref[:, 0 * d:1 * d])
    gr_ref[...] = _dot(xn, w4_ref[:, 1 * d:2 * d])
    ga_ref[...] = _dot(xn, w4_ref[:, 2 * d:3 * d])
    gb_ref[...] = _dot(xn, w4_ref[:, 3 * d:4 * d])


def _proj_sample(x, g, wq, wk, wv, wf, bf_row, w4):
    m, d = x.shape
    a = ATTN_WIDTH
    shapes = [(m, a), (m, a), (m, a), (m, LANES), (m, d), (m, d), (m, d), (m, d)]
    return pl.pallas_call(
        _proj_sample_kernel,
        grid=(1,),
        in_specs=[_full(v.shape) for v in (x, g, wq, wk, wv, wf, bf_row, w4)],
        out_specs=[_full(s) for s in shapes],
        out_shape=[jax.ShapeDtypeStruct(s, F32) for s in shapes],
        compiler_params=_cparams("arbitrary"),
        name="proj_sample",
    )(x, g, wq, wk, wv, wf, bf_row, w4)


def _attn_prompt_kernel(qi_ref, ki_ref, q_ref, kta_ref, vt_ref, o_ref, m_ref, l_ref, acc_ref):
    step = pl.program_id(1)
    qi = qi_ref[step]
    ki = ki_ref[step]
    ta = q_ref.shape[2]

    @pl.when(ki == 0)
    def _():
        m_ref[...] = jnp.full_like(m_ref, NEG_INF)
        l_ref[...] = jnp.zeros_like(l_ref)
        acc_ref[...] = jnp.zeros_like(acc_ref)

    def update(on_diagonal):
        if on_diagonal:
            mask = (lax.broadcasted_iota(jnp.int32, (ta, ta), 1)
                    <= lax.broadcasted_iota(jnp.int32, (ta, ta), 0))
        for h in range(N_HEADS):
            s = _dot(q_ref[0, h], kta_ref[0, 2 * h * HEAD_DIM:2 * (h + 1) * HEAD_DIM, :])
            if on_diagonal:
                s = jnp.where(mask, s, NEG_INF)
            m_old = m_ref[h]
            m_new = jnp.maximum(m_old, jnp.max(s, axis=-1, keepdims=True))
            p = jnp.exp(s - jnp.concatenate([m_new] * (ta // LANES), axis=1))
            alpha = jnp.exp(m_old - m_new)
            l_ref[h] = alpha * l_ref[h] + jnp.sum(p, axis=-1, keepdims=True)
            pv = _dot_nt(p.astype(BF16), vt_ref[0, h * HEAD_DIM:(h + 1) * HEAD_DIM, :])
            acc_ref[h] = alpha[:, :HEAD_DIM] * acc_ref[h] + pv
            m_ref[h] = m_new

    @pl.when(ki < qi)
    def _():
        update(False)

    @pl.when(ki == qi)
    def _():
        update(True)
        for h in range(0, N_HEADS, 2):
            pair = [acc_ref[h + j] / l_ref[h + j][:, :HEAD_DIM] for j in range(2)]
            o_ref[0, :, h * HEAD_DIM:(h + 2) * HEAD_DIM] = jnp.concatenate(pair, axis=1).astype(o_ref.dtype)


def _attn_prompt(qa, kta, vtb):
    b, _, t, _ = qa.shape
    ta = min(ATTN_TILE, t)
    nq = t // ta
    pairs = [(i, j) for i in range(nq) for j in range(i + 1)]
    qi_tab = jnp.asarray([p[0] for p in pairs], jnp.int32)
    ki_tab = jnp.asarray([p[1] for p in pairs], jnp.int32)
    a = ATTN_WIDTH
    grid_spec = pltpu.PrefetchScalarGridSpec(
        num_scalar_prefetch=2,
        grid=(b, len(pairs)),
        in_specs=[
            pl.BlockSpec((1, N_HEADS, ta, 2 * HEAD_DIM), lambda i, s, qt, kt: (i, 0, qt[s], 0)),
            pl.BlockSpec((1, 2 * a, ta), lambda i, s, qt, kt: (i, 0, kt[s])),
            pl.BlockSpec((1, a, ta), lambda i, s, qt, kt: (i, 0, kt[s])),
        ],
        out_specs=pl.BlockSpec((1, ta, a), lambda i, s, qt, kt: (i, qt[s], 0)),
        scratch_shapes=[pltpu.VMEM((N_HEADS, ta, LANES), F32), pltpu.VMEM((N_HEADS, ta, LANES), F32),
                        pltpu.VMEM((N_HEADS, ta, HEAD_DIM), F32)],
    )
    return pl.pallas_call(
        _attn_prompt_kernel,
        grid_spec=grid_spec,
        out_shape=jax.ShapeDtypeStruct((b, t, a), BF16),
        compiler_params=_cparams("arbitrary", "arbitrary"),
        name="attn_prompt",
    )(qi_tab, ki_tab, qa, kta, vtb)


def _attn_decode_kernel(n_pg, pt_ref, q_ref, kn_ref, vn_ref, lfn_ref, w_ref, *rest):
    k_refs = rest[0:n_pg]
    v_refs = rest[n_pg:2 * n_pg]
    lt_refs = rest[2 * n_pg:3 * n_pg]
    o_ref = rest[3 * n_pg]
    m_ref, l_ref, c_ref, acc_ref = rest[3 * n_pg + 1:]
    c_step = pl.program_id(1)
    scale = HEAD_DIM ** -0.5
    lane = lax.broadcasted_iota(jnp.int32, (1, LANES), 1)

    @pl.when(c_step == 0)
    def _():
        for h in range(N_HEADS):
            hs = slice(h * HEAD_DIM, (h + 1) * HEAD_DIM)
            s0 = jnp.sum(q_ref[0, hs, :] * kn_ref[0, hs, :], axis=0, keepdims=True) * scale
            m_ref[h:h + 1, :] = jnp.broadcast_to(s0, (1, LANES))
            l_ref[h:h + 1, :] = jnp.where(lane == 0, 1.0, 0.0)
            acc_ref[hs, :] = jnp.where(lane == 0, vn_ref[0, hs, :], 0.0)
            c_ref[h:h + 1, :] = jnp.broadcast_to(lfn_ref[0, h:h + 1, :], (1, LANES))

    carry = c_ref[...]
    bias = []
    for i in range(n_pg):
        r = _exact_dot(lt_refs[i][0], w_ref[...])
        bias.append(r[:, :LANES] + carry)
        carry = carry + r[:, LANES:]
    c_ref[...] = carry

    for h in range(N_HEADS):
        hs = slice(h * HEAD_DIM, (h + 1) * HEAD_DIM)
        qh = q_ref[0, hs, :]
        s = [jnp.sum(k_refs[i][h] * qh, axis=0, keepdims=True) * scale + bias[i][h:h + 1, :]
             for i in range(n_pg)]
        m_old = m_ref[h:h + 1, :]
        m_tile = s[0]
        for i in range(1, n_pg):
            m_tile = jnp.maximum(m_tile, s[i])
        m_new = jnp.maximum(m_old, jnp.max(m_tile, axis=1, keepdims=True))
        alpha = jnp.exp(m_old - m_new)
        l_new = alpha * l_ref[h:h + 1, :]
        acc = alpha * acc_ref[hs, :]
        for i in range(n_pg):
            p = jnp.exp(s[i] - m_new)
            l_new = l_new + p
            acc = acc + v_refs[i][h] * p
        m_ref[h:h + 1, :] = m_new
        l_ref[h:h + 1, :] = l_new
        acc_ref[hs, :] = acc

    @pl.when(c_step == pl.num_programs(1) - 1)
    def _():
        for h in range(N_HEADS):
            hs = slice(h * HEAD_DIM, (h + 1) * HEAD_DIM)
            den = jnp.sum(l_ref[h:h + 1, :], axis=1, keepdims=True)
            o_ref[0, hs, :] = jnp.sum(acc_ref[hs, :], axis=1, keepdims=True) / den


def _attn_decode(layer, page_table, q_col, kn_col, vn_col, lfn_col, kt_pool, vt_pool, lt_pool, n_pool):
    db, n_pages = page_table.shape
    n_pg = min(DECODE_PAGES, n_pages)
    n_chunks = n_pages // n_pg
    page = kt_pool.shape[-1]
    base = layer * n_pool
    rowi = lax.broadcasted_iota(jnp.int32, (page, 2 * page), 0)
    coli = lax.broadcasted_iota(jnp.int32, (page, 2 * page), 1)
    w = ((rowi > coli) | (coli >= page)).astype(BF16)

    def page_of(slot):
        def index(b, c, pt):
            return base + pt[b * n_pages + (n_pages - 1 - (c * n_pg + slot))]
        return index

    col = lambda r: pl.BlockSpec((1, r, 1), lambda b, c, pt: (b, 0, 0))
    def page_specs(block):
        return [pl.BlockSpec(block, (lambda b, c, pt, f=page_of(i): (f(b, c, pt), 0, 0))) for i in range(n_pg)]
    grid_spec = pltpu.PrefetchScalarGridSpec(
        num_scalar_prefetch=1,
        grid=(db, n_chunks),
        in_specs=[col(ATTN_WIDTH), col(ATTN_WIDTH), col(ATTN_WIDTH), col(N_HEADS),
                  pl.BlockSpec(w.shape, lambda b, c, pt: (0, 0))]
        + page_specs((N_HEADS, HEAD_DIM, page)) + page_specs((N_HEADS, HEAD_DIM, page))
        + page_specs((1, N_HEADS, page)),
        out_specs=col(ATTN_WIDTH),
        scratch_shapes=[pltpu.VMEM((N_HEADS, LANES), F32), pltpu.VMEM((N_HEADS, LANES), F32),
                        pltpu.VMEM((N_HEADS, LANES), F32), pltpu.VMEM((ATTN_WIDTH, LANES), F32)],
    )
    return pl.pallas_call(
        functools.partial(_attn_decode_kernel, n_pg),
        grid_spec=grid_spec,
        out_shape=jax.ShapeDtypeStruct((db, ATTN_WIDTH, 1), F32),
        compiler_params=_cparams("arbitrary", "arbitrary"),
        name="attn_decode",
    )(page_table.reshape(-1), q_col, kn_col, vn_col, lfn_col, w,
      *([kt_pool] * n_pg), *([vt_pool] * n_pg), *([lt_pool] * n_pg))


def _lru_gates(xc, wrg_ref, brg_ref, wig_ref, big_ref, lam_ref, store):
    xcb = xc.astype(BF16)
    blk = xc.shape[-1] // N_LRU_BLOCKS
    for n in range(N_LRU_BLOCKS):
        cs = slice(n * blk, (n + 1) * blk)
        r = jax.nn.sigmoid(_dot(xcb[:, cs], wrg_ref[n]) + brg_ref[:, cs])
        i = jax.nn.sigmoid(_dot(xcb[:, cs], wig_ref[n]) + big_ref[:, cs])
        log_a = -LRU_C * r * _softplus(-lam_ref[:, cs])
        a = jnp.exp(log_a)
        th = jnp.tanh(log_a)
        mult = jnp.sqrt(-2.0 * th / (1.0 - th))
        store(cs, a, mult * (i * xc[:, cs]))


def _lru_prompt_kernel(xr_ref, gr_ref, cp_ref, h0_ref, cw_ref, cb_ref, wrg_ref, brg_ref, wig_ref,
                       big_ref, lam_ref, y_ref, tail_ref, hl_ref, buf_ref, a_ref, b_ref, h_ref, hc_ref):
    t = pl.program_id(1)
    tt = xr_ref.shape[1]
    halo = SUBLANES

    @pl.when(t == 0)
    def _():
        buf_ref[0:halo, :] = cp_ref[0]
        hc_ref[...] = h0_ref[0]

    x = xr_ref[0]
    buf_ref[halo:, :] = x
    taps = [buf_ref[pl.ds(halo - (CONV_WIDTH - 1) + j, tt), :] * cw_ref[j:j + 1, :]
            for j in range(CONV_WIDTH - 1)]
    taps.append(x * cw_ref[CONV_WIDTH - 1:CONV_WIDTH, :])
    acc = taps[0]
    for term in taps[1:]:
        acc = acc + term
    xc = cb_ref[...] + acc

    def store(cs, a, b):
        a_ref[:, cs] = a
        b_ref[:, cs] = b

    _lru_gates(xc, wrg_ref, brg_ref, wig_ref, big_ref, lam_ref, store)

    def group(g, h):
        r0 = pl.multiple_of(g * SUBLANES, SUBLANES)
        a8 = a_ref[pl.ds(r0, SUBLANES), :]
        b8 = b_ref[pl.ds(r0, SUBLANES), :]
        rows = []
        for r in range(SUBLANES):
            h = a8[r:r + 1, :] * h + b8[r:r + 1, :]
            rows.append(h)
        h_ref[pl.ds(r0, SUBLANES), :] = jnp.concatenate(rows, axis=0)
        return h

    h_last = lax.fori_loop(0, tt // SUBLANES, group, hc_ref[...])
    hc_ref[...] = h_last
    hl_ref[0] = h_last
    y_ref[0] = (h_ref[...] * jax.nn.gelu(gr_ref[0])).astype(y_ref.dtype)
    tail = buf_ref[tt:tt + halo, :]
    tail_ref[0] = tail
    buf_ref[0:halo, :] = tail


def _lru_prompt(xr, gr, conv_prev8, h0, cw, cb, wrg, brg, wig, big, lam):
    b, t, d = xr.shape
    tt = min(LRU_TILE, t)
    seq = pl.BlockSpec((1, tt, d), lambda i, j: (i, j, 0))
    per_b = lambda r: pl.BlockSpec((1, r, d), lambda i, j: (i, 0, 0))
    return pl.pallas_call(
        _lru_prompt_kernel,
        grid=(b, t // tt),
        in_specs=[seq, seq, per_b(SUBLANES), per_b(1)] + [_full(v.shape) for v in (cw, cb, wrg, brg, wig, big, lam)],
        out_specs=[seq, per_b(SUBLANES), per_b(1)],
        out_shape=[jax.ShapeDtypeStruct((b, t, d), BF16), jax.ShapeDtypeStruct((b, SUBLANES, d), F32),
                   jax.ShapeDtypeStruct((b, 1, d), F32)],
        scratch_shapes=[pltpu.VMEM((tt + SUBLANES, d), F32), pltpu.VMEM((tt, d), F32),
                        pltpu.VMEM((tt, d), F32), pltpu.VMEM((tt, d), F32), pltpu.VMEM((1, d), F32)],
        compiler_params=_cparams("arbitrary", "arbitrary"),
        name="lru_prompt",
    )(xr, gr, conv_prev8, h0, cw, cb, wrg, brg, wig, big, lam)


def _lru_sample_kernel(xr_ref, gr_ref, c0_ref, c1_ref, c2_ref, h0_ref, cw_ref, cb_ref, wrg_ref, brg_ref,
                       wig_ref, big_ref, lam_ref, y_ref, hn_ref):
    x = xr_ref[...]
    acc = c0_ref[...] * cw_ref[0:1, :]
    acc = acc + c1_ref[...] * cw_ref[1:2, :]
    acc = acc + c2_ref[...] * cw_ref[2:3, :]
    acc = acc + x * cw_ref[3:4, :]
    xc = cb_ref[...] + acc

    def store(cs, a, b):
        hn_ref[:, cs] = a * h0_ref[:, cs] + b

    _lru_gates(xc, wrg_ref, brg_ref, wig_ref, big_ref, lam_ref, store)
    y_ref[...] = (hn_ref[...] * jax.nn.gelu(gr_ref[...])).astype(y_ref.dtype)


def _lru_sample(xr, gr, c0, c1, c2, h0, cw, cb, wrg, brg, wig, big, lam):
    m, d = xr.shape
    args = (xr, gr, c0, c1, c2, h0, cw, cb, wrg, brg, wig, big, lam)
    return pl.pallas_call(
        _lru_sample_kernel,
        grid=(1,),
        in_specs=[_full(v.shape) for v in args],
        out_specs=[_full((m, d)), _full((m, d))],
        out_shape=[jax.ShapeDtypeStruct((m, d), BF16), jax.ShapeDtypeStruct((m, d), F32)],
        compiler_params=_cparams("arbitrary"),
        name="lru_sample",
    )(*args)


def _merge_kernel(attn_ref, y_ref, ga_ref, gb_ref, x_ref, wa_ref, wb_ref, wo_ref, g_ref, x1_ref, xn_ref):
    merged = (jax.nn.sigmoid(ga_ref[...]) * _dot(attn_ref[...].astype(BF16), wa_ref[...])
              + jax.nn.sigmoid(gb_ref[...]) * _dot(y_ref[...], wb_ref[...]))
    x1 = x_ref[...] + _dot(merged.astype(BF16), wo_ref[...])
    x1_ref[...] = x1
    xn_ref[...] = _rmsnorm(x1, g_ref[...]).astype(xn_ref.dtype)


def _merge(attn, y, ga, gb, x, wa, wb, wo, g):
    m, d = x.shape
    tm = min(ROW_TILE, m)
    row = lambda w: pl.BlockSpec((tm, w), lambda i: (i, 0))
    return pl.pallas_call(
        _merge_kernel,
        grid=(m // tm,),
        in_specs=[row(attn.shape[1]), row(d), row(d), row(d), row(d)] + [_full(v.shape) for v in (wa, wb, wo, g)],
        out_specs=[row(d), row(d)],
        out_shape=[jax.ShapeDtypeStruct((m, d), F32), jax.ShapeDtypeStruct((m, d), BF16)],
        compiler_params=_cparams("arbitrary"),
        name="merge",
    )(attn, y, ga, gb, x, wa, wb, wo, g)


def _top2(logits):
    lane = lax.broadcasted_iota(jnp.int32, logits.shape, 1)
    logits = jnp.where(lane < N_EXPERTS, logits, -jnp.inf)
    m1 = jnp.max(logits, axis=-1, keepdims=True)
    i1 = jnp.min(jnp.where(logits == m1, lane, LANES), axis=-1, keepdims=True)
    rest = jnp.where(lane == i1, -jnp.inf, logits)
    m2 = jnp.max(rest, axis=-1, keepdims=True)
    i2 = jnp.min(jnp.where(rest == m2, lane, LANES), axis=-1, keepdims=True)
    e2 = jnp.exp(m2 - m1)
    den = 1.0 + e2
    return lane, i1, i2, 1.0 / den, e2 / den


def _to_row_tiles(dst_ref, x):
    rows = x.shape[0]
    for s in range(x.shape[1] // LANES):
        dst_ref[pl.ds(s, rows, stride=SUBLANES), :] = x[:, s * LANES:(s + 1) * LANES]


def _from_row_tiles(src_ref, rows, s):
    return src_ref[pl.ds(s, rows, stride=SUBLANES), :]


def _merge_route_kernel(attn_ref, y_ref, ga_ref, gb_ref, x_ref, wa_ref, wb_ref, wo_ref, g_ref, wr_ref, tri_ref,
                        x1_ref, xrt_ref, meta_ref, cnt_ref, carry_ref):
    i = pl.program_id(0)

    @pl.when(i == 0)
    def _():
        carry_ref[...] = jnp.zeros_like(carry_ref)

    merged = (jax.nn.sigmoid(ga_ref[...]) * _dot(attn_ref[...].astype(BF16), wa_ref[...])
              + jax.nn.sigmoid(gb_ref[...]) * _dot(y_ref[...], wb_ref[...]))
    x1 = x_ref[...] + _dot(merged.astype(BF16), wo_ref[...])
    x1_ref[...] = x1
    xn = _rmsnorm(x1, g_ref[...])
    _to_row_tiles(xrt_ref, xn)
    lane, i1, i2, w1, w2 = _top2(_dot(xn.astype(BF16), wr_ref[...]))
    oh1 = lane == i1
    oh2 = lane == i2
    tri = tri_ref[...]
    before1 = _dot(tri, oh1.astype(BF16))
    before2 = _dot(tri, oh2.astype(BF16))
    cnt1 = jnp.sum(oh1.astype(F32), axis=0, keepdims=True)
    cnt2 = jnp.sum(oh2.astype(F32), axis=0, keepdims=True)
    carry = carry_ref[0:1, :]
    rank1 = jnp.sum(jnp.where(oh1, carry + before1, 0.0), axis=-1, keepdims=True)
    rank2 = jnp.sum(jnp.where(oh2, carry + cnt1 + before2, 0.0), axis=-1, keepdims=True)
    total = carry + cnt1 + cnt2
    carry_ref[...] = jnp.broadcast_to(total, carry_ref.shape)
    cnt_ref[...] = jnp.broadcast_to(total, cnt_ref.shape)
    cols = (i1.astype(F32), i2.astype(F32), rank1, rank2, w1, w2)
    meta = jnp.zeros(meta_ref.shape, F32)
    for k, v in enumerate(cols):
        meta = jnp.where(lane == k, v, meta)
    meta_ref[...] = meta


def _merge_route(attn, y, ga, gb, x, wa, wb, wo, g, wr_pad):
    m, d = x.shape
    tm = min(ROW_TILE, m)
    tri = (lax.broadcasted_iota(jnp.int32, (tm, tm), 0)
           > lax.broadcasted_iota(jnp.int32, (tm, tm), 1)).astype(BF16)
    row = lambda w: pl.BlockSpec((tm, w), lambda i: (i, 0))
    return pl.pallas_call(
        _merge_route_kernel,
        grid=(m // tm,),
        in_specs=[row(attn.shape[1]), row(d), row(d), row(d), row(d)]
        + [_full(v.shape) for v in (wa, wb, wo, g, wr_pad, tri)],
        out_specs=[row(d), pl.BlockSpec((tm * SUBLANES, LANES), lambda i: (i, 0)), row(LANES),
                   _full((SUBLANES, LANES))],
        out_shape=[jax.ShapeDtypeStruct((m, d), F32), jax.ShapeDtypeStruct((m * SUBLANES, LANES), F32),
                   jax.ShapeDtypeStruct((m, LANES), F32), jax.ShapeDtypeStruct((SUBLANES, LANES), F32)],
        scratch_shapes=[pltpu.VMEM((SUBLANES, LANES), F32)],
        compiler_params=_cparams("arbitrary"),
        name="merge_route",
    )(attn, y, ga, gb, x, wa, wb, wo, g, wr_pad, tri)


def _row_copy(src_ref, src_row, dst_ref, dst_row, sem):
    return pltpu.make_async_copy(
        src_ref.at[pl.ds(pl.multiple_of(src_row * SUBLANES, SUBLANES), SUBLANES)],
        dst_ref.at[pl.ds(pl.multiple_of(dst_row * SUBLANES, SUBLANES), SUBLANES)], sem)


def _dispatch_kernel(d1_ref, d2_ref, x_ref, init_ref, xs_ref, sem):
    del init_ref
    tm = x_ref.shape[0] // SUBLANES
    base = pl.program_id(0) * tm

    def issue(r, carry):
        _row_copy(x_ref, r, xs_ref, d1_ref[base + r], sem).start()
        _row_copy(x_ref, r, xs_ref, d2_ref[base + r], sem).start()
        return carry

    lax.fori_loop(0, tm, issue, 0)

    def drain(r, carry):
        _row_copy(x_ref, r, xs_ref, d1_ref[base + r], sem).wait()
        _row_copy(x_ref, r, xs_ref, d2_ref[base + r], sem).wait()
        return carry

    lax.fori_loop(0, tm, drain, 0)


def _dispatch(dest1, dest2, x_rt, n_rows):
    m = dest1.shape[0]
    tm = min(ROW_TILE, m)
    grid_spec = pltpu.PrefetchScalarGridSpec(
        num_scalar_prefetch=2,
        grid=(m // tm,),
        in_specs=[pl.BlockSpec((tm * SUBLANES, LANES), lambda i, d1, d2: (i, 0)),
                  pl.BlockSpec(memory_space=pl.ANY)],
        out_specs=pl.BlockSpec(memory_space=pl.ANY),
        scratch_shapes=[pltpu.SemaphoreType.DMA],
    )
    return pl.pallas_call(
        _dispatch_kernel,
        grid_spec=grid_spec,
        out_shape=jax.ShapeDtypeStruct((n_rows * SUBLANES, LANES), F32),
        input_output_aliases={3: 0},
        compiler_params=_cparams("arbitrary"),
        name="moe_dispatch",
    )(dest1, dest2, x_rt, jnp.zeros((n_rows * SUBLANES, LANES), F32))


def _moe_ffn_kernel(te_ref, nu_ref, xs_ref, wg_ref, wu_ref, wd_ref, ys_ref, xb_ref, acc_ref):
    i = pl.program_id(0)
    f = pl.program_id(1)
    rows, d = xb_ref.shape

    @pl.when(i < nu_ref[0])
    def _():
        @pl.when(f == 0)
        def _():
            for s in range(d // LANES):
                xb_ref[:, s * LANES:(s + 1) * LANES] = _from_row_tiles(xs_ref, rows, s).astype(BF16)
            acc_ref[...] = jnp.zeros_like(acc_ref)

        xb = xb_ref[...]
        hid = jax.nn.silu(_dot(xb, wg_ref[0])) * _dot(xb, wu_ref[0])
        acc_ref[...] += _dot(hid.astype(BF16), wd_ref[0])

        @pl.when(f == pl.num_programs(1) - 1)
        def _():
            _to_row_tiles(ys_ref, acc_ref[...])

    @pl.when((i >= nu_ref[0]) & (f == pl.num_programs(1) - 1))
    def _():
        ys_ref[...] = jnp.zeros_like(ys_ref)


def _moe_ffn(tile_expert, n_used, xs, wg, wu, wd):
    n_e, d, ff = wg.shape
    tm = MOE_ROW_TILE
    tf = MOE_COL_TILE
    nf = ff // tf
    n_tiles = tile_expert.shape[0]

    def tile(i, f, te, nu):
        return (i, 0)

    def col(i, f, te, nu):
        return jnp.where(i < nu[0], f, nf - 1)

    grid_spec = pltpu.PrefetchScalarGridSpec(
        num_scalar_prefetch=2,
        grid=(n_tiles, nf),
        in_specs=[pl.BlockSpec((tm * SUBLANES, LANES), tile),
                  pl.BlockSpec((1, d, tf), lambda i, f, te, nu: (te[i], 0, col(i, f, te, nu))),
                  pl.BlockSpec((1, d, tf), lambda i, f, te, nu: (te[i], 0, col(i, f, te, nu))),
                  pl.BlockSpec((1, tf, d), lambda i, f, te, nu: (te[i], col(i, f, te, nu), 0))],
        out_specs=pl.BlockSpec((tm * SUBLANES, LANES), tile),
        scratch_shapes=[pltpu.VMEM((tm, d), BF16), pltpu.VMEM((tm, d), F32)],
    )
    return pl.pallas_call(
        _moe_ffn_kernel,
        grid_spec=grid_spec,
        out_shape=jax.ShapeDtypeStruct(xs.shape, F32),
        compiler_params=_cparams("arbitrary", "arbitrary"),
        name="moe_ffn",
    )(tile_expert, n_used, xs, wg, wu, wd)


def _combine_kernel(d1_ref, d2_ref, x1_ref, meta_ref, ys_ref, o_ref, g1_ref, g2_ref, sem):
    tm, d = x1_ref.shape
    base = pl.program_id(0) * tm

    def issue(r, carry):
        _row_copy(ys_ref, d1_ref[base + r], g1_ref, r, sem).start()
        _row_copy(ys_ref, d2_ref[base + r], g2_ref, r, sem).start()
        return carry

    lax.fori_loop(0, tm, issue, 0)

    def drain(r, carry):
        _row_copy(ys_ref, d1_ref[base + r], g1_ref, r, sem).wait()
        _row_copy(ys_ref, d2_ref[base + r], g2_ref, r, sem).wait()
        return carry

    lax.fori_loop(0, tm, drain, 0)
    w1 = meta_ref[:, 4:5]
    w2 = meta_ref[:, 5:6]
    for s in range(d // LANES):
        cs = slice(s * LANES, (s + 1) * LANES)
        o_ref[:, cs] = (x1_ref[:, cs] + w1 * _from_row_tiles(g1_ref, tm, s)
                        + w2 * _from_row_tiles(g2_ref, tm, s))


def _combine(dest1, dest2, x1, meta, ys):
    m, d = x1.shape
    tm = min(ROW_TILE, m)
    grid_spec = pltpu.PrefetchScalarGridSpec(
        num_scalar_prefetch=2,
        grid=(m // tm,),
        in_specs=[pl.BlockSpec((tm, d), lambda i, d1, d2: (i, 0)),
                  pl.BlockSpec((tm, LANES), lambda i, d1, d2: (i, 0)),
                  pl.BlockSpec(memory_space=pl.ANY)],
        out_specs=pl.BlockSpec((tm, d), lambda i, d1, d2: (i, 0)),
        scratch_shapes=[pltpu.VMEM((tm * SUBLANES, LANES), F32), pltpu.VMEM((tm * SUBLANES, LANES), F32),
                        pltpu.SemaphoreType.DMA],
    )
    return pl.pallas_call(
        _combine_kernel,
        grid_spec=grid_spec,
        out_shape=jax.ShapeDtypeStruct((m, d), F32),
        compiler_params=_cparams("arbitrary"),
        name="moe_combine",
    )(dest1, dest2, x1, meta, ys)


def _moe_sorted(x1, x_rt, meta, counts, wg, wu, wd):
    m = x1.shape[0]
    tm = MOE_ROW_TILE
    n_tiles = (2 * m) // tm + N_EXPERTS
    cnt = counts[0, :N_EXPERTS].astype(jnp.int32)
    padded = ((cnt + tm - 1) // tm) * tm
    ends = jnp.cumsum(padded)
    starts = ends - padded
    experts = jnp.arange(N_EXPERTS, dtype=jnp.int32)[None, :]

    def dest(expert_col, rank_col):
        e = meta[:, expert_col].astype(jnp.int32)[:, None]
        return jnp.sum(jnp.where(e == experts, starts[None, :], 0), axis=1) + meta[:, rank_col].astype(jnp.int32)

    dest1 = dest(0, 2)
    dest2 = dest(1, 3)
    n_used = (ends[-1] // tm).reshape(1)
    tile_start = jnp.minimum(jnp.arange(n_tiles, dtype=jnp.int32), n_used - 1) * tm
    tile_expert = jnp.sum(ends[None, :] <= tile_start[:, None], axis=1).astype(jnp.int32)
    xs = _dispatch(dest1, dest2, x_rt, n_tiles * tm)
    ys = _moe_ffn(tile_expert, n_used.astype(jnp.int32), xs, wg, wu, wd)
    return _combine(dest1, dest2, x1, meta, ys)


def _ffn_kernel(xn_ref, x1_ref, wg_ref, wu_ref, wd_ref, o_ref, acc_ref):
    f = pl.program_id(1)

    @pl.when(f == 0)
    def _():
        acc_ref[...] = jnp.zeros_like(acc_ref)

    xn = xn_ref[...]
    hid = jax.nn.silu(_dot(xn, wg_ref[...])) * _dot(xn, wu_ref[...])
    acc_ref[...] += _dot(hid.astype(BF16), wd_ref[...])

    @pl.when(f == pl.num_programs(1) - 1)
    def _():
        o_ref[...] = x1_ref[...] + acc_ref[...]


def _ffn(xn, x1, wg, wu, wd):
    m, d = x1.shape
    ff = wg.shape[1]
    tm = min(FFN_ROW_TILE, m)
    tf = FFN_COL_TILE
    row = pl.BlockSpec((tm, d), lambda i, f: (i, 0))
    return pl.pallas_call(
        _ffn_kernel,
        grid=(m // tm, ff // tf),
        in_specs=[row, row, pl.BlockSpec((d, tf), lambda i, f: (0, f)), pl.BlockSpec((d, tf), lambda i, f: (0, f)),
                  pl.BlockSpec((tf, d), lambda i, f: (f, 0))],
        out_specs=row,
        out_shape=jax.ShapeDtypeStruct((m, d), F32),
        scratch_shapes=[pltpu.VMEM((tm, d), F32)],
        compiler_params=_cparams("arbitrary", "arbitrary"),
        name="ffn_dense",
    )(xn, x1, wg, wu, wd)


def _router_kernel(xn_ref, wr_ref, gates_ref):
    lane, i1, i2, w1, w2 = _top2(_dot(xn_ref[...], wr_ref[...]))
    gates_ref[...] = jnp.where(lane == i1, w1, 0.0) + jnp.where(lane == i2, w2, 0.0)


def _router(xn, wr_pad):
    m, d = xn.shape
    tm = min(ROW_TILE, m)
    return pl.pallas_call(
        _router_kernel,
        grid=(m // tm,),
        in_specs=[pl.BlockSpec((tm, d), lambda i: (i, 0)), _full(wr_pad.shape)],
        out_specs=pl.BlockSpec((tm, LANES), lambda i: (i, 0)),
        out_shape=jax.ShapeDtypeStruct((m, LANES), F32),
        compiler_params=_cparams("arbitrary"),
        name="router",
    )(xn, wr_pad)


def _moe_kernel(xn_ref, x1_ref, gates_ref, wg_ref, wu_ref, wd_ref, o_ref, acc_ref):
    e = pl.program_id(1)
    f = pl.program_id(2)

    @pl.when((e == 0) & (f == 0))
    def _():
        acc_ref[...] = jnp.zeros_like(acc_ref)

    xn = xn_ref[...]
    hid = jax.nn.silu(_dot(xn, wg_ref[0])) * _dot(xn, wu_ref[0])
    gates = gates_ref[...]
    lane = lax.broadcasted_iota(jnp.int32, gates.shape, 1)
    gate = jnp.sum(jnp.where(lane == e, gates, 0.0), axis=-1, keepdims=True)
    acc_ref[...] += gate * _dot(hid.astype(BF16), wd_ref[0])

    @pl.when((e == pl.num_programs(1) - 1) & (f == pl.num_programs(2) - 1))
    def _():
        o_ref[...] = x1_ref[...] + acc_ref[...]


def _moe(xn, x1, gates, wg, wu, wd):
    m, d = x1.shape
    n_e, _, ff = wg.shape
    tm = min(FFN_ROW_TILE, m)
    tf = FFN_COL_TILE
    row = lambda w: pl.BlockSpec((tm, w), lambda i, e, f: (i, 0))
    return pl.pallas_call(
        _moe_kernel,
        grid=(m // tm, n_e, ff // tf),
        in_specs=[row(d), row(d), row(LANES),
                  pl.BlockSpec((1, d, tf), lambda i, e, f: (e, 0, f)),
                  pl.BlockSpec((1, d, tf), lambda i, e, f: (e, 0, f)),
                  pl.BlockSpec((1, tf, d), lambda i, e, f: (e, f, 0))],
        out_specs=row(d),
        out_shape=jax.ShapeDtypeStruct((m, d), F32),
        scratch_shapes=[pltpu.VMEM((tm, d), F32)],
        compiler_params=_cparams("arbitrary", "arbitrary", "arbitrary"),
        name="moe_dense",
    )(xn, x1, gates, wg, wu, wd)


def _ple_kernel(final, x_ref, p_ref, g_ref, wgate_ref, wproj_ref, gf_ref, o_ref):
    x = x_ref[...]
    gate = jax.nn.sigmoid(_dot(_rmsnorm(x, g_ref[...]).astype(BF16), wgate_ref[...]))
    x3 = x + gate * _dot(p_ref[...].astype(BF16), wproj_ref[...])
    o_ref[...] = _rmsnorm(x3, gf_ref[...]) if final else x3


def _ple(x, p, g, wgate, wproj, g_final, final):
    m, d = x.shape
    tm = min(ROW_TILE, m)
    row = lambda w: pl.BlockSpec((tm, w), lambda i: (i, 0))
    return pl.pallas_call(
        functools.partial(_ple_kernel, final),
        grid=(m // tm,),
        in_specs=[row(d), row(p.shape[1])] + [_full(v.shape) for v in (g, wgate, wproj, g_final)],
        out_specs=row(d),
        out_shape=jax.ShapeDtypeStruct((m, d), F32),
        compiler_params=_cparams("arbitrary"),
        name="ple",
    )(x, p, g, wgate, wproj, g_final)


def _row(v):
    return v.reshape(1, -1)


def kernel(x_prompt, x_sample, p_prompt, p_sample, cache_k, cache_v, cache_logf, state_conv, state_h, page_table, g_mix, w_in, b_f, conv_w, conv_b, w_rg, b_rg, w_ig, b_ig, lru_lambda, w_a_up, w_b_up, w_out, g_ffn, dense_wg, dense_wu, dense_wd, moe_router, moe_wg, moe_wu, moe_wd, g_ple, w_ple_gate, w_ple_proj, g_final):
    depth = g_mix.shape[0]
    b, t, d = x_prompt.shape
    db = x_sample.shape[0]
    n_pool, page = cache_k.shape[1], cache_k.shape[2]
    a = ATTN_WIDTH
    kt_pool = jnp.transpose(cache_k, (0, 1, 3, 4, 2)).reshape(depth * n_pool * N_HEADS, HEAD_DIM, page)
    vt_pool = jnp.transpose(cache_v, (0, 1, 3, 4, 2)).reshape(depth * n_pool * N_HEADS, HEAD_DIM, page)
    lt_pool = jnp.transpose(cache_logf, (0, 1, 3, 2)).reshape(depth * n_pool, N_HEADS, page)

    xp = x_prompt
    xs = x_sample.reshape(db, d)
    outs = {name: [] for name in ("kp", "vp", "lp", "cp", "hp", "ks", "vs", "ls", "cs", "hs")}
    for l in range(depth):
        wl = w_in[l]
        wq = wl[:, 0:a].astype(BF16)
        wk = wl[:, a:2 * a].astype(BF16)
        wv = wl[:, 2 * a:3 * a].astype(BF16)
        wf = jnp.pad(wl[:, 3 * a:3 * a + N_HEADS], ((0, 0), (0, LANES - N_HEADS))).astype(BF16)
        w4 = wl[:, 3 * a + N_HEADS:].astype(BF16)
        bf_row = jnp.pad(b_f[l], (0, LANES - N_HEADS)).reshape(1, LANES)
        g1 = _row(g_mix[l])
        lru_w = (conv_w[l], _row(conv_b[l]), w_rg[l].astype(BF16), _row(b_rg[l]), w_ig[l].astype(BF16),
                 _row(b_ig[l]), _row(lru_lambda[l]))
        wa = w_a_up[l].astype(BF16)
        wb = w_b_up[l].astype(BF16)
        wo = w_out[l].astype(BF16)
        g2 = _row(g_ffn[l])
        g3 = _row(g_ple[l])
        wgate = w_ple_gate[l].astype(BF16)
        wproj = w_ple_proj[l].astype(BF16)
        gf = _row(g_final)
        final = l == depth - 1
        mi = l // 2
        if l % 2 == 0:
            ffn_w = (dense_wg[mi].astype(BF16), dense_wu[mi].astype(BF16), dense_wd[mi].astype(BF16))
        else:
            wr_pad = jnp.pad(moe_router[mi], ((0, 0), (0, LANES - N_EXPERTS))).astype(BF16)
            ffn_w = (moe_wg[mi].astype(BF16), moe_wu[mi].astype(BF16), moe_wd[mi].astype(BF16))

        qa, kta, kt, vt, vtb, lft, xr, gr, ga, gb = _proj_prompt(xp, g1, wq, wk.T, wv.T, wf, bf_row, w4)
        attn = _attn_prompt(qa, kta, vtb)
        y, tail, hl = _lru_prompt(xr, gr, jnp.zeros((b, SUBLANES, d), F32), jnp.zeros((b, 1, d), F32), *lru_w)
        m = b * t
        merge_in = (attn.reshape(m, a), y.reshape(m, d), ga.reshape(m, d), gb.reshape(m, d),
                    xp.reshape(m, d), wa, wb, wo, g2)
        if l % 2 == 0:
            x1, xn = _merge(*merge_in)
            x2 = _ffn(xn, x1, *ffn_w)
        else:
            x1, x_rt, meta, counts = _merge_route(*merge_in, wr_pad)
            x2 = _moe_sorted(x1, x_rt, meta, counts, *ffn_w)
        xp = _ple(x2, p_prompt[l].reshape(m, -1), g3, wgate, wproj, gf, final).reshape(b, t, d)
        outs["kp"].append(jnp.transpose(kt.reshape(b, N_HEADS, HEAD_DIM, t), (0, 3, 1, 2)))
        outs["vp"].append(jnp.transpose(vt.reshape(b, N_HEADS, HEAD_DIM, t), (0, 3, 1, 2)))
        outs["lp"].append(jnp.transpose(lft, (0, 2, 1)))
        outs["cp"].append(tail[:, SUBLANES - (CONV_WIDTH - 1):, :])
        outs["hp"].append(hl.reshape(b, d))

        qs, ks, vs, lfs, xrs, grs, gas, gbs = _proj_sample(xs, g1, wq, wk, wv, wf, bf_row, w4)
        lf8 = lfs[:, :N_HEADS]
        attn_s = _attn_decode(l, page_table, qs.reshape(db, a, 1), ks.reshape(db, a, 1), vs.reshape(db, a, 1),
                              lf8.reshape(db, N_HEADS, 1), kt_pool, vt_pool, lt_pool, n_pool).reshape(db, a)
        sc = state_conv[l]
        ys, hn = _lru_sample(xrs, grs, sc[:, 0], sc[:, 1], sc[:, 2], state_h[l], *lru_w)
        x1s, xns = _merge(attn_s, ys, gas, gbs, xs, wa, wb, wo, g2)
        if l % 2 == 0:
            x2s = _ffn(xns, x1s, *ffn_w)
        else:
            x2s = _moe(xns, x1s, _router(xns, wr_pad), *ffn_w)
        xs = _ple(x2s, p_sample[l].reshape(db, -1), g3, wgate, wproj, gf, final)
        outs["ks"].append(ks.reshape(db, 1, N_HEADS, HEAD_DIM))
        outs["vs"].append(vs.reshape(db, 1, N_HEADS, HEAD_DIM))
        outs["ls"].append(lf8.reshape(db, 1, N_HEADS))
        outs["cs"].append(jnp.stack([sc[:, 1], sc[:, 2], xrs], axis=1))
        outs["hs"].append(hn)

    st = lambda name: jnp.stack(outs[name])
    return (xp, xs.reshape(db, 1, d), st("kp"), st("vp"), st("lp"), st("cp"), st("hp"),
            st("ks"), st("vs"), st("ls"), st("cs"), st("hs"))
```

```python
import functools

import jax
import jax.numpy as jnp
from jax import lax
from jax.experimental import pallas as pl
from jax.experimental.pallas import tpu as pltpu

BF16 = jnp.bfloat16
F32 = jnp.float32

EPS = 1e-6
NEG_INF = -1e30
LRU_C = 8.0
N_HEADS = 8
HEAD_DIM = 64
ATTN_WIDTH = N_HEADS * HEAD_DIM
N_LRU_BLOCKS = 8
CONV_WIDTH = 4
N_EXPERTS = 8

LANES = 128
SUBLANES = 8
VMEM_LIMIT_BYTES = 56 * 1024 * 1024

PROJ_TILE = 512
ATTN_TILE = 512
LRU_TILE = 256
ROW_TILE = 512
FFN_ROW_TILE = 1024
FFN_COL_TILE = 896
MOE_ROW_TILE = 512
MOE_COL_TILE = 1792
DECODE_PAGES = 16


def _cparams(*sem):
    return pltpu.CompilerParams(dimension_semantics=sem, vmem_limit_bytes=VMEM_LIMIT_BYTES)


def _full(shape):
    zeros = (0,) * len(shape)
    return pl.BlockSpec(shape, lambda *_: zeros, pipeline_mode=pl.Buffered(1))


def _rmsnorm(x, g):
    return x * lax.rsqrt(jnp.mean(x * x, axis=-1, keepdims=True) + EPS) * g


def _log_sigmoid(z):
    return jnp.minimum(z, 0.0) - jnp.log1p(jnp.exp(-jnp.abs(z)))


def _softplus(z):
    return jnp.maximum(z, 0.0) + jnp.log1p(jnp.exp(-jnp.abs(z)))


def _dot(a, b):
    return jnp.dot(a, b, preferred_element_type=F32)


def _dot_nt(a, b):
    return lax.dot_general(a, b, (((1,), (1,)), ((), ())), preferred_element_type=F32)


def _exact_dot(x, w01):
    hi = x.astype(BF16)
    r1 = x - hi.astype(F32)
    mid = r1.astype(BF16)
    lo = (r1 - mid.astype(F32)).astype(BF16)
    return _dot(hi, w01) + _dot(mid, w01) + _dot(lo, w01)


def _split3(x):
    p1 = x.astype(BF16).astype(F32)
    r1 = x - p1
    p2 = r1.astype(BF16).astype(F32)
    p3 = (r1 - p2).astype(BF16).astype(F32)
    return p1, p2, p3


def _proj_prompt_kernel(x_ref, g_ref, wq_ref, wkt_ref, wvt_ref, wf_ref, wft_ref, bfr_ref, bfc_ref, w4_ref,
                        triu_ref, tril_ref,
                        qa_ref, kta_ref, kt_ref, vt_ref, vtb_ref, lft_ref,
                        xr_ref, gr_ref, ga_ref, gb_ref, carry_t_ref, carry_c_ref):
    t = pl.program_id(1)

    @pl.when(t == 0)
    def _():
        carry_t_ref[...] = jnp.zeros_like(carry_t_ref)
        carry_c_ref[...] = jnp.zeros_like(carry_c_ref)

    tm = x_ref.shape[1]
    xn = _rmsnorm(x_ref[0], g_ref[...]).astype(BF16)
    lft = _log_sigmoid(_dot_nt(wft_ref[...], xn) + bfc_ref[...])[:N_HEADS]
    lft_ref[0] = lft
    lfc = _log_sigmoid(_dot(xn, wf_ref[...]) + bfr_ref[...])
    ct = _exact_dot(lft, triu_ref[...]) + carry_t_ref[:, :1]
    carry_t_ref[...] = jnp.broadcast_to(ct[:, -1:], carry_t_ref.shape)
    tril = tril_ref[...]
    cc = carry_c_ref[0:1, :]
    for piece in _split3(lfc):
        cc = cc + _dot(tril, piece.astype(BF16))
    carry_c_ref[...] = jnp.broadcast_to(cc[-1:, :], carry_c_ref.shape)

    q = _dot(xn, wq_ref[...]) * (HEAD_DIM ** -0.5)
    kt = _dot_nt(wkt_ref[...], xn)
    kt_ref[0] = kt
    lane = lax.broadcasted_iota(jnp.int32, (tm, HEAD_DIM), 1)
    sub = lax.broadcasted_iota(jnp.int32, (HEAD_DIM, tm), 0)
    for h in range(N_HEADS):
        hs = slice(h * HEAD_DIM, (h + 1) * HEAD_DIM)
        q1, q2, q3 = _split3(cc[:, h:h + 1])
        q_extra = jnp.where(lane == 0, q1, jnp.where(lane == 1, q2, jnp.where(lane == 2, q3,
                            jnp.where(lane < 6, 1.0, 0.0))))
        qa_ref[0, h] = jnp.concatenate([q[:, hs], q_extra], axis=1).astype(BF16)
        k1, k2, k3 = _split3(ct[h:h + 1, :])
        k_extra = jnp.where(sub < 3, 1.0, jnp.where(sub == 3, -k1, jnp.where(sub == 4, -k2,
                            jnp.where(sub == 5, -k3, 0.0))))
        kta_ref[0, 2 * h * HEAD_DIM:2 * (h + 1) * HEAD_DIM, :] = jnp.concatenate(
            [kt[hs, :], k_extra], axis=0).astype(BF16)
    vt = _dot_nt(wvt_ref[...], xn)
    vt_ref[0] = vt
    vtb_ref[0] = vt.astype(BF16)
    d = xr_ref.shape[-1]
    xr_ref[0] = _dot(xn, w4_ref[:, 0 * d:1 * d])
    gr_ref[0] = _dot(xn, w4_ref[:, 1 * d:2 * d])
    ga_ref[0] = _dot(xn, w4_ref[:, 2 * d:3 * d])
    gb_ref[0] = _dot(xn, w4_ref[:, 3 * d:4 * d])


def _proj_prompt(x, g, wq, wkt, wvt, wf, bf_row, w4):
    b, t, d = x.shape
    tm = min(PROJ_TILE, t)
    nt = t // tm
    ri = lax.broadcasted_iota(jnp.int32, (tm, tm), 0)
    ci = lax.broadcasted_iota(jnp.int32, (tm, tm), 1)
    triu = (ri <= ci).astype(BF16)
    tril = (ri >= ci).astype(BF16)
    a = ATTN_WIDTH
    seq = lambda w: pl.BlockSpec((1, tm, w), lambda i, j: (i, j, 0))
    tr = lambda r: pl.BlockSpec((1, r, tm), lambda i, j: (i, 0, j))
    out_shape = [
        jax.ShapeDtypeStruct((b, N_HEADS, t, 2 * HEAD_DIM), BF16),
        jax.ShapeDtypeStruct((b, 2 * a, t), BF16),
        jax.ShapeDtypeStruct((b, a, t), F32), jax.ShapeDtypeStruct((b, a, t), F32),
        jax.ShapeDtypeStruct((b, a, t), BF16),
        jax.ShapeDtypeStruct((b, N_HEADS, t), F32),
    ] + [jax.ShapeDtypeStruct((b, t, d), F32)] * 4
    out_specs = [
        pl.BlockSpec((1, N_HEADS, tm, 2 * HEAD_DIM), lambda i, j: (i, 0, j, 0)),
        tr(2 * a), tr(a), tr(a), tr(a), tr(N_HEADS),
        seq(d), seq(d), seq(d), seq(d),
    ]
    args = (x, g, wq, wkt, wvt, wf, wf.T, bf_row, bf_row.reshape(LANES, 1), w4, triu, tril)
    return pl.pallas_call(
        _proj_prompt_kernel,
        grid=(b, nt),
        in_specs=[seq(d)] + [_full(v.shape) for v in args[1:]],
        out_specs=out_specs,
        out_shape=out_shape,
        scratch_shapes=[pltpu.VMEM((N_HEADS, LANES), F32), pltpu.VMEM((N_HEADS, LANES), F32)],
        compiler_params=_cparams("arbitrary", "arbitrary"),
        name="proj_prompt",
    )(*args)


def _proj_sample_kernel(x_ref, g_ref, wq_ref, wk_ref, wv_ref, wf_ref, bf_ref, w4_ref,
                        q_ref, k_ref, v_ref, lf_ref, xr_ref, gr_ref, ga_ref, gb_ref):
    xn = _rmsnorm(x_ref[...], g_ref[...]).astype(BF16)
    q_ref[...] = _dot(xn, wq_ref[...])
    k_ref[...] = _dot(xn, wk_ref[...])
    v_ref[...] = _dot(xn, wv_ref[...])
    lf_ref[...] = _log_sigmoid(_dot(xn, wf_ref[...]) + bf_ref[...])
    d = xr_ref.shape[-1]
    xr_ref[...] = _dot(xn, w4_ref[:, 0 * d:1 * d])
    gr_ref[...] = _dot(xn, w4_ref[:, 1 * d:2 * d])
    ga_ref[...] = _dot(xn, w4_ref[:, 2 * d:3 * d])
    gb_ref[...] = _dot(xn, w4_ref[:, 3 * d:4 * d])


def _proj_sample(x, g, wq, wk, wv, wf, bf_row, w4):
    m, d = x.shape
    a = ATTN_WIDTH
    shapes = [(m, a), (m, a), (m, a), (m, LANES), (m, d), (m, d), (m, d), (m, d)]
    return pl.pallas_call(
        _proj_sample_kernel,
        grid=(1,),
        in_specs=[_full(v.shape) for v in (x, g, wq, wk, wv, wf, bf_row, w4)],
        out_specs=[_full(s) for s in shapes],
        out_shape=[jax.ShapeDtypeStruct(s, F32) for s in shapes],
        compiler_params=_cparams("arbitrary"),
        name="proj_sample",
    )(x, g, wq, wk, wv, wf, bf_row, w4)


def _attn_prompt_kernel(qi_ref, ki_ref, q_ref, kta_ref, vt_ref, o_ref, m_ref, l_ref, acc_ref):
    step = pl.program_id(1)
    qi = qi_ref[step]
    ki = ki_ref[step]
    ta = q_ref.shape[2]

    @pl.when(ki == 0)
    def _():
        m_ref[...] = jnp.full_like(m_ref, NEG_INF)
        l_ref[...] = jnp.zeros_like(l_ref)
        acc_ref[...] = jnp.zeros_like(acc_ref)

    def update(on_diagonal):
        if on_diagonal:
            mask = (lax.broadcasted_iota(jnp.int32, (ta, ta), 1)
                    <= lax.broadcasted_iota(jnp.int32, (ta, ta), 0))
        for h in range(N_HEADS):
            s = _dot(q_ref[0, h], kta_ref[0, 2 * h * HEAD_DIM:2 * (h + 1) * HEAD_DIM, :])
            if on_diagonal:
                s = jnp.where(mask, s, NEG_INF)
            m_old = m_ref[h]
            m_new = jnp.maximum(m_old, jnp.max(s, axis=-1, keepdims=True))
            p = jnp.exp(s - jnp.concatenate([m_new] * (ta // LANES), axis=1))
            alpha = jnp.exp(m_old - m_new)
            l_ref[h] = alpha * l_ref[h] + jnp.sum(p, axis=-1, keepdims=True)
            pv = _dot_nt(p.astype(BF16), vt_ref[0, h * HEAD_DIM:(h + 1) * HEAD_DIM, :])
            acc_ref[h] = alpha[:, :HEAD_DIM] * acc_ref[h] + pv
            m_ref[h] = m_new

    @pl.when(ki < qi)
    def _():
        update(False)

    @pl.when(ki == qi)
    def _():
        update(True)
        for h in range(0, N_HEADS, 2):
            pair = [acc_ref[h + j] / l_ref[h + j][:, :HEAD_DIM] for j in range(2)]
            o_ref[0, :, h * HEAD_DIM:(h + 2) * HEAD_DIM] = jnp.concatenate(pair, axis=1).astype(o_ref.dtype)


def _attn_prompt(qa, kta, vtb):
    b, _, t, _ = qa.shape
    ta = min(ATTN_TILE, t)
    nq = t // ta
    pairs = [(i, j) for i in range(nq) for j in range(i + 1)]
    qi_tab = jnp.asarray([p[0] for p in pairs], jnp.int32)
    ki_tab = jnp.asarray([p[1] for p in pairs], jnp.int32)
    a = ATTN_WIDTH
    grid_spec = pltpu.PrefetchScalarGridSpec(
        num_scalar_prefetch=2,
        grid=(b, len(pairs)),
        in_specs=[
            pl.BlockSpec((1, N_HEADS, ta, 2 * HEAD_DIM), lambda i, s, qt, kt: (i, 0, qt[s], 0)),
            pl.BlockSpec((1, 2 * a, ta), lambda i, s, qt, kt: (i, 0, kt[s])),
            pl.BlockSpec((1, a, ta), lambda i, s, qt, kt: (i, 0, kt[s])),
        ],
        out_specs=pl.BlockSpec((1, ta, a), lambda i, s, qt, kt: (i, qt[s], 0)),
        scratch_shapes=[pltpu.VMEM((N_HEADS, ta, LANES), F32), pltpu.VMEM((N_HEADS, ta, LANES), F32),
                        pltpu.VMEM((N_HEADS, ta, HEAD_DIM), F32)],
    )
    return pl.pallas_call(
        _attn_prompt_kernel,
        grid_spec=grid_spec,
        out_shape=jax.ShapeDtypeStruct((b, t, a), BF16),
        compiler_params=_cparams("arbitrary", "arbitrary"),
        name="attn_prompt",
    )(qi_tab, ki_tab, qa, kta, vtb)


def _attn_decode_kernel(n_pg, pt_ref, q_ref, kn_ref, vn_ref, lfn_ref, w_ref, *rest):
    k_refs = rest[0:n_pg]
    v_refs = rest[n_pg:2 * n_pg]
    lt_refs = rest[2 * n_pg:3 * n_pg]
    o_ref = rest[3 * n_pg]
    m_ref, l_ref, c_ref, acc_ref = rest[3 * n_pg + 1:]
    c_step = pl.program_id(1)
    width = acc_ref.shape[1]
    own = (lax.broadcasted_iota(jnp.int32, (N_HEADS, width), 1) // HEAD_DIM
           == lax.broadcasted_iota(jnp.int32, (N_HEADS, width), 0))
    q_rows = jnp.where(own, q_ref[0] * (HEAD_DIM ** -0.5), 0.0)

    @pl.when(c_step == 0)
    def _():
        lane = lax.broadcasted_iota(jnp.int32, (N_HEADS, LANES), 1)
        s0 = jnp.sum(q_rows * kn_ref[0], axis=1, keepdims=True)
        m_ref[...] = jnp.broadcast_to(s0, m_ref.shape)
        l_ref[...] = jnp.where(lane == 0, 1.0, 0.0)
        acc_ref[...] = jnp.broadcast_to(vn_ref[0], acc_ref.shape)
        c_ref[...] = jnp.broadcast_to(lfn_ref[0], c_ref.shape)

    qb = q_rows.astype(BF16)
    carry = c_ref[...]
    s = []
    for i in range(n_pg):
        r = _exact_dot(lt_refs[i][0], w_ref[...])
        s.append(_dot(qb, k_refs[i][...].astype(BF16)) + (r[:, :LANES] + carry))
        carry = carry + r[:, LANES:]
    c_ref[...] = carry

    m_tile = s[0]
    for i in range(1, n_pg):
        m_tile = jnp.maximum(m_tile, s[i])
    m_old = m_ref[...]
    m_new = jnp.maximum(m_old, jnp.max(m_tile, axis=1, keepdims=True))
    alpha = jnp.exp(m_old - m_new)
    l_new = alpha * l_ref[...]
    pv = jnp.zeros(acc_ref.shape, F32)
    for i in range(n_pg):
        p = jnp.exp(s[i] - m_new)
        l_new = l_new + p
        pv = pv + _dot_nt(p.astype(BF16), v_refs[i][...].astype(BF16))
    m_ref[...] = m_new
    l_ref[...] = l_new
    acc_ref[...] = jnp.concatenate([alpha] * (width // LANES), axis=1) * acc_ref[...] + pv

    @pl.when(c_step == pl.num_programs(1) - 1)
    def _():
        den = jnp.sum(l_ref[...], axis=1, keepdims=True)
        o_ref[0] = jnp.sum(jnp.where(own, acc_ref[...] / den, 0.0), axis=0, keepdims=True)


def _attn_decode(layer, page_table, q, kn, vn, lfn_col, kt_pool, vt_pool, lt_pool, n_pool):
    db, n_pages = page_table.shape
    n_pg = min(DECODE_PAGES, n_pages)
    n_chunks = n_pages // n_pg
    page = kt_pool.shape[-1]
    base = layer * n_pool
    rowi = lax.broadcasted_iota(jnp.int32, (page, 2 * page), 0)
    coli = lax.broadcasted_iota(jnp.int32, (page, 2 * page), 1)
    w = ((rowi > coli) | (coli >= page)).astype(BF16)

    def page_of(slot):
        def index(b, c, pt):
            return base + pt[b * n_pages + (n_pages - 1 - (c * n_pg + slot))]
        return index

    per_seq = lambda r, c: pl.BlockSpec((1, r, c), lambda b, ch, pt: (b, 0, 0))

    def kv_specs():
        return [pl.BlockSpec((ATTN_WIDTH, page), (lambda b, ch, pt, f=page_of(i): (f(b, ch, pt), 0)))
                for i in range(n_pg)]

    lt_specs = [pl.BlockSpec((1, N_HEADS, page), (lambda b, ch, pt, f=page_of(i): (f(b, ch, pt), 0, 0)))
                for i in range(n_pg)]
    grid_spec = pltpu.PrefetchScalarGridSpec(
        num_scalar_prefetch=1,
        grid=(db, n_chunks),
        in_specs=[per_seq(1, ATTN_WIDTH), per_seq(1, ATTN_WIDTH), per_seq(1, ATTN_WIDTH), per_seq(N_HEADS, 1),
                  pl.BlockSpec(w.shape, lambda b, ch, pt: (0, 0))] + kv_specs() + kv_specs() + lt_specs,
        out_specs=per_seq(1, ATTN_WIDTH),
        scratch_shapes=[pltpu.VMEM((N_HEADS, LANES), F32), pltpu.VMEM((N_HEADS, LANES), F32),
                        pltpu.VMEM((N_HEADS, LANES), F32), pltpu.VMEM((N_HEADS, ATTN_WIDTH), F32)],
    )
    return pl.pallas_call(
        functools.partial(_attn_decode_kernel, n_pg),
        grid_spec=grid_spec,
        out_shape=jax.ShapeDtypeStruct((db, 1, ATTN_WIDTH), F32),
        compiler_params=_cparams("arbitrary", "arbitrary"),
        name="attn_decode",
    )(page_table.reshape(-1), q, kn, vn, lfn_col, w,
      *([kt_pool] * n_pg), *([vt_pool] * n_pg), *([lt_pool] * n_pg))


def _lru_gates(xc, wrg_ref, brg_ref, wig_ref, big_ref, lam_ref, store):
    xcb = xc.astype(BF16)
    blk = xc.shape[-1] // N_LRU_BLOCKS
    for n in range(N_LRU_BLOCKS):
        cs = slice(n * blk, (n + 1) * blk)
        r = jax.nn.sigmoid(_dot(xcb[:, cs], wrg_ref[n]) + brg_ref[:, cs])
        i = jax.nn.sigmoid(_dot(xcb[:, cs], wig_ref[n]) + big_ref[:, cs])
        log_a = -LRU_C * r * _softplus(-lam_ref[:, cs])
        a = jnp.exp(log_a)
        th = jnp.tanh(log_a)
        mult = jnp.sqrt(-2.0 * th / (1.0 - th))
        store(cs, a, mult * (i * xc[:, cs]))


def _lru_prompt_kernel(xr_ref, gr_ref, cp_ref, h0_ref, cw_ref, cb_ref, wrg_ref, brg_ref, wig_ref,
                       big_ref, lam_ref, y_ref, tail_ref, hl_ref, buf_ref, a_ref, b_ref, h_ref, hc_ref):
    t = pl.program_id(1)
    tt = xr_ref.shape[1]
    halo = SUBLANES

    @pl.when(t == 0)
    def _():
        buf_ref[0:halo, :] = cp_ref[0]
        hc_ref[...] = h0_ref[0]

    x = xr_ref[0]
    buf_ref[halo:, :] = x
    taps = [buf_ref[pl.ds(halo - (CONV_WIDTH - 1) + j, tt), :] * cw_ref[j:j + 1, :]
            for j in range(CONV_WIDTH - 1)]
    taps.append(x * cw_ref[CONV_WIDTH - 1:CONV_WIDTH, :])
    acc = taps[0]
    for term in taps[1:]:
        acc = acc + term
    xc = cb_ref[...] + acc

    def store(cs, a, b):
        a_ref[:, cs] = a
        b_ref[:, cs] = b

    _lru_gates(xc, wrg_ref, brg_ref, wig_ref, big_ref, lam_ref, store)

    grouped = (tt // SUBLANES, SUBLANES, a_ref.shape[1])
    a = a_ref[...].reshape(grouped)
    b = b_ref[...].reshape(grouped)
    in_group = lax.broadcasted_iota(jnp.int32, grouped, 1)
    shift = 1
    while shift < SUBLANES:
        a_prev = pltpu.roll(a, shift, axis=1)
        b_prev = pltpu.roll(b, shift, axis=1)
        take = in_group >= shift
        b = jnp.where(take, a * b_prev + b, b)
        a = jnp.where(take, a * a_prev, a)
        shift *= 2
    a_ref[...] = a.reshape(a_ref.shape)
    b_ref[...] = b.reshape(b_ref.shape)
    h_in = jnp.broadcast_to(hc_ref[...], (SUBLANES, a_ref.shape[1]))
    for g in range(tt // SUBLANES):
        rows = slice(g * SUBLANES, (g + 1) * SUBLANES)
        h_grp = a_ref[rows, :] * h_in + b_ref[rows, :]
        h_ref[rows, :] = h_grp
        h_in = jnp.broadcast_to(h_grp[SUBLANES - 1:, :], h_grp.shape)
    h_last = h_in[:1, :]
    hc_ref[...] = h_last
    hl_ref[0] = h_last
    y_ref[0] = (h_ref[...] * jax.nn.gelu(gr_ref[0])).astype(y_ref.dtype)
    tail = buf_ref[tt:tt + halo, :]
    tail_ref[0] = tail
    buf_ref[0:halo, :] = tail


def _lru_prompt(xr, gr, conv_prev8, h0, cw, cb, wrg, brg, wig, big, lam):
    b, t, d = xr.shape
    tt = min(LRU_TILE, t)
    seq = pl.BlockSpec((1, tt, d), lambda i, j: (i, j, 0))
    per_b = lambda r: pl.BlockSpec((1, r, d), lambda i, j: (i, 0, 0))
    return pl.pallas_call(
        _lru_prompt_kernel,
        grid=(b, t // tt),
        in_specs=[seq, seq, per_b(SUBLANES), per_b(1)] + [_full(v.shape) for v in (cw, cb, wrg, brg, wig, big, lam)],
        out_specs=[seq, per_b(SUBLANES), per_b(1)],
        out_shape=[jax.ShapeDtypeStruct((b, t, d), BF16), jax.ShapeDtypeStruct((b, SUBLANES, d), F32),
                   jax.ShapeDtypeStruct((b, 1, d), F32)],
        scratch_shapes=[pltpu.VMEM((tt + SUBLANES, d), F32), pltpu.VMEM((tt, d), F32),
                        pltpu.VMEM((tt, d), F32), pltpu.VMEM((tt, d), F32), pltpu.VMEM((1, d), F32)],
        compiler_params=_cparams("arbitrary", "arbitrary"),
        name="lru_prompt",
    )(xr, gr, conv_prev8, h0, cw, cb, wrg, brg, wig, big, lam)


def _lru_sample_kernel(xr_ref, gr_ref, c0_ref, c1_ref, c2_ref, h0_ref, cw_ref, cb_ref, wrg_ref, brg_ref,
                       wig_ref, big_ref, lam_ref, y_ref, hn_ref):
    x = xr_ref[...]
    acc = c0_ref[...] * cw_ref[0:1, :]
    acc = acc + c1_ref[...] * cw_ref[1:2, :]
    acc = acc + c2_ref[...] * cw_ref[2:3, :]
    acc = acc + x * cw_ref[3:4, :]
    xc = cb_ref[...] + acc

    def store(cs, a, b):
        hn_ref[:, cs] = a * h0_ref[:, cs] + b

    _lru_gates(xc, wrg_ref, brg_ref, wig_ref, big_ref, lam_ref, store)
    y_ref[...] = (hn_ref[...] * jax.nn.gelu(gr_ref[...])).astype(y_ref.dtype)


def _lru_sample(xr, gr, c0, c1, c2, h0, cw, cb, wrg, brg, wig, big, lam):
    m, d = xr.shape
    args = (xr, gr, c0, c1, c2, h0, cw, cb, wrg, brg, wig, big, lam)
    return pl.pallas_call(
        _lru_sample_kernel,
        grid=(1,),
        in_specs=[_full(v.shape) for v in args],
        out_specs=[_full((m, d)), _full((m, d))],
        out_shape=[jax.ShapeDtypeStruct((m, d), BF16), jax.ShapeDtypeStruct((m, d), F32)],
        compiler_params=_cparams("arbitrary"),
        name="lru_sample",
    )(*args)


def _merge_kernel(attn_ref, y_ref, ga_ref, gb_ref, x_ref, wa_ref, wb_ref, wo_ref, g_ref, x1_ref, xn_ref):
    merged = (jax.nn.sigmoid(ga_ref[...]) * _dot(attn_ref[...].astype(BF16), wa_ref[...])
              + jax.nn.sigmoid(gb_ref[...]) * _dot(y_ref[...], wb_ref[...]))
    x1 = x_ref[...] + _dot(merged.astype(BF16), wo_ref[...])
    x1_ref[...] = x1
    xn_ref[...] = _rmsnorm(x1, g_ref[...]).astype(xn_ref.dtype)


def _merge(attn, y, ga, gb, x, wa, wb, wo, g):
    m, d = x.shape
    tm = min(ROW_TILE, m)
    row = lambda w: pl.BlockSpec((tm, w), lambda i: (i, 0))
    return pl.pallas_call(
        _merge_kernel,
        grid=(m // tm,),
        in_specs=[row(attn.shape[1]), row(d), row(d), row(d), row(d)] + [_full(v.shape) for v in (wa, wb, wo, g)],
        out_specs=[row(d), row(d)],
        out_shape=[jax.ShapeDtypeStruct((m, d), F32), jax.ShapeDtypeStruct((m, d), BF16)],
        compiler_params=_cparams("arbitrary"),
        name="merge",
    )(attn, y, ga, gb, x, wa, wb, wo, g)


def _top2(logits):
    lane = lax.broadcasted_iota(jnp.int32, logits.shape, 1)
    logits = jnp.where(lane < N_EXPERTS, logits, -jnp.inf)
    m1 = jnp.max(logits, axis=-1, keepdims=True)
    i1 = jnp.min(jnp.where(logits == m1, lane, LANES), axis=-1, keepdims=True)
    rest = jnp.where(lane == i1, -jnp.inf, logits)
    m2 = jnp.max(rest, axis=-1, keepdims=True)
    i2 = jnp.min(jnp.where(rest == m2, lane, LANES), axis=-1, keepdims=True)
    e2 = jnp.exp(m2 - m1)
    den = 1.0 + e2
    return lane, i1, i2, 1.0 / den, e2 / den


def _to_row_tiles(dst_ref, x):
    rows = x.shape[0]
    for s in range(x.shape[1] // LANES):
        dst_ref[pl.ds(s, rows, stride=SUBLANES), :] = x[:, s * LANES:(s + 1) * LANES]


def _from_row_tiles(src_ref, rows, s):
    return src_ref[pl.ds(s, rows, stride=SUBLANES), :]


def _merge_route_kernel(attn_ref, y_ref, ga_ref, gb_ref, x_ref, wa_ref, wb_ref, wo_ref, g_ref, wr_ref, tri_ref,
                        x1_ref, xrt_ref, meta_ref, cnt_ref, carry_ref):
    i = pl.program_id(0)

    @pl.when(i == 0)
    def _():
        carry_ref[...] = jnp.zeros_like(carry_ref)

    merged = (jax.nn.sigmoid(ga_ref[...]) * _dot(attn_ref[...].astype(BF16), wa_ref[...])
              + jax.nn.sigmoid(gb_ref[...]) * _dot(y_ref[...], wb_ref[...]))
    x1 = x_ref[...] + _dot(merged.astype(BF16), wo_ref[...])
    x1_ref[...] = x1
    xn = _rmsnorm(x1, g_ref[...])
    _to_row_tiles(xrt_ref, xn)
    lane, i1, i2, w1, w2 = _top2(_dot(xn.astype(BF16), wr_ref[...]))
    oh1 = lane == i1
    oh2 = lane == i2
    tri = tri_ref[...]
    before1 = _dot(tri, oh1.astype(BF16))
    before2 = _dot(tri, oh2.astype(BF16))
    cnt1 = jnp.sum(oh1.astype(F32), axis=0, keepdims=True)
    cnt2 = jnp.sum(oh2.astype(F32), axis=0, keepdims=True)
    carry = carry_ref[0:1, :]
    rank1 = jnp.sum(jnp.where(oh1, carry + before1, 0.0), axis=-1, keepdims=True)
    rank2 = jnp.sum(jnp.where(oh2, carry + cnt1 + before2, 0.0), axis=-1, keepdims=True)
    total = carry + cnt1 + cnt2
    carry_ref[...] = jnp.broadcast_to(total, carry_ref.shape)
    cnt_ref[...] = jnp.broadcast_to(total, cnt_ref.shape)
    cols = (i1.astype(F32), i2.astype(F32), rank1, rank2, w1, w2)
    meta = jnp.zeros(meta_ref.shape, F32)
    for k, v in enumerate(cols):
        meta = jnp.where(lane == k, v, meta)
    meta_ref[...] = meta


def _merge_route(attn, y, ga, gb, x, wa, wb, wo, g, wr_pad):
    m, d = x.shape
    tm = min(ROW_TILE, m)
    tri = (lax.broadcasted_iota(jnp.int32, (tm, tm), 0)
           > lax.broadcasted_iota(jnp.int32, (tm, tm), 1)).astype(BF16)
    row = lambda w: pl.BlockSpec((tm, w), lambda i: (i, 0))
    return pl.pallas_call(
        _merge_route_kernel,
        grid=(m // tm,),
        in_specs=[row(attn.shape[1]), row(d), row(d), row(d), row(d)]
        + [_full(v.shape) for v in (wa, wb, wo, g, wr_pad, tri)],
        out_specs=[row(d), pl.BlockSpec((tm * SUBLANES, LANES), lambda i: (i, 0)), row(LANES),
                   _full((SUBLANES, LANES))],
        out_shape=[jax.ShapeDtypeStruct((m, d), F32), jax.ShapeDtypeStruct((m * SUBLANES, LANES), F32),
                   jax.ShapeDtypeStruct((m, LANES), F32), jax.ShapeDtypeStruct((SUBLANES, LANES), F32)],
        scratch_shapes=[pltpu.VMEM((SUBLANES, LANES), F32)],
        compiler_params=_cparams("arbitrary"),
        name="merge_route",
    )(attn, y, ga, gb, x, wa, wb, wo, g, wr_pad, tri)


def _row_copy(src_ref, src_row, dst_ref, dst_row, sem):
    return pltpu.make_async_copy(
        src_ref.at[pl.ds(pl.multiple_of(src_row * SUBLANES, SUBLANES), SUBLANES)],
        dst_ref.at[pl.ds(pl.multiple_of(dst_row * SUBLANES, SUBLANES), SUBLANES)], sem)


def _dispatch_kernel(d1_ref, d2_ref, x_ref, init_ref, xs_ref, sem):
    del init_ref
    tm = x_ref.shape[0] // SUBLANES
    base = pl.program_id(0) * tm

    def issue(r, carry):
        _row_copy(x_ref, r, xs_ref, d1_ref[base + r], sem).start(priority=0)
        _row_copy(x_ref, r, xs_ref, d2_ref[base + r], sem).start(priority=1)
        return carry

    lax.fori_loop(0, tm, issue, 0)

    def drain(r, carry):
        _row_copy(x_ref, r, xs_ref, d1_ref[base + r], sem).wait()
        _row_copy(x_ref, r, xs_ref, d2_ref[base + r], sem).wait()
        return carry

    lax.fori_loop(0, tm, drain, 0)


def _dispatch(dest1, dest2, x_rt, n_rows):
    m = dest1.shape[0]
    tm = min(ROW_TILE, m)
    grid_spec = pltpu.PrefetchScalarGridSpec(
        num_scalar_prefetch=2,
        grid=(m // tm,),
        in_specs=[pl.BlockSpec((tm * SUBLANES, LANES), lambda i, d1, d2: (i, 0)),
                  pl.BlockSpec(memory_space=pl.ANY)],
        out_specs=pl.BlockSpec(memory_space=pl.ANY),
        scratch_shapes=[pltpu.SemaphoreType.DMA],
    )
    return pl.pallas_call(
        _dispatch_kernel,
        grid_spec=grid_spec,
        out_shape=jax.ShapeDtypeStruct((n_rows * SUBLANES, LANES), F32),
        input_output_aliases={3: 0},
        compiler_params=_cparams("arbitrary"),
        name="moe_dispatch",
    )(dest1, dest2, x_rt, jnp.zeros((n_rows * SUBLANES, LANES), F32))


def _moe_ffn_kernel(te_ref, nu_ref, xs_ref, wg_ref, wu_ref, wd_ref, ys_ref, xb_ref, acc_ref):
    i = pl.program_id(0)
    f = pl.program_id(1)
    rows, d = xb_ref.shape

    @pl.when(i < nu_ref[0])
    def _():
        @pl.when(f == 0)
        def _():
            for s in range(d // LANES):
                xb_ref[:, s * LANES:(s + 1) * LANES] = _from_row_tiles(xs_ref, rows, s).astype(BF16)
            acc_ref[...] = jnp.zeros_like(acc_ref)

        xb = xb_ref[...]
        hid = jax.nn.silu(_dot(xb, wg_ref[0])) * _dot(xb, wu_ref[0])
        acc_ref[...] += _dot(hid.astype(BF16), wd_ref[0])

        @pl.when(f == pl.num_programs(1) - 1)
        def _():
            _to_row_tiles(ys_ref, acc_ref[...])

    @pl.when((i >= nu_ref[0]) & (f == pl.num_programs(1) - 1))
    def _():
        ys_ref[...] = jnp.zeros_like(ys_ref)


def _moe_ffn(tile_expert, n_used, xs, wg, wu, wd):
    n_e, d, ff = wg.shape
    tm = MOE_ROW_TILE
    tf = MOE_COL_TILE
    nf = ff // tf
    n_tiles = tile_expert.shape[0]

    def tile(i, f, te, nu):
        return (i, 0)

    def col(i, f, te, nu):
        return jnp.where(i < nu[0], f, nf - 1)

    grid_spec = pltpu.PrefetchScalarGridSpec(
        num_scalar_prefetch=2,
        grid=(n_tiles, nf),
        in_specs=[pl.BlockSpec((tm * SUBLANES, LANES), tile),
                  pl.BlockSpec((1, d, tf), lambda i, f, te, nu: (te[i], 0, col(i, f, te, nu))),
                  pl.BlockSpec((1, d, tf), lambda i, f, te, nu: (te[i], 0, col(i, f, te, nu))),
                  pl.BlockSpec((1, tf, d), lambda i, f, te, nu: (te[i], col(i, f, te, nu), 0))],
        out_specs=pl.BlockSpec((tm * SUBLANES, LANES), tile),
        scratch_shapes=[pltpu.VMEM((tm, d), BF16), pltpu.VMEM((tm, d), F32)],
    )
    return pl.pallas_call(
        _moe_ffn_kernel,
        grid_spec=grid_spec,
        out_shape=jax.ShapeDtypeStruct(xs.shape, F32),
        compiler_params=_cparams("arbitrary", "arbitrary"),
        name="moe_ffn",
    )(tile_expert, n_used, xs, wg, wu, wd)


def _combine_kernel(d1_ref, d2_ref, x1_ref, meta_ref, ys_ref, o_ref, g1_ref, g2_ref, sem):
    tm, d = x1_ref.shape
    base = pl.program_id(0) * tm

    def issue(r, carry):
        _row_copy(ys_ref, d1_ref[base + r], g1_ref, r, sem).start(priority=0)
        _row_copy(ys_ref, d2_ref[base + r], g2_ref, r, sem).start(priority=1)
        return carry

    lax.fori_loop(0, tm, issue, 0)

    def drain(r, carry):
        _row_copy(ys_ref, d1_ref[base + r], g1_ref, r, sem).wait()
        _row_copy(ys_ref, d2_ref[base + r], g2_ref, r, sem).wait()
        return carry

    lax.fori_loop(0, tm, drain, 0)
    w1 = meta_ref[:, 4:5]
    w2 = meta_ref[:, 5:6]
    for s in range(d // LANES):
        cs = slice(s * LANES, (s + 1) * LANES)
        o_ref[:, cs] = (x1_ref[:, cs] + w1 * _from_row_tiles(g1_ref, tm, s)
                        + w2 * _from_row_tiles(g2_ref, tm, s))


def _combine(dest1, dest2, x1, meta, ys):
    m, d = x1.shape
    tm = min(ROW_TILE, m)
    grid_spec = pltpu.PrefetchScalarGridSpec(
        num_scalar_prefetch=2,
        grid=(m // tm,),
        in_specs=[pl.BlockSpec((tm, d), lambda i, d1, d2: (i, 0)),
                  pl.BlockSpec((tm, LANES), lambda i, d1, d2: (i, 0)),
                  pl.BlockSpec(memory_space=pl.ANY)],
        out_specs=pl.BlockSpec((tm, d), lambda i, d1, d2: (i, 0)),
        scratch_shapes=[pltpu.VMEM((tm * SUBLANES, LANES), F32), pltpu.VMEM((tm * SUBLANES, LANES), F32),
                        pltpu.SemaphoreType.DMA],
    )
    return pl.pallas_call(
        _combine_kernel,
        grid_spec=grid_spec,
        out_shape=jax.ShapeDtypeStruct((m, d), F32),
        compiler_params=_cparams("arbitrary"),
        name="moe_combine",
    )(dest1, dest2, x1, meta, ys)


def _moe_sorted(x1, x_rt, meta, counts, wg, wu, wd):
    m = x1.shape[0]
    tm = MOE_ROW_TILE
    n_tiles = (2 * m) // tm + N_EXPERTS
    cnt = counts[0, :N_EXPERTS].astype(jnp.int32)
    padded = ((cnt + tm - 1) // tm) * tm
    ends = jnp.cumsum(padded)
    starts = ends - padded
    experts = jnp.arange(N_EXPERTS, dtype=jnp.int32)[None, :]

    def dest(expert_col, rank_col):
        e = meta[:, expert_col].astype(jnp.int32)[:, None]
        return jnp.sum(jnp.where(e == experts, starts[None, :], 0), axis=1) + meta[:, rank_col].astype(jnp.int32)

    dest1 = dest(0, 2)
    dest2 = dest(1, 3)
    n_used = (ends[-1] // tm).reshape(1)
    tile_start = jnp.minimum(jnp.arange(n_tiles, dtype=jnp.int32), n_used - 1) * tm
    tile_expert = jnp.sum(ends[None, :] <= tile_start[:, None], axis=1).astype(jnp.int32)
    xs = _dispatch(dest1, dest2, x_rt, n_tiles * tm)
    ys = _moe_ffn(tile_expert, n_used.astype(jnp.int32), xs, wg, wu, wd)
    return _combine(dest1, dest2, x1, meta, ys)


def _ffn_kernel(xn_ref, x1_ref, wg_ref, wu_ref, wd_ref, o_ref, acc_ref):
    f = pl.program_id(1)

    @pl.when(f == 0)
    def _():
        acc_ref[...] = jnp.zeros_like(acc_ref)

    xn = xn_ref[...]
    hid = jax.nn.silu(_dot(xn, wg_ref[...])) * _dot(xn, wu_ref[...])
    acc_ref[...] += _dot(hid.astype(BF16), wd_ref[...])

    @pl.when(f == pl.num_programs(1) - 1)
    def _():
        o_ref[...] = x1_ref[...] + acc_ref[...]


def _ffn(xn, x1, wg, wu, wd):
    m, d = x1.shape
    ff = wg.shape[1]
    tm = min(FFN_ROW_TILE, m)
    tf = FFN_COL_TILE
    row = pl.BlockSpec((tm, d), lambda i, f: (i, 0))
    return pl.pallas_call(
        _ffn_kernel,
        grid=(m // tm, ff // tf),
        in_specs=[row, row, pl.BlockSpec((d, tf), lambda i, f: (0, f)), pl.BlockSpec((d, tf), lambda i, f: (0, f)),
                  pl.BlockSpec((tf, d), lambda i, f: (f, 0))],
        out_specs=row,
        out_shape=jax.ShapeDtypeStruct((m, d), F32),
        scratch_shapes=[pltpu.VMEM((tm, d), F32)],
        compiler_params=_cparams("arbitrary", "arbitrary"),
        name="ffn_dense",
    )(xn, x1, wg, wu, wd)


def _router_kernel(xn_ref, wr_ref, gates_ref):
    lane, i1, i2, w1, w2 = _top2(_dot(xn_ref[...], wr_ref[...]))
    gates_ref[...] = jnp.where(lane == i1, w1, 0.0) + jnp.where(lane == i2, w2, 0.0)


def _router(xn, wr_pad):
    m, d = xn.shape
    tm = min(ROW_TILE, m)
    return pl.pallas_call(
        _router_kernel,
        grid=(m // tm,),
        in_specs=[pl.BlockSpec((tm, d), lambda i: (i, 0)), _full(wr_pad.shape)],
        out_specs=pl.BlockSpec((tm, LANES), lambda i: (i, 0)),
        out_shape=jax.ShapeDtypeStruct((m, LANES), F32),
        compiler_params=_cparams("arbitrary"),
        name="router",
    )(xn, wr_pad)


def _moe_kernel(xn_ref, x1_ref, gates_ref, wg_ref, wu_ref, wd_ref, o_ref, acc_ref):
    e = pl.program_id(1)
    f = pl.program_id(2)

    @pl.when((e == 0) & (f == 0))
    def _():
        acc_ref[...] = jnp.zeros_like(acc_ref)

    xn = xn_ref[...]
    hid = jax.nn.silu(_dot(xn, wg_ref[0])) * _dot(xn, wu_ref[0])
    gates = gates_ref[...]
    lane = lax.broadcasted_iota(jnp.int32, gates.shape, 1)
    gate = jnp.sum(jnp.where(lane == e, gates, 0.0), axis=-1, keepdims=True)
    acc_ref[...] += gate * _dot(hid.astype(BF16), wd_ref[0])

    @pl.when((e == pl.num_programs(1) - 1) & (f == pl.num_programs(2) - 1))
    def _():
        o_ref[...] = x1_ref[...] + acc_ref[...]


def _moe(xn, x1, gates, wg, wu, wd):
    m, d = x1.shape
    n_e, _, ff = wg.shape
    tm = min(FFN_ROW_TILE, m)
    tf = FFN_COL_TILE
    row = lambda w: pl.BlockSpec((tm, w), lambda i, e, f: (i, 0))
    return pl.pallas_call(
        _moe_kernel,
        grid=(m // tm, n_e, ff // tf),
        in_specs=[row(d), row(d), row(LANES),
                  pl.BlockSpec((1, d, tf), lambda i, e, f: (e, 0, f)),
                  pl.BlockSpec((1, d, tf), lambda i, e, f: (e, 0, f)),
                  pl.BlockSpec((1, tf, d), lambda i, e, f: (e, f, 0))],
        out_specs=row(d),
        out_shape=jax.ShapeDtypeStruct((m, d), F32),
        scratch_shapes=[pltpu.VMEM((tm, d), F32)],
        compiler_params=_cparams("arbitrary", "arbitrary", "arbitrary"),
        name="moe_dense",
    )(xn, x1, gates, wg, wu, wd)


def _ple_kernel(final, x_ref, p_ref, g_ref, wgate_ref, wproj_ref, gf_ref, o_ref):
    x = x_ref[...]
    gate = jax.nn.sigmoid(_dot(_rmsnorm(x, g_ref[...]).astype(BF16), wgate_ref[...]))
    x3 = x + gate * _dot(p_ref[...].astype(BF16), wproj_ref[...])
    o_ref[...] = _rmsnorm(x3, gf_ref[...]) if final else x3


def _ple(x, p, g, wgate, wproj, g_final, final):
    m, d = x.shape
    tm = min(ROW_TILE, m)
    row = lambda w: pl.BlockSpec((tm, w), lambda i: (i, 0))
    return pl.pallas_call(
        functools.partial(_ple_kernel, final),
        grid=(m // tm,),
        in_specs=[row(d), row(p.shape[1])] + [_full(v.shape) for v in (g, wgate, wproj, g_final)],
        out_specs=row(d),
        out_shape=jax.ShapeDtypeStruct((m, d), F32),
        compiler_params=_cparams("arbitrary"),
        name="ple",
    )(x, p, g, wgate, wproj, g_final)


def _row(v):
    return v.reshape(1, -1)


def kernel(x_prompt, x_sample, p_prompt, p_sample, cache_k, cache_v, cache_logf, state_conv, state_h, page_table, g_mix, w_in, b_f, conv_w, conv_b, w_rg, b_rg, w_ig, b_ig, lru_lambda, w_a_up, w_b_up, w_out, g_ffn, dense_wg, dense_wu, dense_wd, moe_router, moe_wg, moe_wu, moe_wd, g_ple, w_ple_gate, w_ple_proj, g_final):
    depth = g_mix.shape[0]
    b, t, d = x_prompt.shape
    db = x_sample.shape[0]
    n_pool, page = cache_k.shape[1], cache_k.shape[2]
    a = ATTN_WIDTH
    kt_pool = jnp.transpose(cache_k, (0, 1, 3, 4, 2)).reshape(depth * n_pool * ATTN_WIDTH, page)
    vt_pool = jnp.transpose(cache_v, (0, 1, 3, 4, 2)).reshape(depth * n_pool * ATTN_WIDTH, page)
    lt_pool = jnp.transpose(cache_logf, (0, 1, 3, 2)).reshape(depth * n_pool, N_HEADS, page)

    xp = x_prompt
    xs = x_sample.reshape(db, d)
    outs = {name: [] for name in ("kp", "vp", "lp", "cp", "hp", "ks", "vs", "ls", "cs", "hs")}
    for l in range(depth):
        wl = w_in[l]
        wq = wl[:, 0:a].astype(BF16)
        wk = wl[:, a:2 * a].astype(BF16)
        wv = wl[:, 2 * a:3 * a].astype(BF16)
        wf = jnp.pad(wl[:, 3 * a:3 * a + N_HEADS], ((0, 0), (0, LANES - N_HEADS))).astype(BF16)
        w4 = wl[:, 3 * a + N_HEADS:].astype(BF16)
        bf_row = jnp.pad(b_f[l], (0, LANES - N_HEADS)).reshape(1, LANES)
        g1 = _row(g_mix[l])
        lru_w = (conv_w[l], _row(conv_b[l]), w_rg[l].astype(BF16), _row(b_rg[l]), w_ig[l].astype(BF16),
                 _row(b_ig[l]), _row(lru_lambda[l]))
        wa = w_a_up[l].astype(BF16)
        wb = w_b_up[l].astype(BF16)
        wo = w_out[l].astype(BF16)
        g2 = _row(g_ffn[l])
        g3 = _row(g_ple[l])
        wgate = w_ple_gate[l].astype(BF16)
        wproj = w_ple_proj[l].astype(BF16)
        gf = _row(g_final)
        final = l == depth - 1
        mi = l // 2
        if l % 2 == 0:
            ffn_w = (dense_wg[mi].astype(BF16), dense_wu[mi].astype(BF16), dense_wd[mi].astype(BF16))
        else:
            wr_pad = jnp.pad(moe_router[mi], ((0, 0), (0, LANES - N_EXPERTS))).astype(BF16)
            ffn_w = (moe_wg[mi].astype(BF16), moe_wu[mi].astype(BF16), moe_wd[mi].astype(BF16))

        qa, kta, kt, vt, vtb, lft, xr, gr, ga, gb = _proj_prompt(xp, g1, wq, wk.T, wv.T, wf, bf_row, w4)
        attn = _attn_prompt(qa, kta, vtb)
        y, tail, hl = _lru_prompt(xr, gr, jnp.zeros((b, SUBLANES, d), F32), jnp.zeros((b, 1, d), F32), *lru_w)
        m = b * t
        merge_in = (attn.reshape(m, a), y.reshape(m, d), ga.reshape(m, d), gb.reshape(m, d),
                    xp.reshape(m, d), wa, wb, wo, g2)
        if l % 2 == 0:
            x1, xn = _merge(*merge_in)
            x2 = _ffn(xn, x1, *ffn_w)
        else:
            x1, x_rt, meta, counts = _merge_route(*merge_in, wr_pad)
            x2 = _moe_sorted(x1, x_rt, meta, counts, *ffn_w)
        xp = _ple(x2, p_prompt[l].reshape(m, -1), g3, wgate, wproj, gf, final).reshape(b, t, d)
        outs["kp"].append(jnp.transpose(kt.reshape(b, N_HEADS, HEAD_DIM, t), (0, 3, 1, 2)))
        outs["vp"].append(jnp.transpose(vt.reshape(b, N_HEADS, HEAD_DIM, t), (0, 3, 1, 2)))
        outs["lp"].append(jnp.transpose(lft, (0, 2, 1)))
        outs["cp"].append(tail[:, SUBLANES - (CONV_WIDTH - 1):, :])
        outs["hp"].append(hl.reshape(b, d))

        qs, ks, vs, lfs, xrs, grs, gas, gbs = _proj_sample(xs, g1, wq, wk, wv, wf, bf_row, w4)
        lf8 = lfs[:, :N_HEADS]
        attn_s = _attn_decode(l, page_table, qs.reshape(db, 1, a), ks.reshape(db, 1, a), vs.reshape(db, 1, a),
                              lf8.reshape(db, N_HEADS, 1), kt_pool, vt_pool, lt_pool, n_pool).reshape(db, a)
        sc = state_conv[l]
        ys, hn = _lru_sample(xrs, grs, sc[:, 0], sc[:, 1], sc[:, 2], state_h[l], *lru_w)
        x1s, xns = _merge(attn_s, ys, gas, gbs, xs, wa, wb, wo, g2)
        if l % 2 == 0:
            x2s = _ffn(xns, x1s, *ffn_w)
        else:
            x2s = _moe(xns, x1s, _router(xns, wr_pad), *ffn_w)
        xs = _ple(x2s, p_sample[l].reshape(db, -1), g3, wgate, wproj, gf, final)
        outs["ks"].append(ks.reshape(db, 1, N_HEADS, HEAD_DIM))
        outs["vs"].append(vs.reshape(db, 1, N_HEADS, HEAD_DIM))
        outs["ls"].append(lf8.reshape(db, 1, N_HEADS))
        outs["cs"].append(jnp.stack([sc[:, 1], sc[:, 2], xrs], axis=1))
        outs["hs"].append(hn)

    st = lambda name: jnp.stack(outs[name])
    return (xp, xs.reshape(db, 1, d), st("kp"), st("vp"), st("lp"), st("cp"), st("hp"),
            st("ks"), st("vs"), st("ls"), st("cs"), st("hs"))
```

```python
import functools

import jax
import jax.numpy as jnp
from jax import lax
from jax.experimental import pallas as pl
from jax.experimental.pallas import tpu as pltpu

BF16 = jnp.bfloat16
F32 = jnp.float32

EPS = 1e-6
NEG_INF = -1e30
LRU_C = 8.0
N_HEADS = 8
HEAD_DIM = 64
ATTN_WIDTH = N_HEADS * HEAD_DIM
N_LRU_BLOCKS = 8
CONV_WIDTH = 4
N_EXPERTS = 8

LANES = 128
SUBLANES = 8
VMEM_LIMIT_BYTES = 56 * 1024 * 1024

PROJ_TILE = 512
ATTN_TILE = 512
LRU_TILE = 256
ROW_TILE = 512
FFN_ROW_TILE = 1024
FFN_COL_TILE = 512
MOE_ROW_TILE = 512
MOE_COL_TILE = 1792
DECODE_PAGES = 16


def _cparams(*sem):
    return pltpu.CompilerParams(dimension_semantics=sem, vmem_limit_bytes=VMEM_LIMIT_BYTES)


def _full(shape):
    zeros = (0,) * len(shape)
    return pl.BlockSpec(shape, lambda *_: zeros, pipeline_mode=pl.Buffered(1))


def _rmsnorm(x, g):
    return x * lax.rsqrt(jnp.mean(x * x, axis=-1, keepdims=True) + EPS) * g


def _log_sigmoid(z):
    return jnp.minimum(z, 0.0) - jnp.log1p(jnp.exp(-jnp.abs(z)))


def _softplus(z):
    return jnp.maximum(z, 0.0) + jnp.log1p(jnp.exp(-jnp.abs(z)))


def _dot(a, b):
    return jnp.dot(a, b, preferred_element_type=F32)


def _dot_nt(a, b):
    return lax.dot_general(a, b, (((1,), (1,)), ((), ())), preferred_element_type=F32)


def _exact_dot(x, w01):
    hi = x.astype(BF16)
    r1 = x - hi.astype(F32)
    mid = r1.astype(BF16)
    lo = (r1 - mid.astype(F32)).astype(BF16)
    return _dot(hi, w01) + _dot(mid, w01) + _dot(lo, w01)


def _split3(x):
    p1 = x.astype(BF16).astype(F32)
    r1 = x - p1
    p2 = r1.astype(BF16).astype(F32)
    p3 = (r1 - p2).astype(BF16).astype(F32)
    return p1, p2, p3


def _proj_prompt_kernel(x_ref, g_ref, wq_ref, wkt_ref, wvt_ref, wf_ref, wft_ref, bfr_ref, bfc_ref, w4_ref,
                        triu_ref, tril_ref,
                        qa_ref, kta_ref, kt_ref, vt_ref, vtb_ref, lft_ref,
                        xr_ref, gr_ref, ga_ref, gb_ref, carry_t_ref, carry_c_ref):
    t = pl.program_id(1)

    @pl.when(t == 0)
    def _():
        carry_t_ref[...] = jnp.zeros_like(carry_t_ref)
        carry_c_ref[...] = jnp.zeros_like(carry_c_ref)

    tm = x_ref.shape[1]
    xn = _rmsnorm(x_ref[0], g_ref[...]).astype(BF16)
    lft = _log_sigmoid(_dot_nt(wft_ref[...], xn) + bfc_ref[...])[:N_HEADS]
    lft_ref[0] = lft
    lfc = _log_sigmoid(_dot(xn, wf_ref[...]) + bfr_ref[...])
    ct = _exact_dot(lft, triu_ref[...]) + carry_t_ref[:, :1]
    carry_t_ref[...] = jnp.broadcast_to(ct[:, -1:], carry_t_ref.shape)
    tril = tril_ref[...]
    cc = carry_c_ref[0:1, :]
    for piece in _split3(lfc):
        cc = cc + _dot(tril, piece.astype(BF16))
    carry_c_ref[...] = jnp.broadcast_to(cc[-1:, :], carry_c_ref.shape)

    q = _dot(xn, wq_ref[...]) * (HEAD_DIM ** -0.5)
    kt = _dot_nt(wkt_ref[...], xn)
    kt_ref[0] = kt
    lane = lax.broadcasted_iota(jnp.int32, (tm, HEAD_DIM), 1)
    sub = lax.broadcasted_iota(jnp.int32, (HEAD_DIM, tm), 0)
    for h in range(N_HEADS):
        hs = slice(h * HEAD_DIM, (h + 1) * HEAD_DIM)
        q1, q2, q3 = _split3(cc[:, h:h + 1])
        q_extra = jnp.where(lane == 0, q1, jnp.where(lane == 1, q2, jnp.where(lane == 2, q3,
                            jnp.where(lane < 6, 1.0, 0.0))))
        qa_ref[0, h] = jnp.concatenate([q[:, hs], q_extra], axis=1).astype(BF16)
        k1, k2, k3 = _split3(ct[h:h + 1, :])
        k_extra = jnp.where(sub < 3, 1.0, jnp.where(sub == 3, -k1, jnp.where(sub == 4, -k2,
                            jnp.where(sub == 5, -k3, 0.0))))
        kta_ref[0, 2 * h * HEAD_DIM:2 * (h + 1) * HEAD_DIM, :] = jnp.concatenate(
            [kt[hs, :], k_extra], axis=0).astype(BF16)
    vt = _dot_nt(wvt_ref[...], xn)
    vt_ref[0] = vt
    vtb_ref[0] = vt.astype(BF16)
    d = xr_ref.shape[-1]
    xr_ref[0] = _dot(xn, w4_ref[:, 0 * d:1 * d])
    gr_ref[0] = _dot(xn, w4_ref[:, 1 * d:2 * d])
    ga_ref[0] = _dot(xn, w4_ref[:, 2 * d:3 * d])
    gb_ref[0] = _dot(xn, w4_ref[:, 3 * d:4 * d])


def _proj_prompt(x, g, wq, wkt, wvt, wf, bf_row, w4):
    b, t, d = x.shape
    tm = min(PROJ_TILE, t)
    nt = t // tm
    ri = lax.broadcasted_iota(jnp.int32, (tm, tm), 0)
    ci = lax.broadcasted_iota(jnp.int32, (tm, tm), 1)
    triu = (ri <= ci).astype(BF16)
    tril = (ri >= ci).astype(BF16)
    a = ATTN_WIDTH
    seq = lambda w: pl.BlockSpec((1, tm, w), lambda i, j: (i, j, 0))
    tr = lambda r: pl.BlockSpec((1, r, tm), lambda i, j: (i, 0, j))
    out_shape = [
        jax.ShapeDtypeStruct((b, N_HEADS, t, 2 * HEAD_DIM), BF16),
        jax.ShapeDtypeStruct((b, 2 * a, t), BF16),
        jax.ShapeDtypeStruct((b, a, t), F32), jax.ShapeDtypeStruct((b, a, t), F32),
        jax.ShapeDtypeStruct((b, a, t), BF16),
        jax.ShapeDtypeStruct((b, N_HEADS, t), F32),
    ] + [jax.ShapeDtypeStruct((b, t, d), F32)] * 4
    out_specs = [
        pl.BlockSpec((1, N_HEADS, tm, 2 * HEAD_DIM), lambda i, j: (i, 0, j, 0)),
        tr(2 * a), tr(a), tr(a), tr(a), tr(N_HEADS),
        seq(d), seq(d), seq(d), seq(d),
    ]
    args = (x, g, wq, wkt, wvt, wf, wf.T, bf_row, bf_row.reshape(LANES, 1), w4, triu, tril)
    return pl.pallas_call(
        _proj_prompt_kernel,
        grid=(b, nt),
        in_specs=[seq(d)] + [_full(v.shape) for v in args[1:]],
        out_specs=out_specs,
        out_shape=out_shape,
        scratch_shapes=[pltpu.VMEM((N_HEADS, LANES), F32), pltpu.VMEM((N_HEADS, LANES), F32)],
        compiler_params=_cparams("arbitrary", "arbitrary"),
        name="proj_prompt",
    )(*args)


def _proj_sample_kernel(x_ref, g_ref, wq_ref, wk_ref, wv_ref, wf_ref, bf_ref, w4_ref,
                        q_ref, k_ref, v_ref, lf_ref, xr_ref, gr_ref, ga_ref, gb_ref):
    xn = _rmsnorm(x_ref[...], g_ref[...]).astype(BF16)
    q_ref[...] = _dot(xn, wq_ref[...])
    k_ref[...] = _dot(xn, wk_ref[...])
    v_ref[...] = _dot(xn, wv_ref[...])
    lf_ref[...] = _log_sigmoid(_dot(xn, wf_ref[...]) + bf_ref[...])
    d = xr_ref.shape[-1]
    xr_ref[...] = _dot(xn, w4_ref[:, 0 * d:1 * d])
    gr_ref[...] = _dot(xn, w4_ref[:, 1 * d:2 * d])
    ga_ref[...] = _dot(xn, w4_ref[:, 2 * d:3 * d])
    gb_ref[...] = _dot(xn, w4_ref[:, 3 * d:4 * d])


def _proj_sample(x, g, wq, wk, wv, wf, bf_row, w4):
    m, d = x.shape
    a = ATTN_WIDTH
    shapes = [(m, a), (m, a), (m, a), (m, LANES), (m, d), (m, d), (m, d), (m, d)]
    return pl.pallas_call(
        _proj_sample_kernel,
        grid=(1,),
        in_specs=[_full(v.shape) for v in (x, g, wq, wk, wv, wf, bf_row, w4)],
        out_specs=[_full(s) for s in shapes],
        out_shape=[jax.ShapeDtypeStruct(s, F32) for s in shapes],
        compiler_params=_cparams("arbitrary"),
        name="proj_sample",
    )(x, g, wq, wk, wv, wf, bf_row, w4)


def _attn_prompt_kernel(qi_ref, ki_ref, q_ref, kta_ref, vt_ref, o_ref, m_ref, l_ref, acc_ref):
    step = pl.program_id(1)
    qi = qi_ref[step]
    ki = ki_ref[step]
    ta = q_ref.shape[2]

    @pl.when(ki == 0)
    def _():
        m_ref[...] = jnp.full_like(m_ref, NEG_INF)
        l_ref[...] = jnp.zeros_like(l_ref)
        acc_ref[...] = jnp.zeros_like(acc_ref)

    def update(on_diagonal):
        half = ta // 2
        blocks = [(0, half, half), (half, half, ta)] if on_diagonal and half % LANES == 0 else [(0, ta, ta)]
        for r0, nr, nk in blocks:
            rows = slice(r0, r0 + nr)
            if on_diagonal:
                mask = (lax.broadcasted_iota(jnp.int32, (nr, nk), 1)
                        <= lax.broadcasted_iota(jnp.int32, (nr, nk), 0) + r0)
            for h in range(N_HEADS):
                s = _dot(q_ref[0, h, rows, :], kta_ref[0, 2 * h * HEAD_DIM:2 * (h + 1) * HEAD_DIM, 0:nk])
                if on_diagonal:
                    s = jnp.where(mask, s, NEG_INF)
                m_old = m_ref[h, rows, :]
                m_new = jnp.maximum(m_old, jnp.max(s, axis=-1, keepdims=True))
                p = jnp.exp(s - jnp.concatenate([m_new] * (nk // LANES), axis=1))
                alpha = jnp.exp(m_old - m_new)
                l_ref[h, rows, :] = alpha * l_ref[h, rows, :] + jnp.sum(p, axis=-1, keepdims=True)
                pv = _dot_nt(p.astype(BF16), vt_ref[0, h * HEAD_DIM:(h + 1) * HEAD_DIM, 0:nk])
                acc_ref[h, rows, :] = alpha[:, :HEAD_DIM] * acc_ref[h, rows, :] + pv
                m_ref[h, rows, :] = m_new

    @pl.when(ki < qi)
    def _():
        update(False)

    @pl.when(ki == qi)
    def _():
        update(True)
        for h in range(0, N_HEADS, 2):
            pair = [acc_ref[h + j] / l_ref[h + j][:, :HEAD_DIM] for j in range(2)]
            o_ref[0, :, h * HEAD_DIM:(h + 2) * HEAD_DIM] = jnp.concatenate(pair, axis=1).astype(o_ref.dtype)


def _attn_prompt(qa, kta, vtb):
    b, _, t, _ = qa.shape
    ta = min(ATTN_TILE, t)
    nq = t // ta
    pairs = [(i, j) for i in range(nq) for j in range(i + 1)]
    qi_tab = jnp.asarray([p[0] for p in pairs], jnp.int32)
    ki_tab = jnp.asarray([p[1] for p in pairs], jnp.int32)
    a = ATTN_WIDTH
    grid_spec = pltpu.PrefetchScalarGridSpec(
        num_scalar_prefetch=2,
        grid=(b, len(pairs)),
        in_specs=[
            pl.BlockSpec((1, N_HEADS, ta, 2 * HEAD_DIM), lambda i, s, qt, kt: (i, 0, qt[s], 0)),
            pl.BlockSpec((1, 2 * a, ta), lambda i, s, qt, kt: (i, 0, kt[s])),
            pl.BlockSpec((1, a, ta), lambda i, s, qt, kt: (i, 0, kt[s])),
        ],
        out_specs=pl.BlockSpec((1, ta, a), lambda i, s, qt, kt: (i, qt[s], 0)),
        scratch_shapes=[pltpu.VMEM((N_HEADS, ta, LANES), F32), pltpu.VMEM((N_HEADS, ta, LANES), F32),
                        pltpu.VMEM((N_HEADS, ta, HEAD_DIM), F32)],
    )
    return pl.pallas_call(
        _attn_prompt_kernel,
        grid_spec=grid_spec,
        out_shape=jax.ShapeDtypeStruct((b, t, a), BF16),
        compiler_params=_cparams("arbitrary", "arbitrary"),
        name="attn_prompt",
    )(qi_tab, ki_tab, qa, kta, vtb)


def _attn_decode_kernel(n_pg, pt_ref, q_ref, kn_ref, vn_ref, lfn_ref, w_ref, *rest):
    k_refs = rest[0:n_pg]
    v_refs = rest[n_pg:2 * n_pg]
    lt_refs = rest[2 * n_pg:3 * n_pg]
    o_ref = rest[3 * n_pg]
    m_ref, l_ref, c_ref, acc_ref = rest[3 * n_pg + 1:]
    c_step = pl.program_id(1)
    width = acc_ref.shape[1]
    own = (lax.broadcasted_iota(jnp.int32, (N_HEADS, width), 1) // HEAD_DIM
           == lax.broadcasted_iota(jnp.int32, (N_HEADS, width), 0))
    q_rows = jnp.where(own, q_ref[0] * (HEAD_DIM ** -0.5), 0.0)

    @pl.when(c_step == 0)
    def _():
        lane = lax.broadcasted_iota(jnp.int32, (N_HEADS, LANES), 1)
        s0 = jnp.sum(q_rows * kn_ref[0], axis=1, keepdims=True)
        m_ref[...] = jnp.broadcast_to(s0, m_ref.shape)
        l_ref[...] = jnp.where(lane == 0, 1.0, 0.0)
        acc_ref[...] = jnp.broadcast_to(vn_ref[0], acc_ref.shape)
        c_ref[...] = jnp.broadcast_to(lfn_ref[0], c_ref.shape)

    qb = q_rows.astype(BF16)
    carry = c_ref[...]
    s = []
    for i in range(n_pg):
        r = _exact_dot(lt_refs[i][0], w_ref[...])
        s.append(_dot(qb, k_refs[i][...].astype(BF16)) + (r[:, :LANES] + carry))
        carry = carry + r[:, LANES:]
    c_ref[...] = carry

    m_tile = s[0]
    for i in range(1, n_pg):
        m_tile = jnp.maximum(m_tile, s[i])
    m_old = m_ref[...]
    m_new = jnp.maximum(m_old, jnp.max(m_tile, axis=1, keepdims=True))
    alpha = jnp.exp(m_old - m_new)
    l_new = alpha * l_ref[...]
    pv = jnp.zeros(acc_ref.shape, F32)
    for i in range(n_pg):
        p = jnp.exp(s[i] - m_new)
        l_new = l_new + p
        pv = pv + _dot_nt(p.astype(BF16), v_refs[i][...].astype(BF16))
    m_ref[...] = m_new
    l_ref[...] = l_new
    acc_ref[...] = jnp.concatenate([alpha] * (width // LANES), axis=1) * acc_ref[...] + pv

    @pl.when(c_step == pl.num_programs(1) - 1)
    def _():
        den = jnp.sum(l_ref[...], axis=1, keepdims=True)
        o_ref[0] = jnp.sum(jnp.where(own, acc_ref[...] / den, 0.0), axis=0, keepdims=True)


def _attn_decode(layer, page_table, q, kn, vn, lfn_col, kt_pool, vt_pool, lt_pool, n_pool):
    db, n_pages = page_table.shape
    n_pg = min(DECODE_PAGES, n_pages)
    n_chunks = n_pages // n_pg
    page = kt_pool.shape[-1]
    base = layer * n_pool
    rowi = lax.broadcasted_iota(jnp.int32, (page, 2 * page), 0)
    coli = lax.broadcasted_iota(jnp.int32, (page, 2 * page), 1)
    w = ((rowi > coli) | (coli >= page)).astype(BF16)

    def page_of(slot):
        def index(b, c, pt):
            return base + pt[b * n_pages + (n_pages - 1 - (c * n_pg + slot))]
        return index

    per_seq = lambda r, c: pl.BlockSpec((1, r, c), lambda b, ch, pt: (b, 0, 0))

    def kv_specs():
        return [pl.BlockSpec((ATTN_WIDTH, page), (lambda b, ch, pt, f=page_of(i): (f(b, ch, pt), 0)))
                for i in range(n_pg)]

    lt_specs = [pl.BlockSpec((1, N_HEADS, page), (lambda b, ch, pt, f=page_of(i): (f(b, ch, pt), 0, 0)))
                for i in range(n_pg)]
    grid_spec = pltpu.PrefetchScalarGridSpec(
        num_scalar_prefetch=1,
        grid=(db, n_chunks),
        in_specs=[per_seq(1, ATTN_WIDTH), per_seq(1, ATTN_WIDTH), per_seq(1, ATTN_WIDTH), per_seq(N_HEADS, 1),
                  pl.BlockSpec(w.shape, lambda b, ch, pt: (0, 0))] + kv_specs() + kv_specs() + lt_specs,
        out_specs=per_seq(1, ATTN_WIDTH),
        scratch_shapes=[pltpu.VMEM((N_HEADS, LANES), F32), pltpu.VMEM((N_HEADS, LANES), F32),
                        pltpu.VMEM((N_HEADS, LANES), F32), pltpu.VMEM((N_HEADS, ATTN_WIDTH), F32)],
    )
    return pl.pallas_call(
        functools.partial(_attn_decode_kernel, n_pg),
        grid_spec=grid_spec,
        out_shape=jax.ShapeDtypeStruct((db, 1, ATTN_WIDTH), F32),
        compiler_params=_cparams("arbitrary", "arbitrary"),
        name="attn_decode",
    )(page_table.reshape(-1), q, kn, vn, lfn_col, w,
      *([kt_pool] * n_pg), *([vt_pool] * n_pg), *([lt_pool] * n_pg))


def _lru_gates(xc, wrg_ref, brg_ref, wig_ref, big_ref, lam_ref, store):
    xcb = xc.astype(BF16)
    blk = xc.shape[-1] // N_LRU_BLOCKS
    for n in range(N_LRU_BLOCKS):
        cs = slice(n * blk, (n + 1) * blk)
        r = jax.nn.sigmoid(_dot(xcb[:, cs], wrg_ref[n]) + brg_ref[:, cs])
        i = jax.nn.sigmoid(_dot(xcb[:, cs], wig_ref[n]) + big_ref[:, cs])
        log_a = -LRU_C * r * _softplus(-lam_ref[:, cs])
        a = jnp.exp(log_a)
        th = jnp.tanh(log_a)
        mult = jnp.sqrt(-2.0 * th / (1.0 - th))
        store(cs, a, mult * (i * xc[:, cs]))


def _lru_prompt_kernel(xr_ref, gr_ref, cp_ref, h0_ref, cw_ref, cb_ref, wrg_ref, brg_ref, wig_ref,
                       big_ref, lam_ref, y_ref, tail_ref, hl_ref, buf_ref, a_ref, b_ref, h_ref, hc_ref):
    t = pl.program_id(1)
    tt = xr_ref.shape[1]
    halo = SUBLANES

    @pl.when(t == 0)
    def _():
        buf_ref[0:halo, :] = cp_ref[0]
        hc_ref[...] = h0_ref[0]

    x = xr_ref[0]
    buf_ref[halo:, :] = x
    taps = [buf_ref[pl.ds(halo - (CONV_WIDTH - 1) + j, tt), :] * cw_ref[j:j + 1, :]
            for j in range(CONV_WIDTH - 1)]
    taps.append(x * cw_ref[CONV_WIDTH - 1:CONV_WIDTH, :])
    acc = taps[0]
    for term in taps[1:]:
        acc = acc + term
    xc = cb_ref[...] + acc

    def store(cs, a, b):
        a_ref[:, cs] = a
        b_ref[:, cs] = b

    _lru_gates(xc, wrg_ref, brg_ref, wig_ref, big_ref, lam_ref, store)

    grouped = (tt // SUBLANES, SUBLANES, a_ref.shape[1])
    a = a_ref[...].reshape(grouped)
    b = b_ref[...].reshape(grouped)
    in_group = lax.broadcasted_iota(jnp.int32, grouped, 1)
    shift = 1
    while shift < SUBLANES:
        a_prev = pltpu.roll(a, shift, axis=1)
        b_prev = pltpu.roll(b, shift, axis=1)
        take = in_group >= shift
        b = jnp.where(take, a * b_prev + b, b)
        a = jnp.where(take, a * a_prev, a)
        shift *= 2
    a_ref[...] = a.reshape(a_ref.shape)
    b_ref[...] = b.reshape(b_ref.shape)
    h_in = jnp.broadcast_to(hc_ref[...], (SUBLANES, a_ref.shape[1]))
    for g in range(tt // SUBLANES):
        rows = slice(g * SUBLANES, (g + 1) * SUBLANES)
        h_grp = a_ref[rows, :] * h_in + b_ref[rows, :]
        h_ref[rows, :] = h_grp
        h_in = jnp.broadcast_to(h_grp[SUBLANES - 1:, :], h_grp.shape)
    h_last = h_in[:1, :]
    hc_ref[...] = h_last
    hl_ref[0] = h_last
    y_ref[0] = (h_ref[...] * jax.nn.gelu(gr_ref[0])).astype(y_ref.dtype)
    tail = buf_ref[tt:tt + halo, :]
    tail_ref[0] = tail
    buf_ref[0:halo, :] = tail


def _lru_prompt(xr, gr, conv_prev8, h0, cw, cb, wrg, brg, wig, big, lam):
    b, t, d = xr.shape
    tt = min(LRU_TILE, t)
    seq = pl.BlockSpec((1, tt, d), lambda i, j: (i, j, 0))
    per_b = lambda r: pl.BlockSpec((1, r, d), lambda i, j: (i, 0, 0))
    return pl.pallas_call(
        _lru_prompt_kernel,
        grid=(b, t // tt),
        in_specs=[seq, seq, per_b(SUBLANES), per_b(1)] + [_full(v.shape) for v in (cw, cb, wrg, brg, wig, big, lam)],
        out_specs=[seq, per_b(SUBLANES), per_b(1)],
        out_shape=[jax.ShapeDtypeStruct((b, t, d), BF16), jax.ShapeDtypeStruct((b, SUBLANES, d), F32),
                   jax.ShapeDtypeStruct((b, 1, d), F32)],
        scratch_shapes=[pltpu.VMEM((tt + SUBLANES, d), F32), pltpu.VMEM((tt, d), F32),
                        pltpu.VMEM((tt, d), F32), pltpu.VMEM((tt, d), F32), pltpu.VMEM((1, d), F32)],
        compiler_params=_cparams("arbitrary", "arbitrary"),
        name="lru_prompt",
    )(xr, gr, conv_prev8, h0, cw, cb, wrg, brg, wig, big, lam)


def _lru_sample_kernel(xr_ref, gr_ref, c0_ref, c1_ref, c2_ref, h0_ref, cw_ref, cb_ref, wrg_ref, brg_ref,
                       wig_ref, big_ref, lam_ref, y_ref, hn_ref):
    x = xr_ref[...]
    acc = c0_ref[...] * cw_ref[0:1, :]
    acc = acc + c1_ref[...] * cw_ref[1:2, :]
    acc = acc + c2_ref[...] * cw_ref[2:3, :]
    acc = acc + x * cw_ref[3:4, :]
    xc = cb_ref[...] + acc

    def store(cs, a, b):
        hn_ref[:, cs] = a * h0_ref[:, cs] + b

    _lru_gates(xc, wrg_ref, brg_ref, wig_ref, big_ref, lam_ref, store)
    y_ref[...] = (hn_ref[...] * jax.nn.gelu(gr_ref[...])).astype(y_ref.dtype)


def _lru_sample(xr, gr, c0, c1, c2, h0, cw, cb, wrg, brg, wig, big, lam):
    m, d = xr.shape
    args = (xr, gr, c0, c1, c2, h0, cw, cb, wrg, brg, wig, big, lam)
    return pl.pallas_call(
        _lru_sample_kernel,
        grid=(1,),
        in_specs=[_full(v.shape) for v in args],
        out_specs=[_full((m, d)), _full((m, d))],
        out_shape=[jax.ShapeDtypeStruct((m, d), BF16), jax.ShapeDtypeStruct((m, d), F32)],
        compiler_params=_cparams("arbitrary"),
        name="lru_sample",
    )(*args)


def _merge_kernel(attn_ref, y_ref, ga_ref, gb_ref, x_ref, wa_ref, wb_ref, wo_ref, g_ref, x1_ref, xn_ref):
    merged = (jax.nn.sigmoid(ga_ref[...]) * _dot(attn_ref[...].astype(BF16), wa_ref[...])
              + jax.nn.sigmoid(gb_ref[...]) * _dot(y_ref[...], wb_ref[...]))
    x1 = x_ref[...] + _dot(merged.astype(BF16), wo_ref[...])
    x1_ref[...] = x1
    xn_ref[...] = _rmsnorm(x1, g_ref[...]).astype(xn_ref.dtype)


def _merge(attn, y, ga, gb, x, wa, wb, wo, g):
    m, d = x.shape
    tm = min(ROW_TILE, m)
    row = lambda w: pl.BlockSpec((tm, w), lambda i: (i, 0))
    return pl.pallas_call(
        _merge_kernel,
        grid=(m // tm,),
        in_specs=[row(attn.shape[1]), row(d), row(d), row(d), row(d)] + [_full(v.shape) for v in (wa, wb, wo, g)],
        out_specs=[row(d), row(d)],
        out_shape=[jax.ShapeDtypeStruct((m, d), F32), jax.ShapeDtypeStruct((m, d), BF16)],
        compiler_params=_cparams("arbitrary"),
        name="merge",
    )(attn, y, ga, gb, x, wa, wb, wo, g)


def _top2(logits):
    lane = lax.broadcasted_iota(jnp.int32, logits.shape, 1)
    logits = jnp.where(lane < N_EXPERTS, logits, -jnp.inf)
    m1 = jnp.max(logits, axis=-1, keepdims=True)
    i1 = jnp.min(jnp.where(logits == m1, lane, LANES), axis=-1, keepdims=True)
    rest = jnp.where(lane == i1, -jnp.inf, logits)
    m2 = jnp.max(rest, axis=-1, keepdims=True)
    i2 = jnp.min(jnp.where(rest == m2, lane, LANES), axis=-1, keepdims=True)
    e2 = jnp.exp(m2 - m1)
    den = 1.0 + e2
    return lane, i1, i2, 1.0 / den, e2 / den


def _to_row_tiles(dst_ref, x):
    rows = x.shape[0]
    for s in range(x.shape[1] // LANES):
        dst_ref[pl.ds(s, rows, stride=SUBLANES), :] = x[:, s * LANES:(s + 1) * LANES]


def _from_row_tiles(src_ref, rows, s):
    return src_ref[pl.ds(s, rows, stride=SUBLANES), :]


def _merge_route_kernel(attn_ref, y_ref, ga_ref, gb_ref, x_ref, wa_ref, wb_ref, wo_ref, g_ref, wr_ref, tri_ref,
                        x1_ref, xrt_ref, meta_ref, metat_ref, cnt_ref, carry_ref):
    i = pl.program_id(0)

    @pl.when(i == 0)
    def _():
        carry_ref[...] = jnp.zeros_like(carry_ref)

    merged = (jax.nn.sigmoid(ga_ref[...]) * _dot(attn_ref[...].astype(BF16), wa_ref[...])
              + jax.nn.sigmoid(gb_ref[...]) * _dot(y_ref[...], wb_ref[...]))
    x1 = x_ref[...] + _dot(merged.astype(BF16), wo_ref[...])
    x1_ref[...] = x1
    xn = _rmsnorm(x1, g_ref[...])
    _to_row_tiles(xrt_ref, xn)
    lane, i1, i2, w1, w2 = _top2(_dot(xn.astype(BF16), wr_ref[...]))
    oh1 = lane == i1
    oh2 = lane == i2
    tri = tri_ref[...]
    before1 = _dot(tri, oh1.astype(BF16))
    before2 = _dot(tri, oh2.astype(BF16))
    cnt1 = jnp.sum(oh1.astype(F32), axis=0, keepdims=True)
    cnt2 = jnp.sum(oh2.astype(F32), axis=0, keepdims=True)
    carry = carry_ref[0:1, :]
    rank1 = jnp.sum(jnp.where(oh1, carry + before1, 0.0), axis=-1, keepdims=True)
    rank2 = jnp.sum(jnp.where(oh2, carry + cnt1 + before2, 0.0), axis=-1, keepdims=True)
    total = carry + cnt1 + cnt2
    carry_ref[...] = jnp.broadcast_to(total, carry_ref.shape)
    cnt_ref[...] = jnp.broadcast_to(total, cnt_ref.shape)
    cols = (i1.astype(F32), i2.astype(F32), rank1, rank2, w1, w2)
    meta = jnp.zeros(meta_ref.shape, F32)
    for k, v in enumerate(cols):
        meta = jnp.where(lane == k, v, meta)
    meta_ref[...] = meta
    sel = (lax.broadcasted_iota(jnp.int32, (SUBLANES, LANES), 0)
           == lax.broadcasted_iota(jnp.int32, (SUBLANES, LANES), 1)).astype(BF16)
    meta_t = jnp.zeros(metat_ref.shape, F32)
    for piece in _split3(meta):
        meta_t = meta_t + _dot_nt(sel, piece.astype(BF16))
    metat_ref[...] = meta_t


def _merge_route(attn, y, ga, gb, x, wa, wb, wo, g, wr_pad):
    m, d = x.shape
    tm = min(ROW_TILE, m)
    tri = (lax.broadcasted_iota(jnp.int32, (tm, tm), 0)
           > lax.broadcasted_iota(jnp.int32, (tm, tm), 1)).astype(BF16)
    row = lambda w: pl.BlockSpec((tm, w), lambda i: (i, 0))
    return pl.pallas_call(
        _merge_route_kernel,
        grid=(m // tm,),
        in_specs=[row(attn.shape[1]), row(d), row(d), row(d), row(d)]
        + [_full(v.shape) for v in (wa, wb, wo, g, wr_pad, tri)],
        out_specs=[row(d), pl.BlockSpec((tm * SUBLANES, LANES), lambda i: (i, 0)), row(LANES),
                   pl.BlockSpec((SUBLANES, tm), lambda i: (0, i)),
                   pl.BlockSpec((SUBLANES, LANES), lambda i: (0, 0))],
        out_shape=[jax.ShapeDtypeStruct((m, d), F32), jax.ShapeDtypeStruct((m * SUBLANES, LANES), F32),
                   jax.ShapeDtypeStruct((m, LANES), F32), jax.ShapeDtypeStruct((SUBLANES, m), F32),
                   jax.ShapeDtypeStruct((SUBLANES, LANES), F32)],
        scratch_shapes=[pltpu.VMEM((SUBLANES, LANES), F32)],
        compiler_params=_cparams("arbitrary"),
        name="merge_route",
    )(attn, y, ga, gb, x, wa, wb, wo, g, wr_pad, tri)


def _row_copy(src_ref, src_row, dst_ref, dst_row, sem):
    return pltpu.make_async_copy(
        src_ref.at[pl.ds(pl.multiple_of(src_row * SUBLANES, SUBLANES), SUBLANES)],
        dst_ref.at[pl.ds(pl.multiple_of(dst_row * SUBLANES, SUBLANES), SUBLANES)], sem)


def _dispatch_kernel(d1_ref, d2_ref, x_ref, init_ref, xs_ref, sem):
    del init_ref
    tm = x_ref.shape[0] // SUBLANES
    base = pl.program_id(0) * tm

    def issue(r, carry):
        _row_copy(x_ref, r, xs_ref, d1_ref[base + r], sem).start(priority=0)
        _row_copy(x_ref, r, xs_ref, d2_ref[base + r], sem).start(priority=1)
        return carry

    lax.fori_loop(0, tm, issue, 0)

    def drain(r, carry):
        _row_copy(x_ref, r, xs_ref, d1_ref[base + r], sem).wait()
        _row_copy(x_ref, r, xs_ref, d2_ref[base + r], sem).wait()
        return carry

    lax.fori_loop(0, tm, drain, 0)


def _dispatch(dest1, dest2, x_rt, n_rows):
    m = dest1.shape[0]
    tm = min(ROW_TILE, m)
    grid_spec = pltpu.PrefetchScalarGridSpec(
        num_scalar_prefetch=2,
        grid=(m // tm,),
        in_specs=[pl.BlockSpec((tm * SUBLANES, LANES), lambda i, d1, d2: (i, 0)),
                  pl.BlockSpec(memory_space=pl.ANY)],
        out_specs=pl.BlockSpec(memory_space=pl.ANY),
        scratch_shapes=[pltpu.SemaphoreType.DMA],
    )
    return pl.pallas_call(
        _dispatch_kernel,
        grid_spec=grid_spec,
        out_shape=jax.ShapeDtypeStruct((n_rows * SUBLANES, LANES), F32),
        input_output_aliases={3: 0},
        compiler_params=_cparams("arbitrary"),
        name="moe_dispatch",
    )(dest1, dest2, x_rt, jnp.zeros((n_rows * SUBLANES, LANES), F32))


def _moe_ffn_kernel(te_ref, nu_ref, xs_ref, wg_ref, wu_ref, wd_ref, ys_ref, xb_ref, acc_ref):
    i = pl.program_id(0)
    f = pl.program_id(1)
    rows, d = xb_ref.shape

    @pl.when(i < nu_ref[0])
    def _():
        @pl.when(f == 0)
        def _():
            for s in range(d // LANES):
                xb_ref[:, s * LANES:(s + 1) * LANES] = _from_row_tiles(xs_ref, rows, s).astype(BF16)
            acc_ref[...] = jnp.zeros_like(acc_ref)

        xb = xb_ref[...]
        hid = jax.nn.silu(_dot(xb, wg_ref[0])) * _dot(xb, wu_ref[0])
        acc_ref[...] += _dot(hid.astype(BF16), wd_ref[0])

        @pl.when(f == pl.num_programs(1) - 1)
        def _():
            _to_row_tiles(ys_ref, acc_ref[...])

    @pl.when((i >= nu_ref[0]) & (f == pl.num_programs(1) - 1))
    def _():
        ys_ref[...] = jnp.zeros_like(ys_ref)


def _moe_ffn(tile_expert, n_used, xs, wg, wu, wd):
    n_e, d, ff = wg.shape
    tm = MOE_ROW_TILE
    tf = MOE_COL_TILE
    nf = ff // tf
    n_tiles = tile_expert.shape[0]

    def tile(i, f, te, nu):
        return (i, 0)

    def col(i, f, te, nu):
        return jnp.where(i < nu[0], f, nf - 1)

    grid_spec = pltpu.PrefetchScalarGridSpec(
        num_scalar_prefetch=2,
        grid=(n_tiles, nf),
        in_specs=[pl.BlockSpec((tm * SUBLANES, LANES), tile),
                  pl.BlockSpec((1, d, tf), lambda i, f, te, nu: (te[i], 0, col(i, f, te, nu))),
                  pl.BlockSpec((1, d, tf), lambda i, f, te, nu: (te[i], 0, col(i, f, te, nu))),
                  pl.BlockSpec((1, tf, d), lambda i, f, te, nu: (te[i], col(i, f, te, nu), 0))],
        out_specs=pl.BlockSpec((tm * SUBLANES, LANES), tile),
        scratch_shapes=[pltpu.VMEM((tm, d), BF16), pltpu.VMEM((tm, d), F32)],
    )
    return pl.pallas_call(
        _moe_ffn_kernel,
        grid_spec=grid_spec,
        out_shape=jax.ShapeDtypeStruct(xs.shape, F32),
        compiler_params=_cparams("arbitrary", "arbitrary"),
        name="moe_ffn",
    )(tile_expert, n_used, xs, wg, wu, wd)


def _ple_math(final, x, p_ref, g_ref, wgate_ref, wproj_ref, gf_ref):
    gate = jax.nn.sigmoid(_dot(_rmsnorm(x, g_ref[...]).astype(BF16), wgate_ref[...]))
    x3 = x + gate * _dot(p_ref[...].astype(BF16), wproj_ref[...])
    return _rmsnorm(x3, gf_ref[...]) if final else x3


def _combine_ple_kernel(final, d1_ref, d2_ref, x1_ref, meta_ref, p_ref, g_ref, wgate_ref, wproj_ref, gf_ref,
                        ys_ref, o_ref, g1_ref, g2_ref, x2_ref, sem):
    tm, d = x1_ref.shape
    i = pl.program_id(0)
    slot = i % 2

    def copies(tile, buf, r):
        base = tile * tm
        return (_row_copy(ys_ref, d1_ref[base + r], g1_ref.at[buf], r, sem.at[buf]),
                _row_copy(ys_ref, d2_ref[base + r], g2_ref.at[buf], r, sem.at[buf]))

    def fetch(tile, buf):
        def issue(r, carry):
            c1, c2 = copies(tile, buf, r)
            c1.start(priority=0)
            c2.start(priority=1)
            return carry
        lax.fori_loop(0, tm, issue, 0)

    @pl.when(i == 0)
    def _():
        fetch(0, 0)

    @pl.when(i + 1 < pl.num_programs(0))
    def _():
        fetch(i + 1, 1 - slot)

    def drain(r, carry):
        c1, c2 = copies(i, slot, r)
        c1.wait()
        c2.wait()
        return carry

    lax.fori_loop(0, tm, drain, 0)
    w1 = meta_ref[:, 4:5]
    w2 = meta_ref[:, 5:6]
    for s in range(d // LANES):
        cs = slice(s * LANES, (s + 1) * LANES)
        x2_ref[:, cs] = (x1_ref[:, cs] + w1 * _from_row_tiles(g1_ref.at[slot], tm, s)
                         + w2 * _from_row_tiles(g2_ref.at[slot], tm, s))
    o_ref[...] = _ple_math(final, x2_ref[...], p_ref, g_ref, wgate_ref, wproj_ref, gf_ref)


def _combine_ple(dest1, dest2, x1, meta, ys, p, g, wgate, wproj, g_final, final):
    m, d = x1.shape
    tm = min(ROW_TILE, m)
    row = lambda w: pl.BlockSpec((tm, w), lambda i, d1, d2: (i, 0))
    grid_spec = pltpu.PrefetchScalarGridSpec(
        num_scalar_prefetch=2,
        grid=(m // tm,),
        in_specs=[row(d), row(LANES), row(p.shape[1])] + [_full(v.shape) for v in (g, wgate, wproj, g_final)]
        + [pl.BlockSpec(memory_space=pl.ANY)],
        out_specs=row(d),
        scratch_shapes=[pltpu.VMEM((2, tm * SUBLANES, LANES), F32), pltpu.VMEM((2, tm * SUBLANES, LANES), F32),
                        pltpu.VMEM((tm, d), F32), pltpu.SemaphoreType.DMA((2,))],
    )
    return pl.pallas_call(
        functools.partial(_combine_ple_kernel, final),
        grid_spec=grid_spec,
        out_shape=jax.ShapeDtypeStruct((m, d), F32),
        compiler_params=_cparams("arbitrary"),
        name="moe_combine_ple",
    )(dest1, dest2, x1, meta, p, g, wgate, wproj, g_final, ys)


def _moe_sorted(x1, x_rt, meta, meta_t, counts, wg, wu, wd, p, g, wgate, wproj, g_final, final):
    m = x1.shape[0]
    tm = MOE_ROW_TILE
    n_tiles = (2 * m) // tm + N_EXPERTS
    cnt = counts[0, :N_EXPERTS].astype(jnp.int32)
    padded = ((cnt + tm - 1) // tm) * tm
    ends = jnp.cumsum(padded)
    starts = ends - padded
    experts = jnp.arange(N_EXPERTS, dtype=jnp.int32)[:, None]
    fields = meta_t.astype(jnp.int32)

    def dest(expert_row, rank_row):
        return jnp.sum(jnp.where(fields[expert_row][None, :] == experts, starts[:, None], 0), axis=0) + fields[rank_row]

    dest1 = dest(0, 2)
    dest2 = dest(1, 3)
    n_used = (ends[-1] // tm).reshape(1)
    tile_start = jnp.minimum(jnp.arange(n_tiles, dtype=jnp.int32), n_used - 1) * tm
    tile_expert = jnp.sum(ends[None, :] <= tile_start[:, None], axis=1).astype(jnp.int32)
    xs = _dispatch(dest1, dest2, x_rt, n_tiles * tm)
    ys = _moe_ffn(tile_expert, n_used.astype(jnp.int32), xs, wg, wu, wd)
    return _combine_ple(dest1, dest2, x1, meta, ys, p, g, wgate, wproj, g_final, final)


def _ffn_kernel(xn_ref, x1_ref, wg_ref, wu_ref, wd_ref, o_ref, acc_ref):
    f = pl.program_id(1)

    @pl.when(f == 0)
    def _():
        acc_ref[...] = jnp.zeros_like(acc_ref)

    xn = xn_ref[...]
    hid = jax.nn.silu(_dot(xn, wg_ref[...])) * _dot(xn, wu_ref[...])
    acc_ref[...] += _dot(hid.astype(BF16), wd_ref[...])

    @pl.when(f == pl.num_programs(1) - 1)
    def _():
        o_ref[...] = x1_ref[...] + acc_ref[...]


def _ffn(xn, x1, wg, wu, wd):
    m, d = x1.shape
    ff = wg.shape[1]
    tm = min(FFN_ROW_TILE, m)
    tf = FFN_COL_TILE
    row = pl.BlockSpec((tm, d), lambda i, f: (i, 0))
    return pl.pallas_call(
        _ffn_kernel,
        grid=(m // tm, ff // tf),
        in_specs=[row, row, pl.BlockSpec((d, tf), lambda i, f: (0, f)), pl.BlockSpec((d, tf), lambda i, f: (0, f)),
                  pl.BlockSpec((tf, d), lambda i, f: (f, 0))],
        out_specs=row,
        out_shape=jax.ShapeDtypeStruct((m, d), F32),
        scratch_shapes=[pltpu.VMEM((tm, d), F32)],
        compiler_params=_cparams("arbitrary", "arbitrary"),
        name="ffn_dense",
    )(xn, x1, wg, wu, wd)


def _router_kernel(xn_ref, wr_ref, gates_ref):
    lane, i1, i2, w1, w2 = _top2(_dot(xn_ref[...], wr_ref[...]))
    gates_ref[...] = jnp.where(lane == i1, w1, 0.0) + jnp.where(lane == i2, w2, 0.0)


def _router(xn, wr_pad):
    m, d = xn.shape
    tm = min(ROW_TILE, m)
    return pl.pallas_call(
        _router_kernel,
        grid=(m // tm,),
        in_specs=[pl.BlockSpec((tm, d), lambda i: (i, 0)), _full(wr_pad.shape)],
        out_specs=pl.BlockSpec((tm, LANES), lambda i: (i, 0)),
        out_shape=jax.ShapeDtypeStruct((m, LANES), F32),
        compiler_params=_cparams("arbitrary"),
        name="router",
    )(xn, wr_pad)


def _moe_kernel(xn_ref, x1_ref, gates_ref, wg_ref, wu_ref, wd_ref, o_ref, acc_ref):
    e = pl.program_id(1)
    f = pl.program_id(2)

    @pl.when((e == 0) & (f == 0))
    def _():
        acc_ref[...] = jnp.zeros_like(acc_ref)

    xn = xn_ref[...]
    hid = jax.nn.silu(_dot(xn, wg_ref[0])) * _dot(xn, wu_ref[0])
    gates = gates_ref[...]
    lane = lax.broadcasted_iota(jnp.int32, gates.shape, 1)
    gate = jnp.sum(jnp.where(lane == e, gates, 0.0), axis=-1, keepdims=True)
    acc_ref[...] += gate * _dot(hid.astype(BF16), wd_ref[0])

    @pl.when((e == pl.num_programs(1) - 1) & (f == pl.num_programs(2) - 1))
    def _():
        o_ref[...] = x1_ref[...] + acc_ref[...]


def _moe(xn, x1, gates, wg, wu, wd):
    m, d = x1.shape
    n_e, _, ff = wg.shape
    tm = min(FFN_ROW_TILE, m)
    tf = FFN_COL_TILE
    row = lambda w: pl.BlockSpec((tm, w), lambda i, e, f: (i, 0))
    return pl.pallas_call(
        _moe_kernel,
        grid=(m // tm, n_e, ff // tf),
        in_specs=[row(d), row(d), row(LANES),
                  pl.BlockSpec((1, d, tf), lambda i, e, f: (e, 0, f)),
                  pl.BlockSpec((1, d, tf), lambda i, e, f: (e, 0, f)),
                  pl.BlockSpec((1, tf, d), lambda i, e, f: (e, f, 0))],
        out_specs=row(d),
        out_shape=jax.ShapeDtypeStruct((m, d), F32),
        scratch_shapes=[pltpu.VMEM((tm, d), F32)],
        compiler_params=_cparams("arbitrary", "arbitrary", "arbitrary"),
        name="moe_dense",
    )(xn, x1, gates, wg, wu, wd)


def _ple_kernel(final, x_ref, p_ref, g_ref, wgate_ref, wproj_ref, gf_ref, o_ref):
    o_ref[...] = _ple_math(final, x_ref[...], p_ref, g_ref, wgate_ref, wproj_ref, gf_ref)


def _ple(x, p, g, wgate, wproj, g_final, final):
    m, d = x.shape
    tm = min(ROW_TILE, m)
    row = lambda w: pl.BlockSpec((tm, w), lambda i: (i, 0))
    return pl.pallas_call(
        functools.partial(_ple_kernel, final),
        grid=(m // tm,),
        in_specs=[row(d), row(p.shape[1])] + [_full(v.shape) for v in (g, wgate, wproj, g_final)],
        out_specs=row(d),
        out_shape=jax.ShapeDtypeStruct((m, d), F32),
        compiler_params=_cparams("arbitrary"),
        name="ple",
    )(x, p, g, wgate, wproj, g_final)


def _row(v):
    return v.reshape(1, -1)


def kernel(x_prompt, x_sample, p_prompt, p_sample, cache_k, cache_v, cache_logf, state_conv, state_h, page_table, g_mix, w_in, b_f, conv_w, conv_b, w_rg, b_rg, w_ig, b_ig, lru_lambda, w_a_up, w_b_up, w_out, g_ffn, dense_wg, dense_wu, dense_wd, moe_router, moe_wg, moe_wu, moe_wd, g_ple, w_ple_gate, w_ple_proj, g_final):
    depth = g_mix.shape[0]
    b, t, d = x_prompt.shape
    db = x_sample.shape[0]
    n_pool, page = cache_k.shape[1], cache_k.shape[2]
    a = ATTN_WIDTH
    kt_pool = jnp.transpose(cache_k, (0, 1, 3, 4, 2)).reshape(depth * n_pool * ATTN_WIDTH, page)
    vt_pool = jnp.transpose(cache_v, (0, 1, 3, 4, 2)).reshape(depth * n_pool * ATTN_WIDTH, page)
    lt_pool = jnp.transpose(cache_logf, (0, 1, 3, 2)).reshape(depth * n_pool, N_HEADS, page)

    xp = x_prompt
    xs = x_sample.reshape(db, d)
    outs = {name: [] for name in ("kp", "vp", "lp", "cp", "hp", "ks", "vs", "ls", "cs", "hs")}
    for l in range(depth):
        wl = w_in[l]
        wq = wl[:, 0:a].astype(BF16)
        wk = wl[:, a:2 * a].astype(BF16)
        wv = wl[:, 2 * a:3 * a].astype(BF16)
        wf = jnp.pad(wl[:, 3 * a:3 * a + N_HEADS], ((0, 0), (0, LANES - N_HEADS))).astype(BF16)
        w4 = wl[:, 3 * a + N_HEADS:].astype(BF16)
        bf_row = jnp.pad(b_f[l], (0, LANES - N_HEADS)).reshape(1, LANES)
        g1 = _row(g_mix[l])
        lru_w = (conv_w[l], _row(conv_b[l]), w_rg[l].astype(BF16), _row(b_rg[l]), w_ig[l].astype(BF16),
                 _row(b_ig[l]), _row(lru_lambda[l]))
        wa = w_a_up[l].astype(BF16)
        wb = w_b_up[l].astype(BF16)
        wo = w_out[l].astype(BF16)
        g2 = _row(g_ffn[l])
        g3 = _row(g_ple[l])
        wgate = w_ple_gate[l].astype(BF16)
        wproj = w_ple_proj[l].astype(BF16)
        gf = _row(g_final)
        final = l == depth - 1
        mi = l // 2
        if l % 2 == 0:
            ffn_w = (dense_wg[mi].astype(BF16), dense_wu[mi].astype(BF16), dense_wd[mi].astype(BF16))
        else:
            wr_pad = jnp.pad(moe_router[mi], ((0, 0), (0, LANES - N_EXPERTS))).astype(BF16)
            ffn_w = (moe_wg[mi].astype(BF16), moe_wu[mi].astype(BF16), moe_wd[mi].astype(BF16))

        qa, kta, kt, vt, vtb, lft, xr, gr, ga, gb = _proj_prompt(xp, g1, wq, wk.T, wv.T, wf, bf_row, w4)
        attn = _attn_prompt(qa, kta, vtb)
        y, tail, hl = _lru_prompt(xr, gr, jnp.zeros((b, SUBLANES, d), F32), jnp.zeros((b, 1, d), F32), *lru_w)
        m = b * t
        merge_in = (attn.reshape(m, a), y.reshape(m, d), ga.reshape(m, d), gb.reshape(m, d),
                    xp.reshape(m, d), wa, wb, wo, g2)
        ple_in = (p_prompt[l].reshape(m, -1), g3, wgate, wproj, gf, final)
        if l % 2 == 0:
            x1, xn = _merge(*merge_in)
            xp = _ple(_ffn(xn, x1, *ffn_w), *ple_in).reshape(b, t, d)
        else:
            x1, x_rt, meta, meta_t, counts = _merge_route(*merge_in, wr_pad)
            xp = _moe_sorted(x1, x_rt, meta, meta_t, counts, *ffn_w, *ple_in).reshape(b, t, d)
        outs["kp"].append(jnp.transpose(kt.reshape(b, N_HEADS, HEAD_DIM, t), (0, 3, 1, 2)))
        outs["vp"].append(jnp.transpose(vt.reshape(b, N_HEADS, HEAD_DIM, t), (0, 3, 1, 2)))
        outs["lp"].append(jnp.transpose(lft, (0, 2, 1)))
        outs["cp"].append(tail[:, SUBLANES - (CONV_WIDTH - 1):, :])
        outs["hp"].append(hl.reshape(b, d))

        qs, ks, vs, lfs, xrs, grs, gas, gbs = _proj_sample(xs, g1, wq, wk, wv, wf, bf_row, w4)
        lf8 = lfs[:, :N_HEADS]
        attn_s = _attn_decode(l, page_table, qs.reshape(db, 1, a), ks.reshape(db, 1, a), vs.reshape(db, 1, a),
                              lf8.reshape(db, N_HEADS, 1), kt_pool, vt_pool, lt_pool, n_pool).reshape(db, a)
        sc = state_conv[l]
        ys, hn = _lru_sample(xrs, grs, sc[:, 0], sc[:, 1], sc[:, 2], state_h[l], *lru_w)
        x1s, xns = _merge(attn_s, ys, gas, gbs, xs, wa, wb, wo, g2)
        if l % 2 == 0:
            x2s = _ffn(xns, x1s, *ffn_w)
        else:
            x2s = _moe(xns, x1s, _router(xns, wr_pad), *ffn_w)
        xs = _ple(x2s, p_sample[l].reshape(db, -1), g3, wgate, wproj, gf, final)
        outs["ks"].append(ks.reshape(db, 1, N_HEADS, HEAD_DIM))
        outs["vs"].append(vs.reshape(db, 1, N_HEADS, HEAD_DIM))
        outs["ls"].append(lf8.reshape(db, 1, N_HEADS))
        outs["cs"].append(jnp.stack([sc[:, 1], sc[:, 2], xrs], axis=1))
        outs["hs"].append(hn)

    st = lambda name: jnp.stack(outs[name])
    return (xp, xs.reshape(db, 1, d), st("kp"), st("vp"), st("lp"), st("cp"), st("hp"),
            st("ks"), st("vs"), st("ls"), st("cs"), st("hs"))
```

```python
import functools

import jax
import jax.numpy as jnp
from jax import lax
from jax.experimental import pallas as pl
from jax.experimental.pallas import tpu as pltpu

BF16 = jnp.bfloat16
F32 = jnp.float32

EPS = 1e-6
NEG_INF = -1e30
LRU_C = 8.0
N_HEADS = 8
HEAD_DIM = 64
ATTN_WIDTH = N_HEADS * HEAD_DIM
N_LRU_BLOCKS = 8
CONV_WIDTH = 4
N_EXPERTS = 8

LANES = 128
SUBLANES = 8
VMEM_LIMIT_BYTES = 56 * 1024 * 1024

PROJ_TILE = 512
ATTN_TILE = 512
LRU_TILE = 256
ROW_TILE = 512
FFN_ROW_TILE = 1024
FFN_COL_TILE = 512
MOE_ROW_TILE = 512
MOE_COL_TILE = 1792
DECODE_PAGES = 16


def _cparams(*sem):
    return pltpu.CompilerParams(dimension_semantics=sem, vmem_limit_bytes=VMEM_LIMIT_BYTES)


def _full(shape):
    zeros = (0,) * len(shape)
    return pl.BlockSpec(shape, lambda *_: zeros, pipeline_mode=pl.Buffered(1))


def _rmsnorm(x, g):
    return x * lax.rsqrt(jnp.mean(x * x, axis=-1, keepdims=True) + EPS) * g


def _log_sigmoid(z):
    return jnp.minimum(z, 0.0) - jnp.log1p(jnp.exp(-jnp.abs(z)))


def _softplus(z):
    return jnp.maximum(z, 0.0) + jnp.log1p(jnp.exp(-jnp.abs(z)))


def _dot(a, b):
    return jnp.dot(a, b, preferred_element_type=F32)


def _dot_nt(a, b):
    return lax.dot_general(a, b, (((1,), (1,)), ((), ())), preferred_element_type=F32)


def _exact_dot(x, w01):
    hi = x.astype(BF16)
    r1 = x - hi.astype(F32)
    mid = r1.astype(BF16)
    lo = (r1 - mid.astype(F32)).astype(BF16)
    return _dot(hi, w01) + _dot(mid, w01) + _dot(lo, w01)


def _split3(x):
    p1 = x.astype(BF16).astype(F32)
    r1 = x - p1
    p2 = r1.astype(BF16).astype(F32)
    p3 = (r1 - p2).astype(BF16).astype(F32)
    return p1, p2, p3


def _proj_prompt_kernel(x_ref, g_ref, wq_ref, wkt_ref, wvt_ref, wf_ref, wft_ref, bfr_ref, bfc_ref, w4_ref,
                        triu_ref, tril_ref, kt_all_ref, vt_all_ref,
                        qa_ref, kta_ref, kt_ref, vt_ref, vtb_ref, lft_ref,
                        xr_ref, gr_ref, ga_ref, gb_ref, carry_t_ref, carry_c_ref):
    del kt_all_ref, vt_all_ref
    t = pl.program_id(1)

    @pl.when(t == 0)
    def _():
        carry_t_ref[...] = jnp.zeros_like(carry_t_ref)
        carry_c_ref[...] = jnp.zeros_like(carry_c_ref)

    tm = x_ref.shape[1]
    xn = _rmsnorm(x_ref[0], g_ref[...]).astype(BF16)
    lft = _log_sigmoid(_dot_nt(wft_ref[...], xn) + bfc_ref[...])[:N_HEADS]
    lft_ref[0] = lft
    lfc = _log_sigmoid(_dot(xn, wf_ref[...]) + bfr_ref[...])
    ct = _exact_dot(lft, triu_ref[...]) + carry_t_ref[:, :1]
    carry_t_ref[...] = jnp.broadcast_to(ct[:, -1:], carry_t_ref.shape)
    tril = tril_ref[...]
    cc = carry_c_ref[0:1, :]
    for piece in _split3(lfc):
        cc = cc + _dot(tril, piece.astype(BF16))
    carry_c_ref[...] = jnp.broadcast_to(cc[-1:, :], carry_c_ref.shape)

    q = _dot(xn, wq_ref[...]) * (HEAD_DIM ** -0.5)
    kt = _dot_nt(wkt_ref[...], xn)
    kt_ref[0, 0] = kt
    lane = lax.broadcasted_iota(jnp.int32, (tm, HEAD_DIM), 1)
    sub = lax.broadcasted_iota(jnp.int32, (HEAD_DIM, tm), 0)
    for h in range(N_HEADS):
        hs = slice(h * HEAD_DIM, (h + 1) * HEAD_DIM)
        q1, q2, q3 = _split3(cc[:, h:h + 1])
        q_extra = jnp.where(lane == 0, q1, jnp.where(lane == 1, q2, jnp.where(lane == 2, q3,
                            jnp.where(lane < 6, 1.0, 0.0))))
        qa_ref[0, h] = jnp.concatenate([q[:, hs], q_extra], axis=1).astype(BF16)
        k1, k2, k3 = _split3(ct[h:h + 1, :])
        k_extra = jnp.where(sub < 3, 1.0, jnp.where(sub == 3, -k1, jnp.where(sub == 4, -k2,
                            jnp.where(sub == 5, -k3, 0.0))))
        kta_ref[0, 2 * h * HEAD_DIM:2 * (h + 1) * HEAD_DIM, :] = jnp.concatenate(
            [kt[hs, :], k_extra], axis=0).astype(BF16)
    vt = _dot_nt(wvt_ref[...], xn)
    vt_ref[0, 0] = vt
    vtb_ref[0] = vt.astype(BF16)
    d = xr_ref.shape[-1]
    xr_ref[0] = _dot(xn, w4_ref[:, 0 * d:1 * d])
    gr_ref[0] = _dot(xn, w4_ref[:, 1 * d:2 * d])
    ga_ref[0] = _dot(xn, w4_ref[:, 2 * d:3 * d])
    gb_ref[0] = _dot(xn, w4_ref[:, 3 * d:4 * d])


def _proj_prompt(x, g, wq, wkt, wvt, wf, bf_row, w4, layer, kt_all, vt_all):
    b, t, d = x.shape
    tm = min(PROJ_TILE, t)
    nt = t // tm
    ri = lax.broadcasted_iota(jnp.int32, (tm, tm), 0)
    ci = lax.broadcasted_iota(jnp.int32, (tm, tm), 1)
    triu = (ri <= ci).astype(BF16)
    tril = (ri >= ci).astype(BF16)
    a = ATTN_WIDTH
    seq = lambda w: pl.BlockSpec((1, tm, w), lambda i, j: (i, j, 0))
    tr = lambda r: pl.BlockSpec((1, r, tm), lambda i, j: (i, 0, j))
    out_shape = [
        jax.ShapeDtypeStruct((b, N_HEADS, t, 2 * HEAD_DIM), BF16),
        jax.ShapeDtypeStruct((b, 2 * a, t), BF16),
        jax.ShapeDtypeStruct(kt_all.shape, F32), jax.ShapeDtypeStruct(vt_all.shape, F32),
        jax.ShapeDtypeStruct((b, a, t), BF16),
        jax.ShapeDtypeStruct((b, N_HEADS, t), F32),
    ] + [jax.ShapeDtypeStruct((b, t, d), F32)] * 4
    stacked = pl.BlockSpec((1, 1, a, tm), lambda i, j: (layer, i, 0, j))
    out_specs = [
        pl.BlockSpec((1, N_HEADS, tm, 2 * HEAD_DIM), lambda i, j: (i, 0, j, 0)),
        tr(2 * a), stacked, stacked, tr(a), tr(N_HEADS),
        seq(d), seq(d), seq(d), seq(d),
    ]
    args = (x, g, wq, wkt, wvt, wf, wf.T, bf_row, bf_row.reshape(LANES, 1), w4, triu, tril)
    return pl.pallas_call(
        _proj_prompt_kernel,
        grid=(b, nt),
        in_specs=[seq(d)] + [_full(v.shape) for v in args[1:]] + [pl.BlockSpec(memory_space=pl.ANY)] * 2,
        out_specs=out_specs,
        out_shape=out_shape,
        input_output_aliases={len(args): 2, len(args) + 1: 3},
        scratch_shapes=[pltpu.VMEM((N_HEADS, LANES), F32), pltpu.VMEM((N_HEADS, LANES), F32)],
        compiler_params=_cparams("arbitrary", "arbitrary"),
        name="proj_prompt",
    )(*args, kt_all, vt_all)


def _proj_sample_kernel(x_ref, g_ref, wq_ref, wk_ref, wv_ref, wf_ref, bf_ref, w4_ref,
                        q_ref, k_ref, v_ref, lf_ref, xr_ref, gr_ref, ga_ref, gb_ref):
    xn = _rmsnorm(x_ref[...], g_ref[...]).astype(BF16)
    q_ref[...] = _dot(xn, wq_ref[...])
    k_ref[...] = _dot(xn, wk_ref[...])
    v_ref[...] = _dot(xn, wv_ref[...])
    lf_ref[...] = _log_sigmoid(_dot(xn, wf_ref[...]) + bf_ref[...])
    d = xr_ref.shape[-1]
    xr_ref[...] = _dot(xn, w4_ref[:, 0 * d:1 * d])
    gr_ref[...] = _dot(xn, w4_ref[:, 1 * d:2 * d])
    ga_ref[...] = _dot(xn, w4_ref[:, 2 * d:3 * d])
    gb_ref[...] = _dot(xn, w4_ref[:, 3 * d:4 * d])


def _proj_sample(x, g, wq, wk, wv, wf, bf_row, w4):
    m, d = x.shape
    a = ATTN_WIDTH
    shapes = [(m, a), (m, a), (m, a), (m, LANES), (m, d), (m, d), (m, d), (m, d)]
    return pl.pallas_call(
        _proj_sample_kernel,
        grid=(1,),
        in_specs=[_full(v.shape) for v in (x, g, wq, wk, wv, wf, bf_row, w4)],
        out_specs=[_full(s) for s in shapes],
        out_shape=[jax.ShapeDtypeStruct(s, F32) for s in shapes],
        compiler_params=_cparams("arbitrary"),
        name="proj_sample",
    )(x, g, wq, wk, wv, wf, bf_row, w4)


def _attn_prompt_kernel(qi_ref, ki_ref, q_ref, kta_ref, vt_ref, o_ref, m_ref, l_ref, acc_ref):
    step = pl.program_id(1)
    qi = qi_ref[step]
    ki = ki_ref[step]
    ta = q_ref.shape[2]

    @pl.when(ki == 0)
    def _():
        m_ref[...] = jnp.full_like(m_ref, NEG_INF)
        l_ref[...] = jnp.zeros_like(l_ref)
        acc_ref[...] = jnp.zeros_like(acc_ref)

    def update(on_diagonal):
        if on_diagonal:
            mask = (lax.broadcasted_iota(jnp.int32, (ta, ta), 1)
                    <= lax.broadcasted_iota(jnp.int32, (ta, ta), 0))
        for h in range(N_HEADS):
            s = _dot(q_ref[0, h], kta_ref[0, 2 * h * HEAD_DIM:2 * (h + 1) * HEAD_DIM, :])
            if on_diagonal:
                s = jnp.where(mask, s, NEG_INF)
            m_old = m_ref[h]
            m_new = jnp.maximum(m_old, jnp.max(s, axis=-1, keepdims=True))
            p = jnp.exp(s - jnp.concatenate([m_new] * (ta // LANES), axis=1))
            alpha = jnp.exp(m_old - m_new)
            l_ref[h] = alpha * l_ref[h] + jnp.sum(p, axis=-1, keepdims=True)
            pv = _dot_nt(p.astype(BF16), vt_ref[0, h * HEAD_DIM:(h + 1) * HEAD_DIM, :])
            acc_ref[h] = alpha[:, :HEAD_DIM] * acc_ref[h] + pv
            m_ref[h] = m_new

    @pl.when(ki < qi)
    def _():
        update(False)

    @pl.when(ki == qi)
    def _():
        update(True)
        for h in range(0, N_HEADS, 2):
            pair = [acc_ref[h + j] / l_ref[h + j][:, :HEAD_DIM] for j in range(2)]
            o_ref[0, :, h * HEAD_DIM:(h + 2) * HEAD_DIM] = jnp.concatenate(pair, axis=1).astype(o_ref.dtype)


def _attn_prompt(qa, kta, vtb):
    b, _, t, _ = qa.shape
    ta = min(ATTN_TILE, t)
    nq = t // ta
    pairs = [(i, j) for i in range(nq) for j in range(i + 1)]
    qi_tab = jnp.asarray([p[0] for p in pairs], jnp.int32)
    ki_tab = jnp.asarray([p[1] for p in pairs], jnp.int32)
    a = ATTN_WIDTH
    grid_spec = pltpu.PrefetchScalarGridSpec(
        num_scalar_prefetch=2,
        grid=(b, len(pairs)),
        in_specs=[
            pl.BlockSpec((1, N_HEADS, ta, 2 * HEAD_DIM), lambda i, s, qt, kt: (i, 0, qt[s], 0)),
            pl.BlockSpec((1, 2 * a, ta), lambda i, s, qt, kt: (i, 0, kt[s])),
            pl.BlockSpec((1, a, ta), lambda i, s, qt, kt: (i, 0, kt[s])),
        ],
        out_specs=pl.BlockSpec((1, ta, a), lambda i, s, qt, kt: (i, qt[s], 0)),
        scratch_shapes=[pltpu.VMEM((N_HEADS, ta, LANES), F32), pltpu.VMEM((N_HEADS, ta, LANES), F32),
                        pltpu.VMEM((N_HEADS, ta, HEAD_DIM), F32)],
    )
    return pl.pallas_call(
        _attn_prompt_kernel,
        grid_spec=grid_spec,
        out_shape=jax.ShapeDtypeStruct((b, t, a), BF16),
        compiler_params=_cparams("arbitrary", "arbitrary"),
        name="attn_prompt",
    )(qi_tab, ki_tab, qa, kta, vtb)


def _attn_decode_kernel(n_pg, pt_ref, q_ref, kn_ref, vn_ref, lfn_ref, w_ref, *rest):
    k_refs = rest[0:n_pg]
    v_refs = rest[n_pg:2 * n_pg]
    lt_refs = rest[2 * n_pg:3 * n_pg]
    o_ref = rest[3 * n_pg]
    m_ref, l_ref, c_ref, acc_ref = rest[3 * n_pg + 1:]
    c_step = pl.program_id(1)
    width = acc_ref.shape[1]
    own = (lax.broadcasted_iota(jnp.int32, (N_HEADS, width), 1) // HEAD_DIM
           == lax.broadcasted_iota(jnp.int32, (N_HEADS, width), 0))
    q_rows = jnp.where(own, q_ref[0] * (HEAD_DIM ** -0.5), 0.0)

    @pl.when(c_step == 0)
    def _():
        lane = lax.broadcasted_iota(jnp.int32, (N_HEADS, LANES), 1)
        s0 = jnp.sum(q_rows * kn_ref[0], axis=1, keepdims=True)
        m_ref[...] = jnp.broadcast_to(s0, m_ref.shape)
        l_ref[...] = jnp.where(lane == 0, 1.0, 0.0)
        acc_ref[...] = jnp.broadcast_to(vn_ref[0], acc_ref.shape)
        c_ref[...] = jnp.broadcast_to(lfn_ref[0], c_ref.shape)

    qb = q_rows.astype(BF16)
    carry = c_ref[...]
    s = []
    for i in range(n_pg):
        r = _exact_dot(lt_refs[i][0], w_ref[...])
        s.append(_dot(qb, k_refs[i][...].astype(BF16)) + (r[:, :LANES] + carry))
        carry = carry + r[:, LANES:]
    c_ref[...] = carry

    m_tile = s[0]
    for i in range(1, n_pg):
        m_tile = jnp.maximum(m_tile, s[i])
    m_old = m_ref[...]
    m_new = jnp.maximum(m_old, jnp.max(m_tile, axis=1, keepdims=True))
    alpha = jnp.exp(m_old - m_new)
    l_new = alpha * l_ref[...]
    pv = jnp.zeros(acc_ref.shape, F32)
    for i in range(n_pg):
        p = jnp.exp(s[i] - m_new)
        l_new = l_new + p
        pv = pv + _dot_nt(p.astype(BF16), v_refs[i][...].astype(BF16))
    m_ref[...] = m_new
    l_ref[...] = l_new
    acc_ref[...] = jnp.concatenate([alpha] * (width // LANES), axis=1) * acc_ref[...] + pv

    @pl.when(c_step == pl.num_programs(1) - 1)
    def _():
        den = jnp.sum(l_ref[...], axis=1, keepdims=True)
        o_ref[0] = jnp.sum(jnp.where(own, acc_ref[...] / den, 0.0), axis=0, keepdims=True)


def _attn_decode(layer, page_table, q, kn, vn, lfn_col, kt_pool, vt_pool, lt_pool, n_pool):
    db, n_pages = page_table.shape
    n_pg = min(DECODE_PAGES, n_pages)
    n_chunks = n_pages // n_pg
    page = kt_pool.shape[-1]
    base = layer * n_pool
    rowi = lax.broadcasted_iota(jnp.int32, (page, 2 * page), 0)
    coli = lax.broadcasted_iota(jnp.int32, (page, 2 * page), 1)
    w = ((rowi > coli) | (coli >= page)).astype(BF16)

    def page_of(slot):
        def index(b, c, pt):
            return base + pt[b * n_pages + (n_pages - 1 - (c * n_pg + slot))]
        return index

    per_seq = lambda r, c: pl.BlockSpec((1, r, c), lambda b, ch, pt: (b, 0, 0))

    def kv_specs():
        return [pl.BlockSpec((ATTN_WIDTH, page), (lambda b, ch, pt, f=page_of(i): (f(b, ch, pt), 0)))
                for i in range(n_pg)]

    lt_specs = [pl.BlockSpec((1, N_HEADS, page), (lambda b, ch, pt, f=page_of(i): (f(b, ch, pt), 0, 0)))
                for i in range(n_pg)]
    grid_spec = pltpu.PrefetchScalarGridSpec(
        num_scalar_prefetch=1,
        grid=(db, n_chunks),
        in_specs=[per_seq(1, ATTN_WIDTH), per_seq(1, ATTN_WIDTH), per_seq(1, ATTN_WIDTH), per_seq(N_HEADS, 1),
                  pl.BlockSpec(w.shape, lambda b, ch, pt: (0, 0))] + kv_specs() + kv_specs() + lt_specs,
        out_specs=per_seq(1, ATTN_WIDTH),
        scratch_shapes=[pltpu.VMEM((N_HEADS, LANES), F32), pltpu.VMEM((N_HEADS, LANES), F32),
                        pltpu.VMEM((N_HEADS, LANES), F32), pltpu.VMEM((N_HEADS, ATTN_WIDTH), F32)],
    )
    return pl.pallas_call(
        functools.partial(_attn_decode_kernel, n_pg),
        grid_spec=grid_spec,
        out_shape=jax.ShapeDtypeStruct((db, 1, ATTN_WIDTH), F32),
        compiler_params=_cparams("arbitrary", "arbitrary"),
        name="attn_decode",
    )(page_table.reshape(-1), q, kn, vn, lfn_col, w,
      *([kt_pool] * n_pg), *([vt_pool] * n_pg), *([lt_pool] * n_pg))


def _lru_gates(xc, wrg_ref, brg_ref, wig_ref, big_ref, lam_ref, store):
    xcb = xc.astype(BF16)
    blk = xc.shape[-1] // N_LRU_BLOCKS
    for n in range(N_LRU_BLOCKS):
        cs = slice(n * blk, (n + 1) * blk)
        r = jax.nn.sigmoid(_dot(xcb[:, cs], wrg_ref[n]) + brg_ref[:, cs])
        i = jax.nn.sigmoid(_dot(xcb[:, cs], wig_ref[n]) + big_ref[:, cs])
        log_a = -LRU_C * r * _softplus(-lam_ref[:, cs])
        a = jnp.exp(log_a)
        th = jnp.tanh(log_a)
        mult = jnp.sqrt(-2.0 * th / (1.0 - th))
        store(cs, a, mult * (i * xc[:, cs]))


def _lru_prompt_kernel(xr_ref, gr_ref, cp_ref, h0_ref, cw_ref, cb_ref, wrg_ref, brg_ref, wig_ref,
                       big_ref, lam_ref, y_ref, tail_ref, hl_ref, buf_ref, a_ref, b_ref, h_ref, hc_ref):
    t = pl.program_id(1)
    tt = xr_ref.shape[1]
    halo = SUBLANES

    @pl.when(t == 0)
    def _():
        buf_ref[0:halo, :] = cp_ref[0]
        hc_ref[...] = h0_ref[0]

    x = xr_ref[0]
    buf_ref[halo:, :] = x
    taps = [buf_ref[pl.ds(halo - (CONV_WIDTH - 1) + j, tt), :] * cw_ref[j:j + 1, :]
            for j in range(CONV_WIDTH - 1)]
    taps.append(x * cw_ref[CONV_WIDTH - 1:CONV_WIDTH, :])
    acc = taps[0]
    for term in taps[1:]:
        acc = acc + term
    xc = cb_ref[...] + acc

    def store(cs, a, b):
        a_ref[:, cs] = a
        b_ref[:, cs] = b

    _lru_gates(xc, wrg_ref, brg_ref, wig_ref, big_ref, lam_ref, store)

    grouped = (tt // SUBLANES, SUBLANES, a_ref.shape[1])
    a = a_ref[...].reshape(grouped)
    b = b_ref[...].reshape(grouped)
    in_group = lax.broadcasted_iota(jnp.int32, grouped, 1)
    shift = 1
    while shift < SUBLANES:
        a_prev = pltpu.roll(a, shift, axis=1)
        b_prev = pltpu.roll(b, shift, axis=1)
        take = in_group >= shift
        b = jnp.where(take, a * b_prev + b, b)
        a = jnp.where(take, a * a_prev, a)
        shift *= 2
    a_ref[...] = a.reshape(a_ref.shape)
    b_ref[...] = b.reshape(b_ref.shape)
    h_in = jnp.broadcast_to(hc_ref[...], (SUBLANES, a_ref.shape[1]))
    for g in range(tt // SUBLANES):
        rows = slice(g * SUBLANES, (g + 1) * SUBLANES)
        h_grp = a_ref[rows, :] * h_in + b_ref[rows, :]
        h_ref[rows, :] = h_grp
        h_in = jnp.broadcast_to(h_grp[SUBLANES - 1:, :], h_grp.shape)
    h_last = h_in[:1, :]
    hc_ref[...] = h_last
    hl_ref[0] = h_last
    y_ref[0] = (h_ref[...] * jax.nn.gelu(gr_ref[0])).astype(y_ref.dtype)
    tail = buf_ref[tt:tt + halo, :]
    tail_ref[0] = tail
    buf_ref[0:halo, :] = tail


def _lru_prompt(xr, gr, conv_prev8, h0, cw, cb, wrg, brg, wig, big, lam):
    b, t, d = xr.shape
    tt = min(LRU_TILE, t)
    seq = pl.BlockSpec((1, tt, d), lambda i, j: (i, j, 0))
    per_b = lambda r: pl.BlockSpec((1, r, d), lambda i, j: (i, 0, 0))
    return pl.pallas_call(
        _lru_prompt_kernel,
        grid=(b, t // tt),
        in_specs=[seq, seq, per_b(SUBLANES), per_b(1)] + [_full(v.shape) for v in (cw, cb, wrg, brg, wig, big, lam)],
        out_specs=[seq, per_b(SUBLANES), per_b(1)],
        out_shape=[jax.ShapeDtypeStruct((b, t, d), BF16), jax.ShapeDtypeStruct((b, SUBLANES, d), F32),
                   jax.ShapeDtypeStruct((b, 1, d), F32)],
        scratch_shapes=[pltpu.VMEM((tt + SUBLANES, d), F32), pltpu.VMEM((tt, d), F32),
                        pltpu.VMEM((tt, d), F32), pltpu.VMEM((tt, d), F32), pltpu.VMEM((1, d), F32)],
        compiler_params=_cparams("arbitrary", "arbitrary"),
        name="lru_prompt",
    )(xr, gr, conv_prev8, h0, cw, cb, wrg, brg, wig, big, lam)


def _lru_sample_kernel(xr_ref, gr_ref, c0_ref, c1_ref, c2_ref, h0_ref, cw_ref, cb_ref, wrg_ref, brg_ref,
                       wig_ref, big_ref, lam_ref, y_ref, hn_ref):
    x = xr_ref[...]
    acc = c0_ref[...] * cw_ref[0:1, :]
    acc = acc + c1_ref[...] * cw_ref[1:2, :]
    acc = acc + c2_ref[...] * cw_ref[2:3, :]
    acc = acc + x * cw_ref[3:4, :]
    xc = cb_ref[...] + acc

    def store(cs, a, b):
        hn_ref[:, cs] = a * h0_ref[:, cs] + b

    _lru_gates(xc, wrg_ref, brg_ref, wig_ref, big_ref, lam_ref, store)
    y_ref[...] = (hn_ref[...] * jax.nn.gelu(gr_ref[...])).astype(y_ref.dtype)


def _lru_sample(xr, gr, c0, c1, c2, h0, cw, cb, wrg, brg, wig, big, lam):
    m, d = xr.shape
    args = (xr, gr, c0, c1, c2, h0, cw, cb, wrg, brg, wig, big, lam)
    return pl.pallas_call(
        _lru_sample_kernel,
        grid=(1,),
        in_specs=[_full(v.shape) for v in args],
        out_specs=[_full((m, d)), _full((m, d))],
        out_shape=[jax.ShapeDtypeStruct((m, d), BF16), jax.ShapeDtypeStruct((m, d), F32)],
        compiler_params=_cparams("arbitrary"),
        name="lru_sample",
    )(*args)


def _merge_kernel(attn_ref, y_ref, ga_ref, gb_ref, x_ref, wa_ref, wb_ref, wo_ref, g_ref, x1_ref, xn_ref):
    merged = (jax.nn.sigmoid(ga_ref[...]) * _dot(attn_ref[...].astype(BF16), wa_ref[...])
              + jax.nn.sigmoid(gb_ref[...]) * _dot(y_ref[...], wb_ref[...]))
    x1 = x_ref[...] + _dot(merged.astype(BF16), wo_ref[...])
    x1_ref[...] = x1
    xn_ref[...] = _rmsnorm(x1, g_ref[...]).astype(xn_ref.dtype)


def _merge(attn, y, ga, gb, x, wa, wb, wo, g):
    m, d = x.shape
    tm = min(ROW_TILE, m)
    row = lambda w: pl.BlockSpec((tm, w), lambda i: (i, 0))
    return pl.pallas_call(
        _merge_kernel,
        grid=(m // tm,),
        in_specs=[row(attn.shape[1]), row(d), row(d), row(d), row(d)] + [_full(v.shape) for v in (wa, wb, wo, g)],
        out_specs=[row(d), row(d)],
        out_shape=[jax.ShapeDtypeStruct((m, d), F32), jax.ShapeDtypeStruct((m, d), BF16)],
        compiler_params=_cparams("arbitrary"),
        name="merge",
    )(attn, y, ga, gb, x, wa, wb, wo, g)


def _top2(logits):
    lane = lax.broadcasted_iota(jnp.int32, logits.shape, 1)
    logits = jnp.where(lane < N_EXPERTS, logits, -jnp.inf)
    m1 = jnp.max(logits, axis=-1, keepdims=True)
    i1 = jnp.min(jnp.where(logits == m1, lane, LANES), axis=-1, keepdims=True)
    rest = jnp.where(lane == i1, -jnp.inf, logits)
    m2 = jnp.max(rest, axis=-1, keepdims=True)
    i2 = jnp.min(jnp.where(rest == m2, lane, LANES), axis=-1, keepdims=True)
    e2 = jnp.exp(m2 - m1)
    den = 1.0 + e2
    return lane, i1, i2, 1.0 / den, e2 / den


def _to_row_tiles(dst_ref, x):
    rows = x.shape[0]
    for s in range(x.shape[1] // LANES):
        dst_ref[pl.ds(s, rows, stride=SUBLANES), :] = x[:, s * LANES:(s + 1) * LANES]


def _from_row_tiles(src_ref, rows, s):
    return src_ref[pl.ds(s, rows, stride=SUBLANES), :]


def _merge_route_kernel(attn_ref, y_ref, ga_ref, gb_ref, x_ref, wa_ref, wb_ref, wo_ref, g_ref, wr_ref, tri_ref,
                        x1_ref, xrt_ref, meta_ref, metat_ref, cnt_ref, carry_ref):
    i = pl.program_id(0)

    @pl.when(i == 0)
    def _():
        carry_ref[...] = jnp.zeros_like(carry_ref)

    merged = (jax.nn.sigmoid(ga_ref[...]) * _dot(attn_ref[...].astype(BF16), wa_ref[...])
              + jax.nn.sigmoid(gb_ref[...]) * _dot(y_ref[...], wb_ref[...]))
    x1 = x_ref[...] + _dot(merged.astype(BF16), wo_ref[...])
    x1_ref[...] = x1
    xn = _rmsnorm(x1, g_ref[...])
    _to_row_tiles(xrt_ref, xn)
    lane, i1, i2, w1, w2 = _top2(_dot(xn.astype(BF16), wr_ref[...]))
    oh1 = lane == i1
    oh2 = lane == i2
    tri = tri_ref[...]
    before1 = _dot(tri, oh1.astype(BF16))
    before2 = _dot(tri, oh2.astype(BF16))
    cnt1 = jnp.sum(oh1.astype(F32), axis=0, keepdims=True)
    cnt2 = jnp.sum(oh2.astype(F32), axis=0, keepdims=True)
    carry = carry_ref[0:1, :]
    rank1 = jnp.sum(jnp.where(oh1, carry + before1, 0.0), axis=-1, keepdims=True)
    rank2 = jnp.sum(jnp.where(oh2, carry + cnt1 + before2, 0.0), axis=-1, keepdims=True)
    total = carry + cnt1 + cnt2
    carry_ref[...] = jnp.broadcast_to(total, carry_ref.shape)
    cnt_ref[...] = jnp.broadcast_to(total, cnt_ref.shape)
    cols = (i1.astype(F32), i2.astype(F32), rank1, rank2, w1, w2)
    meta = jnp.zeros(meta_ref.shape, F32)
    for k, v in enumerate(cols):
        meta = jnp.where(lane == k, v, meta)
    meta_ref[...] = meta
    sel = (lax.broadcasted_iota(jnp.int32, (SUBLANES, LANES), 0)
           == lax.broadcasted_iota(jnp.int32, (SUBLANES, LANES), 1)).astype(BF16)
    meta_t = jnp.zeros(metat_ref.shape, F32)
    for piece in _split3(meta):
        meta_t = meta_t + _dot_nt(sel, piece.astype(BF16))
    metat_ref[...] = meta_t


def _merge_route(attn, y, ga, gb, x, wa, wb, wo, g, wr_pad):
    m, d = x.shape
    tm = min(ROW_TILE, m)
    tri = (lax.broadcasted_iota(jnp.int32, (tm, tm), 0)
           > lax.broadcasted_iota(jnp.int32, (tm, tm), 1)).astype(BF16)
    row = lambda w: pl.BlockSpec((tm, w), lambda i: (i, 0))
    return pl.pallas_call(
        _merge_route_kernel,
        grid=(m // tm,),
        in_specs=[row(attn.shape[1]), row(d), row(d), row(d), row(d)]
        + [_full(v.shape) for v in (wa, wb, wo, g, wr_pad, tri)],
        out_specs=[row(d), pl.BlockSpec((tm * SUBLANES, LANES), lambda i: (i, 0)), row(LANES),
                   pl.BlockSpec((SUBLANES, tm), lambda i: (0, i)),
                   pl.BlockSpec((SUBLANES, LANES), lambda i: (0, 0))],
        out_shape=[jax.ShapeDtypeStruct((m, d), F32), jax.ShapeDtypeStruct((m * SUBLANES, LANES), F32),
                   jax.ShapeDtypeStruct((m, LANES), F32), jax.ShapeDtypeStruct((SUBLANES, m), F32),
                   jax.ShapeDtypeStruct((SUBLANES, LANES), F32)],
        scratch_shapes=[pltpu.VMEM((SUBLANES, LANES), F32)],
        compiler_params=_cparams("arbitrary"),
        name="merge_route",
    )(attn, y, ga, gb, x, wa, wb, wo, g, wr_pad, tri)


def _row_copy(src_ref, src_row, dst_ref, dst_row, sem):
    return pltpu.make_async_copy(
        src_ref.at[pl.ds(pl.multiple_of(src_row * SUBLANES, SUBLANES), SUBLANES)],
        dst_ref.at[pl.ds(pl.multiple_of(dst_row * SUBLANES, SUBLANES), SUBLANES)], sem)


def _dispatch_kernel(d1_ref, d2_ref, x_ref, init_ref, xs_ref, sem):
    del init_ref
    tm = x_ref.shape[0] // SUBLANES
    base = pl.program_id(0) * tm

    def issue(r, carry):
        _row_copy(x_ref, r, xs_ref, d1_ref[base + r], sem).start(priority=0)
        _row_copy(x_ref, r, xs_ref, d2_ref[base + r], sem).start(priority=1)
        return carry

    lax.fori_loop(0, tm, issue, 0)

    def drain(r, carry):
        _row_copy(x_ref, r, xs_ref, d1_ref[base + r], sem).wait()
        _row_copy(x_ref, r, xs_ref, d2_ref[base + r], sem).wait()
        return carry

    lax.fori_loop(0, tm, drain, 0)


def _dispatch(dest1, dest2, x_rt, n_rows):
    m = dest1.shape[0]
    tm = min(ROW_TILE, m)
    grid_spec = pltpu.PrefetchScalarGridSpec(
        num_scalar_prefetch=2,
        grid=(m // tm,),
        in_specs=[pl.BlockSpec((tm * SUBLANES, LANES), lambda i, d1, d2: (i, 0)),
                  pl.BlockSpec(memory_space=pl.ANY)],
        out_specs=pl.BlockSpec(memory_space=pl.ANY),
        scratch_shapes=[pltpu.SemaphoreType.DMA],
    )
    return pl.pallas_call(
        _dispatch_kernel,
        grid_spec=grid_spec,
        out_shape=jax.ShapeDtypeStruct((n_rows * SUBLANES, LANES), F32),
        input_output_aliases={3: 0},
        compiler_params=_cparams("arbitrary"),
        name="moe_dispatch",
    )(dest1, dest2, x_rt, jnp.zeros((n_rows * SUBLANES, LANES), F32))


def _moe_ffn_kernel(te_ref, nu_ref, xs_ref, wg_ref, wu_ref, wd_ref, ys_ref, xb_ref, acc_ref):
    i = pl.program_id(0)
    f = pl.program_id(1)
    rows, d = xb_ref.shape

    @pl.when(i < nu_ref[0])
    def _():
        @pl.when(f == 0)
        def _():
            for s in range(d // LANES):
                xb_ref[:, s * LANES:(s + 1) * LANES] = _from_row_tiles(xs_ref, rows, s).astype(BF16)
            acc_ref[...] = jnp.zeros_like(acc_ref)

        xb = xb_ref[...]
        hid = jax.nn.silu(_dot(xb, wg_ref[0])) * _dot(xb, wu_ref[0])
        acc_ref[...] += _dot(hid.astype(BF16), wd_ref[0])

        @pl.when(f == pl.num_programs(1) - 1)
        def _():
            _to_row_tiles(ys_ref, acc_ref[...])

    @pl.when((i >= nu_ref[0]) & (f == pl.num_programs(1) - 1))
    def _():
        ys_ref[...] = jnp.zeros_like(ys_ref)


def _moe_ffn(tile_expert, n_used, xs, wg, wu, wd):
    n_e, d, ff = wg.shape
    tm = MOE_ROW_TILE
    tf = MOE_COL_TILE
    nf = ff // tf
    n_tiles = tile_expert.shape[0]

    def tile(i, f, te, nu):
        return (i, 0)

    def col(i, f, te, nu):
        return jnp.where(i < nu[0], f, nf - 1)

    grid_spec = pltpu.PrefetchScalarGridSpec(
        num_scalar_prefetch=2,
        grid=(n_tiles, nf),
        in_specs=[pl.BlockSpec((tm * SUBLANES, LANES), tile),
                  pl.BlockSpec((1, d, tf), lambda i, f, te, nu: (te[i], 0, col(i, f, te, nu))),
                  pl.BlockSpec((1, d, tf), lambda i, f, te, nu: (te[i], 0, col(i, f, te, nu))),
                  pl.BlockSpec((1, tf, d), lambda i, f, te, nu: (te[i], col(i, f, te, nu), 0))],
        out_specs=pl.BlockSpec((tm * SUBLANES, LANES), tile),
        scratch_shapes=[pltpu.VMEM((tm, d), BF16), pltpu.VMEM((tm, d), F32)],
    )
    return pl.pallas_call(
        _moe_ffn_kernel,
        grid_spec=grid_spec,
        out_shape=jax.ShapeDtypeStruct(xs.shape, F32),
        compiler_params=_cparams("arbitrary", "arbitrary"),
        name="moe_ffn",
    )(tile_expert, n_used, xs, wg, wu, wd)


def _combine_kernel(d1_ref, d2_ref, x1_ref, meta_ref, ys_ref, o_ref, g1_ref, g2_ref, sem):
    tm, d = x1_ref.shape
    base = pl.program_id(0) * tm

    def issue(r, carry):
        _row_copy(ys_ref, d1_ref[base + r], g1_ref, r, sem).start(priority=0)
        _row_copy(ys_ref, d2_ref[base + r], g2_ref, r, sem).start(priority=1)
        return carry

    lax.fori_loop(0, tm, issue, 0)

    def drain(r, carry):
        _row_copy(ys_ref, d1_ref[base + r], g1_ref, r, sem).wait()
        _row_copy(ys_ref, d2_ref[base + r], g2_ref, r, sem).wait()
        return carry

    lax.fori_loop(0, tm, drain, 0)
    w1 = meta_ref[:, 4:5]
    w2 = meta_ref[:, 5:6]
    for s in range(d // LANES):
        cs = slice(s * LANES, (s + 1) * LANES)
        o_ref[:, cs] = (x1_ref[:, cs] + w1 * _from_row_tiles(g1_ref, tm, s)
                        + w2 * _from_row_tiles(g2_ref, tm, s))


def _combine(dest1, dest2, x1, meta, ys):
    m, d = x1.shape
    tm = min(ROW_TILE, m)
    grid_spec = pltpu.PrefetchScalarGridSpec(
        num_scalar_prefetch=2,
        grid=(m // tm,),
        in_specs=[pl.BlockSpec((tm, d), lambda i, d1, d2: (i, 0)),
                  pl.BlockSpec((tm, LANES), lambda i, d1, d2: (i, 0)),
                  pl.BlockSpec(memory_space=pl.ANY)],
        out_specs=pl.BlockSpec((tm, d), lambda i, d1, d2: (i, 0)),
        scratch_shapes=[pltpu.VMEM((tm * SUBLANES, LANES), F32), pltpu.VMEM((tm * SUBLANES, LANES), F32),
                        pltpu.SemaphoreType.DMA],
    )
    return pl.pallas_call(
        _combine_kernel,
        grid_spec=grid_spec,
        out_shape=jax.ShapeDtypeStruct((m, d), F32),
        compiler_params=_cparams("arbitrary"),
        name="moe_combine",
    )(dest1, dest2, x1, meta, ys)


def _moe_sorted(x1, x_rt, meta, meta_t, counts, wg, wu, wd):
    m = x1.shape[0]
    tm = MOE_ROW_TILE
    n_tiles = (2 * m) // tm + N_EXPERTS
    cnt = counts[0, :N_EXPERTS].astype(jnp.int32)
    padded = ((cnt + tm - 1) // tm) * tm
    ends = jnp.cumsum(padded)
    starts = ends - padded
    experts = jnp.arange(N_EXPERTS, dtype=jnp.int32)[:, None]
    fields = meta_t.astype(jnp.int32)

    def dest(expert_row, rank_row):
        return jnp.sum(jnp.where(fields[expert_row][None, :] == experts, starts[:, None], 0), axis=0) + fields[rank_row]

    dest1 = dest(0, 2)
    dest2 = dest(1, 3)
    n_used = (ends[-1] // tm).reshape(1)
    tile_start = jnp.minimum(jnp.arange(n_tiles, dtype=jnp.int32), n_used - 1) * tm
    tile_expert = jnp.sum(ends[None, :] <= tile_start[:, None], axis=1).astype(jnp.int32)
    xs = _dispatch(dest1, dest2, x_rt, n_tiles * tm)
    ys = _moe_ffn(tile_expert, n_used.astype(jnp.int32), xs, wg, wu, wd)
    return _combine(dest1, dest2, x1, meta, ys)


def _ffn_kernel(xn_ref, x1_ref, wg_ref, wu_ref, wd_ref, o_ref, acc_ref):
    f = pl.program_id(1)

    @pl.when(f == 0)
    def _():
        acc_ref[...] = jnp.zeros_like(acc_ref)

    xn = xn_ref[...]
    hid = jax.nn.silu(_dot(xn, wg_ref[...])) * _dot(xn, wu_ref[...])
    acc_ref[...] += _dot(hid.astype(BF16), wd_ref[...])

    @pl.when(f == pl.num_programs(1) - 1)
    def _():
        o_ref[...] = x1_ref[...] + acc_ref[...]


def _ffn(xn, x1, wg, wu, wd):
    m, d = x1.shape
    ff = wg.shape[1]
    tm = min(FFN_ROW_TILE, m)
    tf = FFN_COL_TILE
    row = pl.BlockSpec((tm, d), lambda i, f: (i, 0))
    return pl.pallas_call(
        _ffn_kernel,
        grid=(m // tm, ff // tf),
        in_specs=[row, row, pl.BlockSpec((d, tf), lambda i, f: (0, f)), pl.BlockSpec((d, tf), lambda i, f: (0, f)),
                  pl.BlockSpec((tf, d), lambda i, f: (f, 0))],
        out_specs=row,
        out_shape=jax.ShapeDtypeStruct((m, d), F32),
        scratch_shapes=[pltpu.VMEM((tm, d), F32)],
        compiler_params=_cparams("arbitrary", "arbitrary"),
        name="ffn_dense",
    )(xn, x1, wg, wu, wd)


def _router_kernel(xn_ref, wr_ref, gates_ref):
    lane, i1, i2, w1, w2 = _top2(_dot(xn_ref[...], wr_ref[...]))
    gates_ref[...] = jnp.where(lane == i1, w1, 0.0) + jnp.where(lane == i2, w2, 0.0)


def _router(xn, wr_pad):
    m, d = xn.shape
    tm = min(ROW_TILE, m)
    return pl.pallas_call(
        _router_kernel,
        grid=(m // tm,),
        in_specs=[pl.BlockSpec((tm, d), lambda i: (i, 0)), _full(wr_pad.shape)],
        out_specs=pl.BlockSpec((tm, LANES), lambda i: (i, 0)),
        out_shape=jax.ShapeDtypeStruct((m, LANES), F32),
        compiler_params=_cparams("arbitrary"),
        name="router",
    )(xn, wr_pad)


def _moe_kernel(xn_ref, x1_ref, gates_ref, wg_ref, wu_ref, wd_ref, o_ref, acc_ref):
    e = pl.program_id(1)
    f = pl.program_id(2)

    @pl.when((e == 0) & (f == 0))
    def _():
        acc_ref[...] = jnp.zeros_like(acc_ref)

    xn = xn_ref[...]
    hid = jax.nn.silu(_dot(xn, wg_ref[0])) * _dot(xn, wu_ref[0])
    gates = gates_ref[...]
    lane = lax.broadcasted_iota(jnp.int32, gates.shape, 1)
    gate = jnp.sum(jnp.where(lane == e, gates, 0.0), axis=-1, keepdims=True)
    acc_ref[...] += gate * _dot(hid.astype(BF16), wd_ref[0])

    @pl.when((e == pl.num_programs(1) - 1) & (f == pl.num_programs(2) - 1))
    def _():
        o_ref[...] = x1_ref[...] + acc_ref[...]


def _moe(xn, x1, gates, wg, wu, wd):
    m, d = x1.shape
    n_e, _, ff = wg.shape
    tm = min(FFN_ROW_TILE, m)
    tf = MOE_COL_TILE
    row = lambda w: pl.BlockSpec((tm, w), lambda i, e, f: (i, 0))
    return pl.pallas_call(
        _moe_kernel,
        grid=(m // tm, n_e, ff // tf),
        in_specs=[row(d), row(d), row(LANES),
                  pl.BlockSpec((1, d, tf), lambda i, e, f: (e, 0, f)),
                  pl.BlockSpec((1, d, tf), lambda i, e, f: (e, 0, f)),
                  pl.BlockSpec((1, tf, d), lambda i, e, f: (e, f, 0))],
        out_specs=row(d),
        out_shape=jax.ShapeDtypeStruct((m, d), F32),
        scratch_shapes=[pltpu.VMEM((tm, d), F32)],
        compiler_params=_cparams("arbitrary", "arbitrary", "arbitrary"),
        name="moe_dense",
    )(xn, x1, gates, wg, wu, wd)


def _ple_kernel(final, x_ref, p_ref, g_ref, wgate_ref, wproj_ref, gf_ref, o_ref):
    x = x_ref[...]
    gate = jax.nn.sigmoid(_dot(_rmsnorm(x, g_ref[...]).astype(BF16), wgate_ref[...]))
    x3 = x + gate * _dot(p_ref[0].astype(BF16), wproj_ref[...])
    o_ref[...] = _rmsnorm(x3, gf_ref[...]) if final else x3


def _ple(x, p_all, layer, g, wgate, wproj, g_final, final):
    m, d = x.shape
    tm = min(ROW_TILE, m)
    row = lambda w: pl.BlockSpec((tm, w), lambda i: (i, 0))
    return pl.pallas_call(
        functools.partial(_ple_kernel, final),
        grid=(m // tm,),
        in_specs=[row(d), pl.BlockSpec((1, tm, p_all.shape[2]), lambda i: (layer, i, 0))]
        + [_full(v.shape) for v in (g, wgate, wproj, g_final)],
        out_specs=row(d),
        out_shape=jax.ShapeDtypeStruct((m, d), F32),
        compiler_params=_cparams("arbitrary"),
        name="ple",
    )(x, p_all, g, wgate, wproj, g_final)


def _row(v):
    return v.reshape(1, -1)


def kernel(x_prompt, x_sample, p_prompt, p_sample, cache_k, cache_v, cache_logf, state_conv, state_h, page_table, g_mix, w_in, b_f, conv_w, conv_b, w_rg, b_rg, w_ig, b_ig, lru_lambda, w_a_up, w_b_up, w_out, g_ffn, dense_wg, dense_wu, dense_wd, moe_router, moe_wg, moe_wu, moe_wd, g_ple, w_ple_gate, w_ple_proj, g_final):
    depth = g_mix.shape[0]
    b, t, d = x_prompt.shape
    db = x_sample.shape[0]
    n_pool, page = cache_k.shape[1], cache_k.shape[2]
    a = ATTN_WIDTH
    kt_pool = jnp.transpose(cache_k, (0, 1, 3, 4, 2)).reshape(depth * n_pool * ATTN_WIDTH, page)
    vt_pool = jnp.transpose(cache_v, (0, 1, 3, 4, 2)).reshape(depth * n_pool * ATTN_WIDTH, page)
    lt_pool = jnp.transpose(cache_logf, (0, 1, 3, 2)).reshape(depth * n_pool, N_HEADS, page)

    xp = x_prompt
    xs = x_sample.reshape(db, d)
    m = b * t
    pp_all = p_prompt.reshape(depth, m, -1)
    ps_all = p_sample.reshape(depth, db, -1)
    kt_all = jnp.zeros((depth, b, a, t), F32)
    vt_all = jnp.zeros((depth, b, a, t), F32)
    outs = {name: [] for name in ("lp", "cp", "hp", "ks", "vs", "ls", "cs", "hs")}
    for l in range(depth):
        wl = w_in[l]
        wq = wl[:, 0:a].astype(BF16)
        wk = wl[:, a:2 * a].astype(BF16)
        wv = wl[:, 2 * a:3 * a].astype(BF16)
        wf = jnp.pad(wl[:, 3 * a:3 * a + N_HEADS], ((0, 0), (0, LANES - N_HEADS))).astype(BF16)
        w4 = wl[:, 3 * a + N_HEADS:].astype(BF16)
        bf_row = jnp.pad(b_f[l], (0, LANES - N_HEADS)).reshape(1, LANES)
        g1 = _row(g_mix[l])
        lru_w = (conv_w[l], _row(conv_b[l]), w_rg[l].astype(BF16), _row(b_rg[l]), w_ig[l].astype(BF16),
                 _row(b_ig[l]), _row(lru_lambda[l]))
        wa = w_a_up[l].astype(BF16)
        wb = w_b_up[l].astype(BF16)
        wo = w_out[l].astype(BF16)
        g2 = _row(g_ffn[l])
        g3 = _row(g_ple[l])
        wgate = w_ple_gate[l].astype(BF16)
        wproj = w_ple_proj[l].astype(BF16)
        gf = _row(g_final)
        final = l == depth - 1
        mi = l // 2
        if l % 2 == 0:
            ffn_w = (dense_wg[mi].astype(BF16), dense_wu[mi].astype(BF16), dense_wd[mi].astype(BF16))
        else:
            wr_pad = jnp.pad(moe_router[mi], ((0, 0), (0, LANES - N_EXPERTS))).astype(BF16)
            ffn_w = (moe_wg[mi].astype(BF16), moe_wu[mi].astype(BF16), moe_wd[mi].astype(BF16))

        qa, kta, kt_all, vt_all, vtb, lft, xr, gr, ga, gb = _proj_prompt(
            xp, g1, wq, wk.T, wv.T, wf, bf_row, w4, l, kt_all, vt_all)
        attn = _attn_prompt(qa, kta, vtb)
        y, tail, hl = _lru_prompt(xr, gr, jnp.zeros((b, SUBLANES, d), F32), jnp.zeros((b, 1, d), F32), *lru_w)
        merge_in = (attn.reshape(m, a), y.reshape(m, d), ga.reshape(m, d), gb.reshape(m, d),
                    xp.reshape(m, d), wa, wb, wo, g2)
        if l % 2 == 0:
            x1, xn = _merge(*merge_in)
            x2 = _ffn(xn, x1, *ffn_w)
        else:
            x1, x_rt, meta, meta_t, counts = _merge_route(*merge_in, wr_pad)
            x2 = _moe_sorted(x1, x_rt, meta, meta_t, counts, *ffn_w)
        xp = _ple(x2, pp_all, l, g3, wgate, wproj, gf, final).reshape(b, t, d)
        outs["lp"].append(jnp.transpose(lft, (0, 2, 1)))
        outs["cp"].append(tail[:, SUBLANES - (CONV_WIDTH - 1):, :])
        outs["hp"].append(hl.reshape(b, d))

        qs, ks, vs, lfs, xrs, grs, gas, gbs = _proj_sample(xs, g1, wq, wk, wv, wf, bf_row, w4)
        lf8 = lfs[:, :N_HEADS]
        attn_s = _attn_decode(l, page_table, qs.reshape(db, 1, a), ks.reshape(db, 1, a), vs.reshape(db, 1, a),
                              lf8.reshape(db, N_HEADS, 1), kt_pool, vt_pool, lt_pool, n_pool).reshape(db, a)
        sc = state_conv[l]
        ys, hn = _lru_sample(xrs, grs, sc[:, 0], sc[:, 1], sc[:, 2], state_h[l], *lru_w)
        x1s, xns = _merge(attn_s, ys, gas, gbs, xs, wa, wb, wo, g2)
        if l % 2 == 0:
            x2s = _ffn(xns, x1s, *ffn_w)
        else:
            x2s = _moe(xns, x1s, _router(xns, wr_pad), *ffn_w)
        xs = _ple(x2s, ps_all, l, g3, wgate, wproj, gf, final)
        outs["ks"].append(ks.reshape(db, 1, N_HEADS, HEAD_DIM))
        outs["vs"].append(vs.reshape(db, 1, N_HEADS, HEAD_DIM))
        outs["ls"].append(lf8.reshape(db, 1, N_HEADS))
        outs["cs"].append(jnp.stack([sc[:, 1], sc[:, 2], xrs], axis=1))
        outs["hs"].append(hn)

    st = lambda name: jnp.stack(outs[name])
    heads_last = lambda v: jnp.transpose(v.reshape(depth, b, N_HEADS, HEAD_DIM, t), (0, 1, 4, 2, 3))
    return (xp, xs.reshape(db, 1, d), heads_last(kt_all), heads_last(vt_all), st("lp"), st("cp"), st("hp"),
            st("ks"), st("vs"), st("ls"), st("cs"), st("hs"))
```

```python
import functools

import jax
import jax.numpy as jnp
from jax import lax
from jax.experimental import pallas as pl
from jax.experimental.pallas import tpu as pltpu
from jax.experimental.pallas import tpu_sc as plsc

BF16 = jnp.bfloat16
F32 = jnp.float32

EPS = 1e-6
NEG_INF = -1e30
LRU_C = 8.0
N_HEADS = 8
HEAD_DIM = 64
ATTN_WIDTH = N_HEADS * HEAD_DIM
N_LRU_BLOCKS = 8
CONV_WIDTH = 4
N_EXPERTS = 8

LANES = 128
SUBLANES = 8
VMEM_LIMIT_BYTES = 56 * 1024 * 1024

PROJ_TILE = 512
ATTN_TILE = 512
LRU_TILE = 256
ROW_TILE = 512
FFN_ROW_TILE = 1024
FFN_COL_TILE = 512
MOE_ROW_TILE = 512
MOE_COL_TILE = 1792
DECODE_PAGES = 16


def _cparams(*sem):
    return pltpu.CompilerParams(dimension_semantics=sem, vmem_limit_bytes=VMEM_LIMIT_BYTES)


def _full(shape):
    zeros = (0,) * len(shape)
    return pl.BlockSpec(shape, lambda *_: zeros, pipeline_mode=pl.Buffered(1))


def _rmsnorm(x, g):
    return x * lax.rsqrt(jnp.mean(x * x, axis=-1, keepdims=True) + EPS) * g


def _log_sigmoid(z):
    return jnp.minimum(z, 0.0) - jnp.log1p(jnp.exp(-jnp.abs(z)))


def _softplus(z):
    return jnp.maximum(z, 0.0) + jnp.log1p(jnp.exp(-jnp.abs(z)))


def _dot(a, b):
    return jnp.dot(a, b, preferred_element_type=F32)


def _dot_nt(a, b):
    return lax.dot_general(a, b, (((1,), (1,)), ((), ())), preferred_element_type=F32)


def _exact_dot(x, w01):
    hi = x.astype(BF16)
    r1 = x - hi.astype(F32)
    mid = r1.astype(BF16)
    lo = (r1 - mid.astype(F32)).astype(BF16)
    return _dot(hi, w01) + _dot(mid, w01) + _dot(lo, w01)


def _split3(x):
    p1 = x.astype(BF16).astype(F32)
    r1 = x - p1
    p2 = r1.astype(BF16).astype(F32)
    p3 = (r1 - p2).astype(BF16).astype(F32)
    return p1, p2, p3


def _proj_prompt_kernel(x_ref, g_ref, wq_ref, wkt_ref, wvt_ref, wf_ref, wft_ref, bfr_ref, bfc_ref, w4_ref,
                        triu_ref, tril_ref, kt_all_ref, vt_all_ref,
                        qa_ref, kta_ref, kt_ref, vt_ref, vtb_ref, lft_ref,
                        xr_ref, gr_ref, ga_ref, gb_ref, carry_t_ref, carry_c_ref):
    del kt_all_ref, vt_all_ref
    t = pl.program_id(1)

    @pl.when(t == 0)
    def _():
        carry_t_ref[...] = jnp.zeros_like(carry_t_ref)
        carry_c_ref[...] = jnp.zeros_like(carry_c_ref)

    tm = x_ref.shape[1]
    xn = _rmsnorm(x_ref[0], g_ref[...]).astype(BF16)
    lft = _log_sigmoid(_dot_nt(wft_ref[...], xn) + bfc_ref[...])[:N_HEADS]
    lft_ref[0] = lft
    lfc = _log_sigmoid(_dot(xn, wf_ref[...]) + bfr_ref[...])
    ct = _exact_dot(lft, triu_ref[...]) + carry_t_ref[:, :1]
    carry_t_ref[...] = jnp.broadcast_to(ct[:, -1:], carry_t_ref.shape)
    tril = tril_ref[...]
    cc = carry_c_ref[0:1, :]
    for piece in _split3(lfc):
        cc = cc + _dot(tril, piece.astype(BF16))
    carry_c_ref[...] = jnp.broadcast_to(cc[-1:, :], carry_c_ref.shape)

    q = _dot(xn, wq_ref[...]) * (HEAD_DIM ** -0.5)
    kt = _dot_nt(wkt_ref[...], xn)
    kt_ref[0, 0] = kt
    lane = lax.broadcasted_iota(jnp.int32, (tm, HEAD_DIM), 1)
    sub = lax.broadcasted_iota(jnp.int32, (HEAD_DIM, tm), 0)
    for h in range(N_HEADS):
        hs = slice(h * HEAD_DIM, (h + 1) * HEAD_DIM)
        q1, q2, q3 = _split3(cc[:, h:h + 1])
        q_extra = jnp.where(lane == 0, q1, jnp.where(lane == 1, q2, jnp.where(lane == 2, q3,
                            jnp.where(lane < 6, 1.0, 0.0))))
        qa_ref[0, h] = jnp.concatenate([q[:, hs], q_extra], axis=1).astype(BF16)
        k1, k2, k3 = _split3(ct[h:h + 1, :])
        k_extra = jnp.where(sub < 3, 1.0, jnp.where(sub == 3, -k1, jnp.where(sub == 4, -k2,
                            jnp.where(sub == 5, -k3, 0.0))))
        kta_ref[0, 2 * h * HEAD_DIM:2 * (h + 1) * HEAD_DIM, :] = jnp.concatenate(
            [kt[hs, :], k_extra], axis=0).astype(BF16)
    vt = _dot_nt(wvt_ref[...], xn)
    vt_ref[0, 0] = vt
    vtb_ref[0] = vt.astype(BF16)
    d = xr_ref.shape[-1]
    xr_ref[0] = _dot(xn, w4_ref[:, 0 * d:1 * d])
    gr_ref[0] = _dot(xn, w4_ref[:, 1 * d:2 * d])
    ga_ref[0] = _dot(xn, w4_ref[:, 2 * d:3 * d])
    gb_ref[0] = _dot(xn, w4_ref[:, 3 * d:4 * d])


def _proj_prompt(x, g, wq, wkt, wvt, wf, bf_row, w4, layer, kt_all, vt_all):
    b, t, d = x.shape
    tm = min(PROJ_TILE, t)
    nt = t // tm
    ri = lax.broadcasted_iota(jnp.int32, (tm, tm), 0)
    ci = lax.broadcasted_iota(jnp.int32, (tm, tm), 1)
    triu = (ri <= ci).astype(BF16)
    tril = (ri >= ci).astype(BF16)
    a = ATTN_WIDTH
    seq = lambda w: pl.BlockSpec((1, tm, w), lambda i, j: (i, j, 0))
    tr = lambda r: pl.BlockSpec((1, r, tm), lambda i, j: (i, 0, j))
    out_shape = [
        jax.ShapeDtypeStruct((b, N_HEADS, t, 2 * HEAD_DIM), BF16),
        jax.ShapeDtypeStruct((b, 2 * a, t), BF16),
        jax.ShapeDtypeStruct(kt_all.shape, F32), jax.ShapeDtypeStruct(vt_all.shape, F32),
        jax.ShapeDtypeStruct((b, a, t), BF16),
        jax.ShapeDtypeStruct((b, N_HEADS, t), F32),
    ] + [jax.ShapeDtypeStruct((b, t, d), F32)] * 4
    stacked = pl.BlockSpec((1, 1, a, tm), lambda i, j: (layer, i, 0, j))
    out_specs = [
        pl.BlockSpec((1, N_HEADS, tm, 2 * HEAD_DIM), lambda i, j: (i, 0, j, 0)),
        tr(2 * a), stacked, stacked, tr(a), tr(N_HEADS),
        seq(d), seq(d), seq(d), seq(d),
    ]
    args = (x, g, wq, wkt, wvt, wf, wf.T, bf_row, bf_row.reshape(LANES, 1), w4, triu, tril)
    return pl.pallas_call(
        _proj_prompt_kernel,
        grid=(b, nt),
        in_specs=[seq(d)] + [_full(v.shape) for v in args[1:]] + [pl.BlockSpec(memory_space=pl.ANY)] * 2,
        out_specs=out_specs,
        out_shape=out_shape,
        input_output_aliases={len(args): 2, len(args) + 1: 3},
        scratch_shapes=[pltpu.VMEM((N_HEADS, LANES), F32), pltpu.VMEM((N_HEADS, LANES), F32)],
        compiler_params=_cparams("arbitrary", "arbitrary"),
        name="proj_prompt",
    )(*args, kt_all, vt_all)


def _proj_sample_kernel(x_ref, g_ref, wq_ref, wk_ref, wv_ref, wf_ref, bf_ref, w4_ref,
                        q_ref, k_ref, v_ref, lf_ref, xr_ref, gr_ref, ga_ref, gb_ref):
    xn = _rmsnorm(x_ref[...], g_ref[...]).astype(BF16)
    q_ref[...] = _dot(xn, wq_ref[...])
    k_ref[...] = _dot(xn, wk_ref[...])
    v_ref[...] = _dot(xn, wv_ref[...])
    lf_ref[...] = _log_sigmoid(_dot(xn, wf_ref[...]) + bf_ref[...])
    d = xr_ref.shape[-1]
    xr_ref[...] = _dot(xn, w4_ref[:, 0 * d:1 * d])
    gr_ref[...] = _dot(xn, w4_ref[:, 1 * d:2 * d])
    ga_ref[...] = _dot(xn, w4_ref[:, 2 * d:3 * d])
    gb_ref[...] = _dot(xn, w4_ref[:, 3 * d:4 * d])


def _proj_sample(x, g, wq, wk, wv, wf, bf_row, w4):
    m, d = x.shape
    a = ATTN_WIDTH
    shapes = [(m, a), (m, a), (m, a), (m, LANES), (m, d), (m, d), (m, d), (m, d)]
    return pl.pallas_call(
        _proj_sample_kernel,
        grid=(1,),
        in_specs=[_full(v.shape) for v in (x, g, wq, wk, wv, wf, bf_row, w4)],
        out_specs=[_full(s) for s in shapes],
        out_shape=[jax.ShapeDtypeStruct(s, F32) for s in shapes],
        compiler_params=_cparams("arbitrary"),
        name="proj_sample",
    )(x, g, wq, wk, wv, wf, bf_row, w4)


def _attn_prompt_kernel(qi_ref, ki_ref, q_ref, kta_ref, vt_ref, o_ref, m_ref, l_ref, acc_ref):
    step = pl.program_id(1)
    qi = qi_ref[step]
    ki = ki_ref[step]
    ta = q_ref.shape[2]

    @pl.when(ki == 0)
    def _():
        m_ref[...] = jnp.full_like(m_ref, NEG_INF)
        l_ref[...] = jnp.zeros_like(l_ref)
        acc_ref[...] = jnp.zeros_like(acc_ref)

    def update(on_diagonal):
        if on_diagonal:
            mask = (lax.broadcasted_iota(jnp.int32, (ta, ta), 1)
                    <= lax.broadcasted_iota(jnp.int32, (ta, ta), 0))
        for h in range(N_HEADS):
            s = _dot(q_ref[0, h], kta_ref[0, 2 * h * HEAD_DIM:2 * (h + 1) * HEAD_DIM, :])
            if on_diagonal:
                s = jnp.where(mask, s, NEG_INF)
            m_old = m_ref[h]
            m_new = jnp.maximum(m_old, jnp.max(s, axis=-1, keepdims=True))
            p = jnp.exp(s - jnp.concatenate([m_new] * (ta // LANES), axis=1))
            alpha = jnp.exp(m_old - m_new)
            l_ref[h] = alpha * l_ref[h] + jnp.sum(p, axis=-1, keepdims=True)
            pv = _dot_nt(p.astype(BF16), vt_ref[0, h * HEAD_DIM:(h + 1) * HEAD_DIM, :])
            acc_ref[h] = alpha[:, :HEAD_DIM] * acc_ref[h] + pv
            m_ref[h] = m_new

    @pl.when(ki < qi)
    def _():
        update(False)

    @pl.when(ki == qi)
    def _():
        update(True)
        for h in range(0, N_HEADS, 2):
            pair = [acc_ref[h + j] / l_ref[h + j][:, :HEAD_DIM] for j in range(2)]
            o_ref[0, :, h * HEAD_DIM:(h + 2) * HEAD_DIM] = jnp.concatenate(pair, axis=1).astype(o_ref.dtype)


def _attn_prompt(qa, kta, vtb):
    b, _, t, _ = qa.shape
    ta = min(ATTN_TILE, t)
    nq = t // ta
    pairs = [(i, j) for i in range(nq) for j in range(i + 1)]
    qi_tab = jnp.asarray([p[0] for p in pairs], jnp.int32)
    ki_tab = jnp.asarray([p[1] for p in pairs], jnp.int32)
    a = ATTN_WIDTH
    grid_spec = pltpu.PrefetchScalarGridSpec(
        num_scalar_prefetch=2,
        grid=(b, len(pairs)),
        in_specs=[
            pl.BlockSpec((1, N_HEADS, ta, 2 * HEAD_DIM), lambda i, s, qt, kt: (i, 0, qt[s], 0)),
            pl.BlockSpec((1, 2 * a, ta), lambda i, s, qt, kt: (i, 0, kt[s])),
            pl.BlockSpec((1, a, ta), lambda i, s, qt, kt: (i, 0, kt[s])),
        ],
        out_specs=pl.BlockSpec((1, ta, a), lambda i, s, qt, kt: (i, qt[s], 0)),
        scratch_shapes=[pltpu.VMEM((N_HEADS, ta, LANES), F32), pltpu.VMEM((N_HEADS, ta, LANES), F32),
                        pltpu.VMEM((N_HEADS, ta, HEAD_DIM), F32)],
    )
    return pl.pallas_call(
        _attn_prompt_kernel,
        grid_spec=grid_spec,
        out_shape=jax.ShapeDtypeStruct((b, t, a), BF16),
        compiler_params=_cparams("arbitrary", "arbitrary"),
        name="attn_prompt",
    )(qi_tab, ki_tab, qa, kta, vtb)


def _attn_decode_kernel(n_pg, pt_ref, q_ref, kn_ref, vn_ref, lfn_ref, w_ref, *rest):
    k_refs = rest[0:n_pg]
    v_refs = rest[n_pg:2 * n_pg]
    lt_refs = rest[2 * n_pg:3 * n_pg]
    o_ref = rest[3 * n_pg]
    m_ref, l_ref, c_ref, acc_ref = rest[3 * n_pg + 1:]
    c_step = pl.program_id(1)
    width = acc_ref.shape[1]
    own = (lax.broadcasted_iota(jnp.int32, (N_HEADS, width), 1) // HEAD_DIM
           == lax.broadcasted_iota(jnp.int32, (N_HEADS, width), 0))
    q_rows = jnp.where(own, q_ref[0] * (HEAD_DIM ** -0.5), 0.0)

    @pl.when(c_step == 0)
    def _():
        lane = lax.broadcasted_iota(jnp.int32, (N_HEADS, LANES), 1)
        s0 = jnp.sum(q_rows * kn_ref[0], axis=1, keepdims=True)
        m_ref[...] = jnp.broadcast_to(s0, m_ref.shape)
        l_ref[...] = jnp.where(lane == 0, 1.0, 0.0)
        acc_ref[...] = jnp.broadcast_to(vn_ref[0], acc_ref.shape)
        c_ref[...] = jnp.broadcast_to(lfn_ref[0], c_ref.shape)

    qb = q_rows.astype(BF16)
    carry = c_ref[...]
    s = []
    for i in range(n_pg):
        r = _exact_dot(lt_refs[i][0], w_ref[...])
        s.append(_dot(qb, k_refs[i][...].astype(BF16)) + (r[:, :LANES] + carry))
        carry = carry + r[:, LANES:]
    c_ref[...] = carry

    m_tile = s[0]
    for i in range(1, n_pg):
        m_tile = jnp.maximum(m_tile, s[i])
    m_old = m_ref[...]
    m_new = jnp.maximum(m_old, jnp.max(m_tile, axis=1, keepdims=True))
    alpha = jnp.exp(m_old - m_new)
    l_new = alpha * l_ref[...]
    pv = jnp.zeros(acc_ref.shape, F32)
    for i in range(n_pg):
        p = jnp.exp(s[i] - m_new)
        l_new = l_new + p
        pv = pv + _dot_nt(p.astype(BF16), v_refs[i][...].astype(BF16))
    m_ref[...] = m_new
    l_ref[...] = l_new
    acc_ref[...] = jnp.concatenate([alpha] * (width // LANES), axis=1) * acc_ref[...] + pv

    @pl.when(c_step == pl.num_programs(1) - 1)
    def _():
        den = jnp.sum(l_ref[...], axis=1, keepdims=True)
        o_ref[0] = jnp.sum(jnp.where(own, acc_ref[...] / den, 0.0), axis=0, keepdims=True)


def _attn_decode(layer, page_table, q, kn, vn, lfn_col, kt_pool, vt_pool, lt_pool, n_pool):
    db, n_pages = page_table.shape
    n_pg = min(DECODE_PAGES, n_pages)
    n_chunks = n_pages // n_pg
    page = kt_pool.shape[-1]
    base = layer * n_pool
    rowi = lax.broadcasted_iota(jnp.int32, (page, 2 * page), 0)
    coli = lax.broadcasted_iota(jnp.int32, (page, 2 * page), 1)
    w = ((rowi > coli) | (coli >= page)).astype(BF16)

    def page_of(slot):
        def index(b, c, pt):
            return base + pt[b * n_pages + (n_pages - 1 - (c * n_pg + slot))]
        return index

    per_seq = lambda r, c: pl.BlockSpec((1, r, c), lambda b, ch, pt: (b, 0, 0))

    def kv_specs():
        return [pl.BlockSpec((ATTN_WIDTH, page), (lambda b, ch, pt, f=page_of(i): (f(b, ch, pt), 0)))
                for i in range(n_pg)]

    lt_specs = [pl.BlockSpec((1, N_HEADS, page), (lambda b, ch, pt, f=page_of(i): (f(b, ch, pt), 0, 0)))
                for i in range(n_pg)]
    grid_spec = pltpu.PrefetchScalarGridSpec(
        num_scalar_prefetch=1,
        grid=(db, n_chunks),
        in_specs=[per_seq(1, ATTN_WIDTH), per_seq(1, ATTN_WIDTH), per_seq(1, ATTN_WIDTH), per_seq(N_HEADS, 1),
                  pl.BlockSpec(w.shape, lambda b, ch, pt: (0, 0))] + kv_specs() + kv_specs() + lt_specs,
        out_specs=per_seq(1, ATTN_WIDTH),
        scratch_shapes=[pltpu.VMEM((N_HEADS, LANES), F32), pltpu.VMEM((N_HEADS, LANES), F32),
                        pltpu.VMEM((N_HEADS, LANES), F32), pltpu.VMEM((N_HEADS, ATTN_WIDTH), F32)],
    )
    return pl.pallas_call(
        functools.partial(_attn_decode_kernel, n_pg),
        grid_spec=grid_spec,
        out_shape=jax.ShapeDtypeStruct((db, 1, ATTN_WIDTH), F32),
        compiler_params=_cparams("arbitrary", "arbitrary"),
        name="attn_decode",
    )(page_table.reshape(-1), q, kn, vn, lfn_col, w,
      *([kt_pool] * n_pg), *([vt_pool] * n_pg), *([lt_pool] * n_pg))


def _lru_gates(xc, wrg_ref, brg_ref, wig_ref, big_ref, lam_ref, store):
    xcb = xc.astype(BF16)
    blk = xc.shape[-1] // N_LRU_BLOCKS
    for n in range(N_LRU_BLOCKS):
        cs = slice(n * blk, (n + 1) * blk)
        r = jax.nn.sigmoid(_dot(xcb[:, cs], wrg_ref[n]) + brg_ref[:, cs])
        i = jax.nn.sigmoid(_dot(xcb[:, cs], wig_ref[n]) + big_ref[:, cs])
        log_a = -LRU_C * r * _softplus(-lam_ref[:, cs])
        a = jnp.exp(log_a)
        th = jnp.tanh(log_a)
        mult = jnp.sqrt(-2.0 * th / (1.0 - th))
        store(cs, a, mult * (i * xc[:, cs]))


def _lru_prompt_kernel(xr_ref, gr_ref, cp_ref, h0_ref, cw_ref, cb_ref, wrg_ref, brg_ref, wig_ref,
                       big_ref, lam_ref, y_ref, tail_ref, hl_ref, buf_ref, a_ref, b_ref, h_ref, hc_ref):
    t = pl.program_id(1)
    tt = xr_ref.shape[1]
    halo = SUBLANES

    @pl.when(t == 0)
    def _():
        buf_ref[0:halo, :] = cp_ref[0]
        hc_ref[...] = h0_ref[0]

    x = xr_ref[0]
    buf_ref[halo:, :] = x
    taps = [buf_ref[pl.ds(halo - (CONV_WIDTH - 1) + j, tt), :] * cw_ref[j:j + 1, :]
            for j in range(CONV_WIDTH - 1)]
    taps.append(x * cw_ref[CONV_WIDTH - 1:CONV_WIDTH, :])
    acc = taps[0]
    for term in taps[1:]:
        acc = acc + term
    xc = cb_ref[...] + acc

    def store(cs, a, b):
        a_ref[:, cs] = a
        b_ref[:, cs] = b

    _lru_gates(xc, wrg_ref, brg_ref, wig_ref, big_ref, lam_ref, store)

    grouped = (tt // SUBLANES, SUBLANES, a_ref.shape[1])
    a = a_ref[...].reshape(grouped)
    b = b_ref[...].reshape(grouped)
    in_group = lax.broadcasted_iota(jnp.int32, grouped, 1)
    shift = 1
    while shift < SUBLANES:
        a_prev = pltpu.roll(a, shift, axis=1)
        b_prev = pltpu.roll(b, shift, axis=1)
        take = in_group >= shift
        b = jnp.where(take, a * b_prev + b, b)
        a = jnp.where(take, a * a_prev, a)
        shift *= 2
    a_ref[...] = a.reshape(a_ref.shape)
    b_ref[...] = b.reshape(b_ref.shape)
    h_in = jnp.broadcast_to(hc_ref[...], (SUBLANES, a_ref.shape[1]))
    for g in range(tt // SUBLANES):
        rows = slice(g * SUBLANES, (g + 1) * SUBLANES)
        h_grp = a_ref[rows, :] * h_in + b_ref[rows, :]
        h_ref[rows, :] = h_grp
        h_in = jnp.broadcast_to(h_grp[SUBLANES - 1:, :], h_grp.shape)
    h_last = h_in[:1, :]
    hc_ref[...] = h_last
    hl_ref[0] = h_last
    y_ref[0] = (h_ref[...] * jax.nn.gelu(gr_ref[0])).astype(y_ref.dtype)
    tail = buf_ref[tt:tt + halo, :]
    tail_ref[0] = tail
    buf_ref[0:halo, :] = tail


def _lru_prompt(xr, gr, conv_prev8, h0, cw, cb, wrg, brg, wig, big, lam):
    b, t, d = xr.shape
    tt = min(LRU_TILE, t)
    seq = pl.BlockSpec((1, tt, d), lambda i, j: (i, j, 0))
    per_b = lambda r: pl.BlockSpec((1, r, d), lambda i, j: (i, 0, 0))
    return pl.pallas_call(
        _lru_prompt_kernel,
        grid=(b, t // tt),
        in_specs=[seq, seq, per_b(SUBLANES), per_b(1)] + [_full(v.shape) for v in (cw, cb, wrg, brg, wig, big, lam)],
        out_specs=[seq, per_b(SUBLANES), per_b(1)],
        out_shape=[jax.ShapeDtypeStruct((b, t, d), BF16), jax.ShapeDtypeStruct((b, SUBLANES, d), F32),
                   jax.ShapeDtypeStruct((b, 1, d), F32)],
        scratch_shapes=[pltpu.VMEM((tt + SUBLANES, d), F32), pltpu.VMEM((tt, d), F32),
                        pltpu.VMEM((tt, d), F32), pltpu.VMEM((tt, d), F32), pltpu.VMEM((1, d), F32)],
        compiler_params=_cparams("arbitrary", "arbitrary"),
        name="lru_prompt",
    )(xr, gr, conv_prev8, h0, cw, cb, wrg, brg, wig, big, lam)


def _lru_sample_kernel(xr_ref, gr_ref, c0_ref, c1_ref, c2_ref, h0_ref, cw_ref, cb_ref, wrg_ref, brg_ref,
                       wig_ref, big_ref, lam_ref, y_ref, hn_ref):
    x = xr_ref[...]
    acc = c0_ref[...] * cw_ref[0:1, :]
    acc = acc + c1_ref[...] * cw_ref[1:2, :]
    acc = acc + c2_ref[...] * cw_ref[2:3, :]
    acc = acc + x * cw_ref[3:4, :]
    xc = cb_ref[...] + acc

    def store(cs, a, b):
        hn_ref[:, cs] = a * h0_ref[:, cs] + b

    _lru_gates(xc, wrg_ref, brg_ref, wig_ref, big_ref, lam_ref, store)
    y_ref[...] = (hn_ref[...] * jax.nn.gelu(gr_ref[...])).astype(y_ref.dtype)


def _lru_sample(xr, gr, c0, c1, c2, h0, cw, cb, wrg, brg, wig, big, lam):
    m, d = xr.shape
    args = (xr, gr, c0, c1, c2, h0, cw, cb, wrg, brg, wig, big, lam)
    return pl.pallas_call(
        _lru_sample_kernel,
        grid=(1,),
        in_specs=[_full(v.shape) for v in args],
        out_specs=[_full((m, d)), _full((m, d))],
        out_shape=[jax.ShapeDtypeStruct((m, d), BF16), jax.ShapeDtypeStruct((m, d), F32)],
        compiler_params=_cparams("arbitrary"),
        name="lru_sample",
    )(*args)


def _merge_kernel(attn_ref, y_ref, ga_ref, gb_ref, x_ref, wa_ref, wb_ref, wo_ref, g_ref, x1_ref, xn_ref):
    merged = (jax.nn.sigmoid(ga_ref[...]) * _dot(attn_ref[...].astype(BF16), wa_ref[...])
              + jax.nn.sigmoid(gb_ref[...]) * _dot(y_ref[...], wb_ref[...]))
    x1 = x_ref[...] + _dot(merged.astype(BF16), wo_ref[...])
    x1_ref[...] = x1
    xn_ref[...] = _rmsnorm(x1, g_ref[...]).astype(xn_ref.dtype)


def _merge(attn, y, ga, gb, x, wa, wb, wo, g):
    m, d = x.shape
    tm = min(ROW_TILE, m)
    row = lambda w: pl.BlockSpec((tm, w), lambda i: (i, 0))
    return pl.pallas_call(
        _merge_kernel,
        grid=(m // tm,),
        in_specs=[row(attn.shape[1]), row(d), row(d), row(d), row(d)] + [_full(v.shape) for v in (wa, wb, wo, g)],
        out_specs=[row(d), row(d)],
        out_shape=[jax.ShapeDtypeStruct((m, d), F32), jax.ShapeDtypeStruct((m, d), BF16)],
        compiler_params=_cparams("arbitrary"),
        name="merge",
    )(attn, y, ga, gb, x, wa, wb, wo, g)


def _top2(logits):
    lane = lax.broadcasted_iota(jnp.int32, logits.shape, 1)
    logits = jnp.where(lane < N_EXPERTS, logits, -jnp.inf)
    m1 = jnp.max(logits, axis=-1, keepdims=True)
    i1 = jnp.min(jnp.where(logits == m1, lane, LANES), axis=-1, keepdims=True)
    rest = jnp.where(lane == i1, -jnp.inf, logits)
    m2 = jnp.max(rest, axis=-1, keepdims=True)
    i2 = jnp.min(jnp.where(rest == m2, lane, LANES), axis=-1, keepdims=True)
    e2 = jnp.exp(m2 - m1)
    den = 1.0 + e2
    return lane, i1, i2, 1.0 / den, e2 / den


def _to_row_tiles(dst_ref, x):
    rows = x.shape[0]
    for s in range(x.shape[1] // LANES):
        dst_ref[pl.ds(s, rows, stride=SUBLANES), :] = x[:, s * LANES:(s + 1) * LANES]


def _from_row_tiles(src_ref, rows, s):
    return src_ref[pl.ds(s, rows, stride=SUBLANES), :]


def _merge_route_kernel(attn_ref, y_ref, ga_ref, gb_ref, x_ref, wa_ref, wb_ref, wo_ref, g_ref, wr_ref, tri_ref,
                        x1_ref, xrt_ref, meta_ref, metat_ref, cnt_ref, carry_ref):
    i = pl.program_id(0)

    @pl.when(i == 0)
    def _():
        carry_ref[...] = jnp.zeros_like(carry_ref)

    merged = (jax.nn.sigmoid(ga_ref[...]) * _dot(attn_ref[...].astype(BF16), wa_ref[...])
              + jax.nn.sigmoid(gb_ref[...]) * _dot(y_ref[...], wb_ref[...]))
    x1 = x_ref[...] + _dot(merged.astype(BF16), wo_ref[...])
    x1_ref[...] = x1
    xn = _rmsnorm(x1, g_ref[...])
    _to_row_tiles(xrt_ref, xn)
    lane, i1, i2, w1, w2 = _top2(_dot(xn.astype(BF16), wr_ref[...]))
    oh1 = lane == i1
    oh2 = lane == i2
    tri = tri_ref[...]
    before1 = _dot(tri, oh1.astype(BF16))
    before2 = _dot(tri, oh2.astype(BF16))
    cnt1 = jnp.sum(oh1.astype(F32), axis=0, keepdims=True)
    cnt2 = jnp.sum(oh2.astype(F32), axis=0, keepdims=True)
    carry = carry_ref[0:1, :]
    rank1 = jnp.sum(jnp.where(oh1, carry + before1, 0.0), axis=-1, keepdims=True)
    rank2 = jnp.sum(jnp.where(oh2, carry + cnt1 + before2, 0.0), axis=-1, keepdims=True)
    total = carry + cnt1 + cnt2
    carry_ref[...] = jnp.broadcast_to(total, carry_ref.shape)
    cnt_ref[...] = jnp.broadcast_to(total, cnt_ref.shape)
    cols = (i1.astype(F32), i2.astype(F32), rank1, rank2, w1, w2)
    meta = jnp.zeros(meta_ref.shape, F32)
    for k, v in enumerate(cols):
        meta = jnp.where(lane == k, v, meta)
    meta_ref[...] = meta
    sel = (lax.broadcasted_iota(jnp.int32, (SUBLANES, LANES), 0)
           == lax.broadcasted_iota(jnp.int32, (SUBLANES, LANES), 1)).astype(BF16)
    meta_t = jnp.zeros(metat_ref.shape, F32)
    for piece in _split3(meta):
        meta_t = meta_t + _dot_nt(sel, piece.astype(BF16))
    metat_ref[...] = meta_t


def _merge_route(attn, y, ga, gb, x, wa, wb, wo, g, wr_pad):
    m, d = x.shape
    tm = min(ROW_TILE, m)
    tri = (lax.broadcasted_iota(jnp.int32, (tm, tm), 0)
           > lax.broadcasted_iota(jnp.int32, (tm, tm), 1)).astype(BF16)
    row = lambda w: pl.BlockSpec((tm, w), lambda i: (i, 0))
    return pl.pallas_call(
        _merge_route_kernel,
        grid=(m // tm,),
        in_specs=[row(attn.shape[1]), row(d), row(d), row(d), row(d)]
        + [_full(v.shape) for v in (wa, wb, wo, g, wr_pad, tri)],
        out_specs=[row(d), pl.BlockSpec((tm * SUBLANES, LANES), lambda i: (i, 0)), row(LANES),
                   pl.BlockSpec((SUBLANES, tm), lambda i: (0, i)),
                   pl.BlockSpec((SUBLANES, LANES), lambda i: (0, 0))],
        out_shape=[jax.ShapeDtypeStruct((m, d), F32), jax.ShapeDtypeStruct((m * SUBLANES, LANES), F32),
                   jax.ShapeDtypeStruct((m, LANES), F32), jax.ShapeDtypeStruct((SUBLANES, m), F32),
                   jax.ShapeDtypeStruct((SUBLANES, LANES), F32)],
        scratch_shapes=[pltpu.VMEM((SUBLANES, LANES), F32)],
        compiler_params=_cparams("arbitrary"),
        name="merge_route",
    )(attn, y, ga, gb, x, wa, wb, wo, g, wr_pad, tri)


def _row_copy(src_ref, src_row, dst_ref, dst_row, sem):
    return pltpu.make_async_copy(
        src_ref.at[pl.ds(pl.multiple_of(src_row * SUBLANES, SUBLANES), SUBLANES)],
        dst_ref.at[pl.ds(pl.multiple_of(dst_row * SUBLANES, SUBLANES), SUBLANES)], sem)


def _dispatch_kernel(d1_ref, d2_ref, x_ref, init_ref, xs_ref, sem):
    del init_ref
    tm = x_ref.shape[0] // SUBLANES
    base = pl.program_id(0) * tm

    def issue(r, carry):
        _row_copy(x_ref, r, xs_ref, d1_ref[base + r], sem).start(priority=0)
        _row_copy(x_ref, r, xs_ref, d2_ref[base + r], sem).start(priority=1)
        return carry

    lax.fori_loop(0, tm, issue, 0)

    def drain(r, carry):
        _row_copy(x_ref, r, xs_ref, d1_ref[base + r], sem).wait()
        _row_copy(x_ref, r, xs_ref, d2_ref[base + r], sem).wait()
        return carry

    lax.fori_loop(0, tm, drain, 0)


def _dispatch(dest1, dest2, x_rt, n_rows):
    m = dest1.shape[0]
    tm = min(ROW_TILE, m)
    grid_spec = pltpu.PrefetchScalarGridSpec(
        num_scalar_prefetch=2,
        grid=(m // tm,),
        in_specs=[pl.BlockSpec((tm * SUBLANES, LANES), lambda i, d1, d2: (i, 0)),
                  pl.BlockSpec(memory_space=pl.ANY)],
        out_specs=pl.BlockSpec(memory_space=pl.ANY),
        scratch_shapes=[pltpu.SemaphoreType.DMA],
    )
    return pl.pallas_call(
        _dispatch_kernel,
        grid_spec=grid_spec,
        out_shape=jax.ShapeDtypeStruct((n_rows * SUBLANES, LANES), F32),
        input_output_aliases={3: 0},
        compiler_params=_cparams("arbitrary"),
        name="moe_dispatch",
    )(dest1, dest2, x_rt, jnp.zeros((n_rows * SUBLANES, LANES), F32))


def _moe_ffn_kernel(te_ref, nu_ref, xs_ref, wg_ref, wu_ref, wd_ref, ys_ref, xb_ref, acc_ref):
    i = pl.program_id(0)
    f = pl.program_id(1)
    rows, d = xb_ref.shape

    @pl.when(i < nu_ref[0])
    def _():
        @pl.when(f == 0)
        def _():
            for s in range(d // LANES):
                xb_ref[:, s * LANES:(s + 1) * LANES] = _from_row_tiles(xs_ref, rows, s).astype(BF16)
            acc_ref[...] = jnp.zeros_like(acc_ref)

        xb = xb_ref[...]
        hid = jax.nn.silu(_dot(xb, wg_ref[0])) * _dot(xb, wu_ref[0])
        acc_ref[...] += _dot(hid.astype(BF16), wd_ref[0])

        @pl.when(f == pl.num_programs(1) - 1)
        def _():
            _to_row_tiles(ys_ref, acc_ref[...])

    @pl.when((i >= nu_ref[0]) & (f == pl.num_programs(1) - 1))
    def _():
        ys_ref[...] = jnp.zeros_like(ys_ref)


def _moe_ffn(tile_expert, n_used, xs, wg, wu, wd):
    n_e, d, ff = wg.shape
    tm = MOE_ROW_TILE
    tf = MOE_COL_TILE
    nf = ff // tf
    n_tiles = tile_expert.shape[0]

    def tile(i, f, te, nu):
        return (i, 0)

    def col(i, f, te, nu):
        return jnp.where(i < nu[0], f, nf - 1)

    grid_spec = pltpu.PrefetchScalarGridSpec(
        num_scalar_prefetch=2,
        grid=(n_tiles, nf),
        in_specs=[pl.BlockSpec((tm * SUBLANES, LANES), tile),
                  pl.BlockSpec((1, d, tf), lambda i, f, te, nu: (te[i], 0, col(i, f, te, nu))),
                  pl.BlockSpec((1, d, tf), lambda i, f, te, nu: (te[i], 0, col(i, f, te, nu))),
                  pl.BlockSpec((1, tf, d), lambda i, f, te, nu: (te[i], col(i, f, te, nu), 0))],
        out_specs=pl.BlockSpec((tm * SUBLANES, LANES), tile),
        scratch_shapes=[pltpu.VMEM((tm, d), BF16), pltpu.VMEM((tm, d), F32)],
    )
    return pl.pallas_call(
        _moe_ffn_kernel,
        grid_spec=grid_spec,
        out_shape=jax.ShapeDtypeStruct(xs.shape, F32),
        compiler_params=_cparams("arbitrary", "arbitrary"),
        name="moe_ffn",
    )(tile_expert, n_used, xs, wg, wu, wd)


def _combine_kernel(d1_ref, d2_ref, x1_ref, meta_ref, ys_ref, o_ref, g1_ref, g2_ref, sem):
    tm, d = x1_ref.shape
    base = pl.program_id(0) * tm

    def issue(r, carry):
        _row_copy(ys_ref, d1_ref[base + r], g1_ref, r, sem).start(priority=0)
        _row_copy(ys_ref, d2_ref[base + r], g2_ref, r, sem).start(priority=1)
        return carry

    lax.fori_loop(0, tm, issue, 0)

    def drain(r, carry):
        _row_copy(ys_ref, d1_ref[base + r], g1_ref, r, sem).wait()
        _row_copy(ys_ref, d2_ref[base + r], g2_ref, r, sem).wait()
        return carry

    lax.fori_loop(0, tm, drain, 0)
    w1 = meta_ref[:, 4:5]
    w2 = meta_ref[:, 5:6]
    for s in range(d // LANES):
        cs = slice(s * LANES, (s + 1) * LANES)
        o_ref[:, cs] = (x1_ref[:, cs] + w1 * _from_row_tiles(g1_ref, tm, s)
                        + w2 * _from_row_tiles(g2_ref, tm, s))


def _combine(dest1, dest2, x1, meta, ys):
    m, d = x1.shape
    tm = min(ROW_TILE, m)
    grid_spec = pltpu.PrefetchScalarGridSpec(
        num_scalar_prefetch=2,
        grid=(m // tm,),
        in_specs=[pl.BlockSpec((tm, d), lambda i, d1, d2: (i, 0)),
                  pl.BlockSpec((tm, LANES), lambda i, d1, d2: (i, 0)),
                  pl.BlockSpec(memory_space=pl.ANY)],
        out_specs=pl.BlockSpec((tm, d), lambda i, d1, d2: (i, 0)),
        scratch_shapes=[pltpu.VMEM((tm * SUBLANES, LANES), F32), pltpu.VMEM((tm * SUBLANES, LANES), F32),
                        pltpu.SemaphoreType.DMA],
    )
    return pl.pallas_call(
        _combine_kernel,
        grid_spec=grid_spec,
        out_shape=jax.ShapeDtypeStruct((m, d), F32),
        compiler_params=_cparams("arbitrary"),
        name="moe_combine",
    )(dest1, dest2, x1, meta, ys)


SC_CORES = 2
SC_SUBCORES = 16
SC_GATHER_ROWS = 32


def _sc_gather_rows(table, idx):
    n = idx.shape[0]
    workers = SC_CORES * SC_SUBCORES
    per_worker = n // workers
    chunk = SC_GATHER_ROWS
    mesh = plsc.VectorSubcoreMesh(core_axis_name="c", subcore_axis_name="s",
                                  num_cores=SC_CORES, num_subcores=SC_SUBCORES)

    def body(table_hbm, idx_hbm, out_hbm, idx_v, rows_v, sem):
        base = (lax.axis_index("s") * SC_CORES + lax.axis_index("c")) * per_worker
        pltpu.sync_copy(idx_hbm.at[pl.ds(base, per_worker)], idx_v)

        @pl.loop(0, per_worker // chunk)
        def _(j):
            off = pl.multiple_of(j * chunk, chunk)
            pltpu.async_copy(table_hbm.at[idx_v.at[pl.ds(off, chunk)]], rows_v, sem).wait()
            pltpu.sync_copy(rows_v, out_hbm.at[pl.ds(base + off, chunk)])

    return pl.kernel(
        body,
        out_type=jax.ShapeDtypeStruct((n,) + table.shape[1:], table.dtype),
        mesh=mesh,
        scratch_types=[pltpu.VMEM((per_worker,), jnp.int32), pltpu.VMEM((chunk,) + table.shape[1:], table.dtype),
                       pltpu.SemaphoreType.DMA],
        name="sc_row_gather",
    )(table, idx)


def _combine_dense_kernel(x1_ref, meta_ref, g1_ref, g2_ref, o_ref):
    tm, d = x1_ref.shape
    w1 = meta_ref[:, 4:5]
    w2 = meta_ref[:, 5:6]
    for s in range(d // LANES):
        cs = slice(s * LANES, (s + 1) * LANES)
        o_ref[:, cs] = (x1_ref[:, cs] + w1 * _from_row_tiles(g1_ref, tm, s)
                        + w2 * _from_row_tiles(g2_ref, tm, s))


def _combine_dense(x1, meta, gathered):
    m, d = x1.shape
    tm = min(ROW_TILE, m)
    nt = m // tm
    row = lambda w: pl.BlockSpec((tm, w), lambda i: (i, 0))
    return pl.pallas_call(
        _combine_dense_kernel,
        grid=(nt,),
        in_specs=[row(d), row(LANES), pl.BlockSpec((tm * SUBLANES, LANES), lambda i: (i, 0)),
                  pl.BlockSpec((tm * SUBLANES, LANES), lambda i: (nt + i, 0))],
        out_specs=row(d),
        out_shape=jax.ShapeDtypeStruct((m, d), F32),
        compiler_params=_cparams("arbitrary"),
        name="moe_combine",
    )(x1, meta, gathered, gathered)


def _moe_sorted(x1, x_rt, meta, meta_t, counts, wg, wu, wd):
    m = x1.shape[0]
    tm = MOE_ROW_TILE
    n_tiles = (2 * m) // tm + N_EXPERTS
    cnt = counts[0, :N_EXPERTS].astype(jnp.int32)
    padded = ((cnt + tm - 1) // tm) * tm
    ends = jnp.cumsum(padded)
    starts = ends - padded
    experts = jnp.arange(N_EXPERTS, dtype=jnp.int32)[:, None]
    fields = meta_t.astype(jnp.int32)

    def dest(expert_row, rank_row):
        return jnp.sum(jnp.where(fields[expert_row][None, :] == experts, starts[:, None], 0), axis=0) + fields[rank_row]

    dest1 = dest(0, 2)
    dest2 = dest(1, 3)
    n_used = (ends[-1] // tm).reshape(1)
    tile_start = jnp.minimum(jnp.arange(n_tiles, dtype=jnp.int32), n_used - 1) * tm
    tile_expert = jnp.sum(ends[None, :] <= tile_start[:, None], axis=1).astype(jnp.int32)
    xs = _dispatch(dest1, dest2, x_rt, n_tiles * tm)
    ys = _moe_ffn(tile_expert, n_used.astype(jnp.int32), xs, wg, wu, wd)
    gathered = _sc_gather_rows(ys.reshape(-1, SUBLANES, LANES), jnp.concatenate([dest1, dest2]))
    return _combine_dense(x1, meta, gathered.reshape(-1, LANES))


def _ffn_kernel(xn_ref, x1_ref, wg_ref, wu_ref, wd_ref, o_ref, acc_ref):
    f = pl.program_id(1)

    @pl.when(f == 0)
    def _():
        acc_ref[...] = jnp.zeros_like(acc_ref)

    xn = xn_ref[...]
    hid = jax.nn.silu(_dot(xn, wg_ref[...])) * _dot(xn, wu_ref[...])
    acc_ref[...] += _dot(hid.astype(BF16), wd_ref[...])

    @pl.when(f == pl.num_programs(1) - 1)
    def _():
        o_ref[...] = x1_ref[...] + acc_ref[...]


def _ffn(xn, x1, wg, wu, wd):
    m, d = x1.shape
    ff = wg.shape[1]
    tm = min(FFN_ROW_TILE, m)
    tf = FFN_COL_TILE
    row = pl.BlockSpec((tm, d), lambda i, f: (i, 0))
    return pl.pallas_call(
        _ffn_kernel,
        grid=(m // tm, ff // tf),
        in_specs=[row, row, pl.BlockSpec((d, tf), lambda i, f: (0, f)), pl.BlockSpec((d, tf), lambda i, f: (0, f)),
                  pl.BlockSpec((tf, d), lambda i, f: (f, 0))],
        out_specs=row,
        out_shape=jax.ShapeDtypeStruct((m, d), F32),
        scratch_shapes=[pltpu.VMEM((tm, d), F32)],
        compiler_params=_cparams("arbitrary", "arbitrary"),
        name="ffn_dense",
    )(xn, x1, wg, wu, wd)


def _router_kernel(xn_ref, wr_ref, gates_ref):
    lane, i1, i2, w1, w2 = _top2(_dot(xn_ref[...], wr_ref[...]))
    gates_ref[...] = jnp.where(lane == i1, w1, 0.0) + jnp.where(lane == i2, w2, 0.0)


def _router(xn, wr_pad):
    m, d = xn.shape
    tm = min(ROW_TILE, m)
    return pl.pallas_call(
        _router_kernel,
        grid=(m // tm,),
        in_specs=[pl.BlockSpec((tm, d), lambda i: (i, 0)), _full(wr_pad.shape)],
        out_specs=pl.BlockSpec((tm, LANES), lambda i: (i, 0)),
        out_shape=jax.ShapeDtypeStruct((m, LANES), F32),
        compiler_params=_cparams("arbitrary"),
        name="router",
    )(xn, wr_pad)


def _moe_kernel(xn_ref, x1_ref, gates_ref, wg_ref, wu_ref, wd_ref, o_ref, acc_ref):
    e = pl.program_id(1)
    f = pl.program_id(2)

    @pl.when((e == 0) & (f == 0))
    def _():
        acc_ref[...] = jnp.zeros_like(acc_ref)

    xn = xn_ref[...]
    hid = jax.nn.silu(_dot(xn, wg_ref[0])) * _dot(xn, wu_ref[0])
    gates = gates_ref[...]
    lane = lax.broadcasted_iota(jnp.int32, gates.shape, 1)
    gate = jnp.sum(jnp.where(lane == e, gates, 0.0), axis=-1, keepdims=True)
    acc_ref[...] += gate * _dot(hid.astype(BF16), wd_ref[0])

    @pl.when((e == pl.num_programs(1) - 1) & (f == pl.num_programs(2) - 1))
    def _():
        o_ref[...] = x1_ref[...] + acc_ref[...]


def _moe(xn, x1, gates, wg, wu, wd):
    m, d = x1.shape
    n_e, _, ff = wg.shape
    tm = min(FFN_ROW_TILE, m)
    tf = MOE_COL_TILE
    row = lambda w: pl.BlockSpec((tm, w), lambda i, e, f: (i, 0))
    return pl.pallas_call(
        _moe_kernel,
        grid=(m // tm, n_e, ff // tf),
        in_specs=[row(d), row(d), row(LANES),
                  pl.BlockSpec((1, d, tf), lambda i, e, f: (e, 0, f)),
                  pl.BlockSpec((1, d, tf), lambda i, e, f: (e, 0, f)),
                  pl.BlockSpec((1, tf, d), lambda i, e, f: (e, f, 0))],
        out_specs=row(d),
        out_shape=jax.ShapeDtypeStruct((m, d), F32),
        scratch_shapes=[pltpu.VMEM((tm, d), F32)],
        compiler_params=_cparams("arbitrary", "arbitrary", "arbitrary"),
        name="moe_dense",
    )(xn, x1, gates, wg, wu, wd)


def _ple_kernel(final, x_ref, p_ref, g_ref, wgate_ref, wproj_ref, gf_ref, o_ref):
    x = x_ref[...]
    gate = jax.nn.sigmoid(_dot(_rmsnorm(x, g_ref[...]).astype(BF16), wgate_ref[...]))
    x3 = x + gate * _dot(p_ref[0].astype(BF16), wproj_ref[...])
    o_ref[...] = _rmsnorm(x3, gf_ref[...]) if final else x3


def _ple(x, p_all, layer, g, wgate, wproj, g_final, final):
    m, d = x.shape
    tm = min(ROW_TILE, m)
    row = lambda w: pl.BlockSpec((tm, w), lambda i: (i, 0))
    return pl.pallas_call(
        functools.partial(_ple_kernel, final),
        grid=(m // tm,),
        in_specs=[row(d), pl.BlockSpec((1, tm, p_all.shape[2]), lambda i: (layer, i, 0))]
        + [_full(v.shape) for v in (g, wgate, wproj, g_final)],
        out_specs=row(d),
        out_shape=jax.ShapeDtypeStruct((m, d), F32),
        compiler_params=_cparams("arbitrary"),
        name="ple",
    )(x, p_all, g, wgate, wproj, g_final)


def _row(v):
    return v.reshape(1, -1)


def kernel(x_prompt, x_sample, p_prompt, p_sample, cache_k, cache_v, cache_logf, state_conv, state_h, page_table, g_mix, w_in, b_f, conv_w, conv_b, w_rg, b_rg, w_ig, b_ig, lru_lambda, w_a_up, w_b_up, w_out, g_ffn, dense_wg, dense_wu, dense_wd, moe_router, moe_wg, moe_wu, moe_wd, g_ple, w_ple_gate, w_ple_proj, g_final):
    depth = g_mix.shape[0]
    b, t, d = x_prompt.shape
    db = x_sample.shape[0]
    n_pool, page = cache_k.shape[1], cache_k.shape[2]
    a = ATTN_WIDTH
    kt_pool = jnp.transpose(cache_k, (0, 1, 3, 4, 2)).reshape(depth * n_pool * ATTN_WIDTH, page)
    vt_pool = jnp.transpose(cache_v, (0, 1, 3, 4, 2)).reshape(depth * n_pool * ATTN_WIDTH, page)
    lt_pool = jnp.transpose(cache_logf, (0, 1, 3, 2)).reshape(depth * n_pool, N_HEADS, page)

    xp = x_prompt
    xs = x_sample.reshape(db, d)
    m = b * t
    pp_all = p_prompt.reshape(depth, m, -1)
    ps_all = p_sample.reshape(depth, db, -1)
    kt_all = jnp.zeros((depth, b, a, t), F32)
    vt_all = jnp.zeros((depth, b, a, t), F32)
    outs = {name: [] for name in ("lp", "cp", "hp", "ks", "vs", "ls", "cs", "hs")}
    for l in range(depth):
        wl = w_in[l]
        wq = wl[:, 0:a].astype(BF16)
        wk = wl[:, a:2 * a].astype(BF16)
        wv = wl[:, 2 * a:3 * a].astype(BF16)
        wf = jnp.pad(wl[:, 3 * a:3 * a + N_HEADS], ((0, 0), (0, LANES - N_HEADS))).astype(BF16)
        w4 = wl[:, 3 * a + N_HEADS:].astype(BF16)
        bf_row = jnp.pad(b_f[l], (0, LANES - N_HEADS)).reshape(1, LANES)
        g1 = _row(g_mix[l])
        lru_w = (conv_w[l], _row(conv_b[l]), w_rg[l].astype(BF16), _row(b_rg[l]), w_ig[l].astype(BF16),
                 _row(b_ig[l]), _row(lru_lambda[l]))
        wa = w_a_up[l].astype(BF16)
        wb = w_b_up[l].astype(BF16)
        wo = w_out[l].astype(BF16)
        g2 = _row(g_ffn[l])
        g3 = _row(g_ple[l])
        wgate = w_ple_gate[l].astype(BF16)
        wproj = w_ple_proj[l].astype(BF16)
        gf = _row(g_final)
        final = l == depth - 1
        mi = l // 2
        if l % 2 == 0:
            ffn_w = (dense_wg[mi].astype(BF16), dense_wu[mi].astype(BF16), dense_wd[mi].astype(BF16))
        else:
            wr_pad = jnp.pad(moe_router[mi], ((0, 0), (0, LANES - N_EXPERTS))).astype(BF16)
            ffn_w = (moe_wg[mi].astype(BF16), moe_wu[mi].astype(BF16), moe_wd[mi].astype(BF16))

        qa, kta, kt_all, vt_all, vtb, lft, xr, gr, ga, gb = _proj_prompt(
            xp, g1, wq, wk.T, wv.T, wf, bf_row, w4, l, kt_all, vt_all)
        attn = _attn_prompt(qa, kta, vtb)
        y, tail, hl = _lru_prompt(xr, gr, jnp.zeros((b, SUBLANES, d), F32), jnp.zeros((b, 1, d), F32), *lru_w)
        merge_in = (attn.reshape(m, a), y.reshape(m, d), ga.reshape(m, d), gb.reshape(m, d),
                    xp.reshape(m, d), wa, wb, wo, g2)
        if l % 2 == 0:
            x1, xn = _merge(*merge_in)
            x2 = _ffn(xn, x1, *ffn_w)
        else:
            x1, x_rt, meta, meta_t, counts = _merge_route(*merge_in, wr_pad)
            x2 = _moe_sorted(x1, x_rt, meta, meta_t, counts, *ffn_w)
        xp = _ple(x2, pp_all, l, g3, wgate, wproj, gf, final).reshape(b, t, d)
        outs["lp"].append(jnp.transpose(lft, (0, 2, 1)))
        outs["cp"].append(tail[:, SUBLANES - (CONV_WIDTH - 1):, :])
        outs["hp"].append(hl.reshape(b, d))

        qs, ks, vs, lfs, xrs, grs, gas, gbs = _proj_sample(xs, g1, wq, wk, wv, wf, bf_row, w4)
        lf8 = lfs[:, :N_HEADS]
        attn_s = _attn_decode(l, page_table, qs.reshape(db, 1, a), ks.reshape(db, 1, a), vs.reshape(db, 1, a),
                              lf8.reshape(db, N_HEADS, 1), kt_pool, vt_pool, lt_pool, n_pool).reshape(db, a)
        sc = state_conv[l]
        ys, hn = _lru_sample(xrs, grs, sc[:, 0], sc[:, 1], sc[:, 2], state_h[l], *lru_w)
        x1s, xns = _merge(attn_s, ys, gas, gbs, xs, wa, wb, wo, g2)
        if l % 2 == 0:
            x2s = _ffn(xns, x1s, *ffn_w)
        else:
            x2s = _moe(xns, x1s, _router(xns, wr_pad), *ffn_w)
        xs = _ple(x2s, ps_all, l, g3, wgate, wproj, gf, final)
        outs["ks"].append(ks.reshape(db, 1, N_HEADS, HEAD_DIM))
        outs["vs"].append(vs.reshape(db, 1, N_HEADS, HEAD_DIM))
        outs["ls"].append(lf8.reshape(db, 1, N_HEADS))
        outs["cs"].append(jnp.stack([sc[:, 1], sc[:, 2], xrs], axis=1))
        outs["hs"].append(hn)

    st = lambda name: jnp.stack(outs[name])
    heads_last = lambda v: jnp.transpose(v.reshape(depth, b, N_HEADS, HEAD_DIM, t), (0, 1, 4, 2, 3))
    return (xp, xs.reshape(db, 1, d), heads_last(kt_all), heads_last(vt_all), st("lp"), st("cp"), st("hp"),
            st("ks"), st("vs"), st("ls"), st("cs"), st("hs"))
```

```python
import functools

import jax
import jax.numpy as jnp
from jax import lax
from jax.experimental import pallas as pl
from jax.experimental.pallas import tpu as pltpu
from jax.experimental.pallas import tpu_sc as plsc

BF16 = jnp.bfloat16
F32 = jnp.float32

EPS = 1e-6
NEG_INF = -1e30
LRU_C = 8.0
N_HEADS = 8
HEAD_DIM = 64
ATTN_WIDTH = N_HEADS * HEAD_DIM
N_LRU_BLOCKS = 8
CONV_WIDTH = 4
N_EXPERTS = 8

LANES = 128
SUBLANES = 8
VMEM_LIMIT_BYTES = 56 * 1024 * 1024

PROJ_TILE = 512
ATTN_TILE = 512
LRU_TILE = 256
ROW_TILE = 512
FFN_ROW_TILE = 1024
FFN_COL_TILE = 512
MOE_ROW_TILE = 512
MOE_COL_TILE = 1792
DECODE_PAGES = 16


def _cparams(*sem):
    return pltpu.CompilerParams(dimension_semantics=sem, vmem_limit_bytes=VMEM_LIMIT_BYTES)


def _full(shape):
    zeros = (0,) * len(shape)
    return pl.BlockSpec(shape, lambda *_: zeros, pipeline_mode=pl.Buffered(1))


def _rmsnorm(x, g):
    return x * lax.rsqrt(jnp.mean(x * x, axis=-1, keepdims=True) + EPS) * g


def _log_sigmoid(z):
    return jnp.minimum(z, 0.0) - jnp.log1p(jnp.exp(-jnp.abs(z)))


def _softplus(z):
    return jnp.maximum(z, 0.0) + jnp.log1p(jnp.exp(-jnp.abs(z)))


def _dot(a, b):
    return jnp.dot(a, b, preferred_element_type=F32)


def _dot_nt(a, b):
    return lax.dot_general(a, b, (((1,), (1,)), ((), ())), preferred_element_type=F32)


def _exact_dot(x, w01):
    hi = x.astype(BF16)
    r1 = x - hi.astype(F32)
    mid = r1.astype(BF16)
    lo = (r1 - mid.astype(F32)).astype(BF16)
    return _dot(hi, w01) + _dot(mid, w01) + _dot(lo, w01)


def _split3(x):
    p1 = x.astype(BF16).astype(F32)
    r1 = x - p1
    p2 = r1.astype(BF16).astype(F32)
    p3 = (r1 - p2).astype(BF16).astype(F32)
    return p1, p2, p3


def _proj_prompt_kernel(x_ref, g_ref, wq_ref, wkt_ref, wvt_ref, wf_ref, wft_ref, bfr_ref, bfc_ref, w4_ref,
                        triu_ref, tril_ref, kt_all_ref, vt_all_ref,
                        qa_ref, kta_ref, kt_ref, vt_ref, vtb_ref, lft_ref,
                        xr_ref, gr_ref, ga_ref, gb_ref, carry_t_ref, carry_c_ref):
    del kt_all_ref, vt_all_ref
    t = pl.program_id(1)

    @pl.when(t == 0)
    def _():
        carry_t_ref[...] = jnp.zeros_like(carry_t_ref)
        carry_c_ref[...] = jnp.zeros_like(carry_c_ref)

    tm = x_ref.shape[1]
    xn = _rmsnorm(x_ref[0], g_ref[...]).astype(BF16)
    lft = _log_sigmoid(_dot_nt(wft_ref[...], xn) + bfc_ref[...])[:N_HEADS]
    lft_ref[0] = lft
    lfc = _log_sigmoid(_dot(xn, wf_ref[...]) + bfr_ref[...])
    ct = _exact_dot(lft, triu_ref[...]) + carry_t_ref[:, :1]
    carry_t_ref[...] = jnp.broadcast_to(ct[:, -1:], carry_t_ref.shape)
    tril = tril_ref[...]
    cc = carry_c_ref[0:1, :]
    for piece in _split3(lfc):
        cc = cc + _dot(tril, piece.astype(BF16))
    carry_c_ref[...] = jnp.broadcast_to(cc[-1:, :], carry_c_ref.shape)

    q = _dot(xn, wq_ref[...]) * (HEAD_DIM ** -0.5)
    kt = _dot_nt(wkt_ref[...], xn)
    kt_ref[0, 0] = kt
    lane = lax.broadcasted_iota(jnp.int32, (tm, HEAD_DIM), 1)
    sub = lax.broadcasted_iota(jnp.int32, (HEAD_DIM, tm), 0)
    for h in range(N_HEADS):
        hs = slice(h * HEAD_DIM, (h + 1) * HEAD_DIM)
        q1, q2, q3 = _split3(cc[:, h:h + 1])
        q_extra = jnp.where(lane == 0, q1, jnp.where(lane == 1, q2, jnp.where(lane == 2, q3,
                            jnp.where(lane < 6, 1.0, 0.0))))
        qa_ref[0, h] = jnp.concatenate([q[:, hs], q_extra], axis=1).astype(BF16)
        k1, k2, k3 = _split3(ct[h:h + 1, :])
        k_extra = jnp.where(sub < 3, 1.0, jnp.where(sub == 3, -k1, jnp.where(sub == 4, -k2,
                            jnp.where(sub == 5, -k3, 0.0))))
        kta_ref[0, 2 * h * HEAD_DIM:2 * (h + 1) * HEAD_DIM, :] = jnp.concatenate(
            [kt[hs, :], k_extra], axis=0).astype(BF16)
    vt = _dot_nt(wvt_ref[...], xn)
    vt_ref[0, 0] = vt
    vtb_ref[0] = vt.astype(BF16)
    d = xr_ref.shape[-1]
    xr_ref[0] = _dot(xn, w4_ref[:, 0 * d:1 * d])
    gr_ref[0] = _dot(xn, w4_ref[:, 1 * d:2 * d])
    ga_ref[0] = _dot(xn, w4_ref[:, 2 * d:3 * d])
    gb_ref[0] = _dot(xn, w4_ref[:, 3 * d:4 * d])


def _proj_prompt(x, g, wq, wkt, wvt, wf, bf_row, w4, layer, kt_all, vt_all):
    b, t, d = x.shape
    tm = min(PROJ_TILE, t)
    nt = t // tm
    ri = lax.broadcasted_iota(jnp.int32, (tm, tm), 0)
    ci = lax.broadcasted_iota(jnp.int32, (tm, tm), 1)
    triu = (ri <= ci).astype(BF16)
    tril = (ri >= ci).astype(BF16)
    a = ATTN_WIDTH
    seq = lambda w: pl.BlockSpec((1, tm, w), lambda i, j: (i, j, 0))
    tr = lambda r: pl.BlockSpec((1, r, tm), lambda i, j: (i, 0, j))
    out_shape = [
        jax.ShapeDtypeStruct((b, N_HEADS, t, 2 * HEAD_DIM), BF16),
        jax.ShapeDtypeStruct((b, 2 * a, t), BF16),
        jax.ShapeDtypeStruct(kt_all.shape, F32), jax.ShapeDtypeStruct(vt_all.shape, F32),
        jax.ShapeDtypeStruct((b, a, t), BF16),
        jax.ShapeDtypeStruct((b, N_HEADS, t), F32),
    ] + [jax.ShapeDtypeStruct((b, t, d), F32)] * 4
    stacked = pl.BlockSpec((1, 1, a, tm), lambda i, j: (layer, i, 0, j))
    out_specs = [
        pl.BlockSpec((1, N_HEADS, tm, 2 * HEAD_DIM), lambda i, j: (i, 0, j, 0)),
        tr(2 * a), stacked, stacked, tr(a), tr(N_HEADS),
        seq(d), seq(d), seq(d), seq(d),
    ]
    args = (x, g, wq, wkt, wvt, wf, wf.T, bf_row, bf_row.reshape(LANES, 1), w4, triu, tril)
    return pl.pallas_call(
        _proj_prompt_kernel,
        grid=(b, nt),
        in_specs=[seq(d)] + [_full(v.shape) for v in args[1:]] + [pl.BlockSpec(memory_space=pl.ANY)] * 2,
        out_specs=out_specs,
        out_shape=out_shape,
        input_output_aliases={len(args): 2, len(args) + 1: 3},
        scratch_shapes=[pltpu.VMEM((N_HEADS, LANES), F32), pltpu.VMEM((N_HEADS, LANES), F32)],
        compiler_params=_cparams("arbitrary", "arbitrary"),
        name="proj_prompt",
    )(*args, kt_all, vt_all)


def _proj_sample_kernel(x_ref, g_ref, wq_ref, wk_ref, wv_ref, wf_ref, bf_ref, w4_ref,
                        q_ref, k_ref, v_ref, lf_ref, xr_ref, gr_ref, ga_ref, gb_ref):
    xn = _rmsnorm(x_ref[...], g_ref[...]).astype(BF16)
    q_ref[...] = _dot(xn, wq_ref[...])
    k_ref[...] = _dot(xn, wk_ref[...])
    v_ref[...] = _dot(xn, wv_ref[...])
    lf_ref[...] = _log_sigmoid(_dot(xn, wf_ref[...]) + bf_ref[...])
    d = xr_ref.shape[-1]
    xr_ref[...] = _dot(xn, w4_ref[:, 0 * d:1 * d])
    gr_ref[...] = _dot(xn, w4_ref[:, 1 * d:2 * d])
    ga_ref[...] = _dot(xn, w4_ref[:, 2 * d:3 * d])
    gb_ref[...] = _dot(xn, w4_ref[:, 3 * d:4 * d])


def _proj_sample(x, g, wq, wk, wv, wf, bf_row, w4):
    m, d = x.shape
    a = ATTN_WIDTH
    shapes = [(m, a), (m, a), (m, a), (m, LANES), (m, d), (m, d), (m, d), (m, d)]
    return pl.pallas_call(
        _proj_sample_kernel,
        grid=(1,),
        in_specs=[_full(v.shape) for v in (x, g, wq, wk, wv, wf, bf_row, w4)],
        out_specs=[_full(s) for s in shapes],
        out_shape=[jax.ShapeDtypeStruct(s, F32) for s in shapes],
        compiler_params=_cparams("arbitrary"),
        name="proj_sample",
    )(x, g, wq, wk, wv, wf, bf_row, w4)


def _attn_prompt_kernel(qi_ref, ki_ref, q_ref, kta_ref, vt_ref, o_ref, m_ref, l_ref, acc_ref):
    step = pl.program_id(1)
    qi = qi_ref[step]
    ki = ki_ref[step]
    ta = q_ref.shape[2]

    @pl.when(ki == 0)
    def _():
        m_ref[...] = jnp.full_like(m_ref, NEG_INF)
        l_ref[...] = jnp.zeros_like(l_ref)
        acc_ref[...] = jnp.zeros_like(acc_ref)

    def update(on_diagonal):
        if on_diagonal:
            mask = (lax.broadcasted_iota(jnp.int32, (ta, ta), 1)
                    <= lax.broadcasted_iota(jnp.int32, (ta, ta), 0))
        for h in range(N_HEADS):
            s = _dot(q_ref[0, h], kta_ref[0, 2 * h * HEAD_DIM:2 * (h + 1) * HEAD_DIM, :])
            if on_diagonal:
                s = jnp.where(mask, s, NEG_INF)
            m_old = m_ref[h]
            m_new = jnp.maximum(m_old, jnp.max(s, axis=-1, keepdims=True))
            p = jnp.exp(s - jnp.concatenate([m_new] * (ta // LANES), axis=1))
            alpha = jnp.exp(m_old - m_new)
            l_ref[h] = alpha * l_ref[h] + jnp.sum(p, axis=-1, keepdims=True)
            pv = _dot_nt(p.astype(BF16), vt_ref[0, h * HEAD_DIM:(h + 1) * HEAD_DIM, :])
            acc_ref[h] = alpha[:, :HEAD_DIM] * acc_ref[h] + pv
            m_ref[h] = m_new

    @pl.when(ki < qi)
    def _():
        update(False)

    @pl.when(ki == qi)
    def _():
        update(True)
        for h in range(0, N_HEADS, 2):
            pair = [acc_ref[h + j] / l_ref[h + j][:, :HEAD_DIM] for j in range(2)]
            o_ref[0, :, h * HEAD_DIM:(h + 2) * HEAD_DIM] = jnp.concatenate(pair, axis=1).astype(o_ref.dtype)


def _attn_prompt(qa, kta, vtb):
    b, _, t, _ = qa.shape
    ta = min(ATTN_TILE, t)
    nq = t // ta
    pairs = [(i, j) for i in range(nq) for j in range(i + 1)]
    qi_tab = jnp.asarray([p[0] for p in pairs], jnp.int32)
    ki_tab = jnp.asarray([p[1] for p in pairs], jnp.int32)
    a = ATTN_WIDTH
    grid_spec = pltpu.PrefetchScalarGridSpec(
        num_scalar_prefetch=2,
        grid=(b, len(pairs)),
        in_specs=[
            pl.BlockSpec((1, N_HEADS, ta, 2 * HEAD_DIM), lambda i, s, qt, kt: (i, 0, qt[s], 0)),
            pl.BlockSpec((1, 2 * a, ta), lambda i, s, qt, kt: (i, 0, kt[s])),
            pl.BlockSpec((1, a, ta), lambda i, s, qt, kt: (i, 0, kt[s])),
        ],
        out_specs=pl.BlockSpec((1, ta, a), lambda i, s, qt, kt: (i, qt[s], 0)),
        scratch_shapes=[pltpu.VMEM((N_HEADS, ta, LANES), F32), pltpu.VMEM((N_HEADS, ta, LANES), F32),
                        pltpu.VMEM((N_HEADS, ta, HEAD_DIM), F32)],
    )
    return pl.pallas_call(
        _attn_prompt_kernel,
        grid_spec=grid_spec,
        out_shape=jax.ShapeDtypeStruct((b, t, a), BF16),
        compiler_params=_cparams("arbitrary", "arbitrary"),
        name="attn_prompt",
    )(qi_tab, ki_tab, qa, kta, vtb)


def _attn_decode_kernel(n_pg, pt_ref, q_ref, kn_ref, vn_ref, lfn_ref, w_ref, *rest):
    k_refs = rest[0:n_pg]
    v_refs = rest[n_pg:2 * n_pg]
    lt_refs = rest[2 * n_pg:3 * n_pg]
    o_ref = rest[3 * n_pg]
    m_ref, l_ref, c_ref, acc_ref = rest[3 * n_pg + 1:]
    c_step = pl.program_id(1)
    width = acc_ref.shape[1]
    own = (lax.broadcasted_iota(jnp.int32, (N_HEADS, width), 1) // HEAD_DIM
           == lax.broadcasted_iota(jnp.int32, (N_HEADS, width), 0))
    q_rows = jnp.where(own, q_ref[0] * (HEAD_DIM ** -0.5), 0.0)

    @pl.when(c_step == 0)
    def _():
        lane = lax.broadcasted_iota(jnp.int32, (N_HEADS, LANES), 1)
        s0 = jnp.sum(q_rows * kn_ref[0], axis=1, keepdims=True)
        m_ref[...] = jnp.broadcast_to(s0, m_ref.shape)
        l_ref[...] = jnp.where(lane == 0, 1.0, 0.0)
        acc_ref[...] = jnp.broadcast_to(vn_ref[0], acc_ref.shape)
        c_ref[...] = jnp.broadcast_to(lfn_ref[0], c_ref.shape)

    qb = q_rows.astype(BF16)
    carry = c_ref[...]
    s = []
    for i in range(n_pg):
        r = _exact_dot(lt_refs[i][0], w_ref[...])
        s.append(_dot(qb, k_refs[i][...].astype(BF16)) + (r[:, :LANES] + carry))
        carry = carry + r[:, LANES:]
    c_ref[...] = carry

    m_tile = s[0]
    for i in range(1, n_pg):
        m_tile = jnp.maximum(m_tile, s[i])
    m_old = m_ref[...]
    m_new = jnp.maximum(m_old, jnp.max(m_tile, axis=1, keepdims=True))
    alpha = jnp.exp(m_old - m_new)
    l_new = alpha * l_ref[...]
    pv = jnp.zeros(acc_ref.shape, F32)
    for i in range(n_pg):
        p = jnp.exp(s[i] - m_new)
        l_new = l_new + p
        pv = pv + _dot_nt(p.astype(BF16), v_refs[i][...].astype(BF16))
    m_ref[...] = m_new
    l_ref[...] = l_new
    acc_ref[...] = jnp.concatenate([alpha] * (width // LANES), axis=1) * acc_ref[...] + pv

    @pl.when(c_step == pl.num_programs(1) - 1)
    def _():
        den = jnp.sum(l_ref[...], axis=1, keepdims=True)
        o_ref[0] = jnp.sum(jnp.where(own, acc_ref[...] / den, 0.0), axis=0, keepdims=True)


def _attn_decode(layer, page_table, q, kn, vn, lfn_col, kt_pool, vt_pool, lt_pool, n_pool):
    db, n_pages = page_table.shape
    n_pg = min(DECODE_PAGES, n_pages)
    n_chunks = n_pages // n_pg
    page = kt_pool.shape[-1]
    base = layer * n_pool
    rowi = lax.broadcasted_iota(jnp.int32, (page, 2 * page), 0)
    coli = lax.broadcasted_iota(jnp.int32, (page, 2 * page), 1)
    w = ((rowi > coli) | (coli >= page)).astype(BF16)

    def page_of(slot):
        def index(b, c, pt):
            return base + pt[b * n_pages + (n_pages - 1 - (c * n_pg + slot))]
        return index

    per_seq = lambda r, c: pl.BlockSpec((1, r, c), lambda b, ch, pt: (b, 0, 0))

    def kv_specs():
        return [pl.BlockSpec((ATTN_WIDTH, page), (lambda b, ch, pt, f=page_of(i): (f(b, ch, pt), 0)))
                for i in range(n_pg)]

    lt_specs = [pl.BlockSpec((1, N_HEADS, page), (lambda b, ch, pt, f=page_of(i): (f(b, ch, pt), 0, 0)))
                for i in range(n_pg)]
    grid_spec = pltpu.PrefetchScalarGridSpec(
        num_scalar_prefetch=1,
        grid=(db, n_chunks),
        in_specs=[per_seq(1, ATTN_WIDTH), per_seq(1, ATTN_WIDTH), per_seq(1, ATTN_WIDTH), per_seq(N_HEADS, 1),
                  pl.BlockSpec(w.shape, lambda b, ch, pt: (0, 0))] + kv_specs() + kv_specs() + lt_specs,
        out_specs=per_seq(1, ATTN_WIDTH),
        scratch_shapes=[pltpu.VMEM((N_HEADS, LANES), F32), pltpu.VMEM((N_HEADS, LANES), F32),
                        pltpu.VMEM((N_HEADS, LANES), F32), pltpu.VMEM((N_HEADS, ATTN_WIDTH), F32)],
    )
    return pl.pallas_call(
        functools.partial(_attn_decode_kernel, n_pg),
        grid_spec=grid_spec,
        out_shape=jax.ShapeDtypeStruct((db, 1, ATTN_WIDTH), F32),
        compiler_params=_cparams("arbitrary", "arbitrary"),
        name="attn_decode",
    )(page_table.reshape(-1), q, kn, vn, lfn_col, w,
      *([kt_pool] * n_pg), *([vt_pool] * n_pg), *([lt_pool] * n_pg))


def _lru_gates(xc, wrg_ref, brg_ref, wig_ref, big_ref, lam_ref, store):
    xcb = xc.astype(BF16)
    blk = xc.shape[-1] // N_LRU_BLOCKS
    for n in range(N_LRU_BLOCKS):
        cs = slice(n * blk, (n + 1) * blk)
        r = jax.nn.sigmoid(_dot(xcb[:, cs], wrg_ref[n]) + brg_ref[:, cs])
        i = jax.nn.sigmoid(_dot(xcb[:, cs], wig_ref[n]) + big_ref[:, cs])
        log_a = -LRU_C * r * _softplus(-lam_ref[:, cs])
        a = jnp.exp(log_a)
        th = jnp.tanh(log_a)
        mult = jnp.sqrt(-2.0 * th / (1.0 - th))
        store(cs, a, mult * (i * xc[:, cs]))


def _lru_prompt_kernel(xr_ref, gr_ref, cp_ref, h0_ref, cw_ref, cb_ref, wrg_ref, brg_ref, wig_ref,
                       big_ref, lam_ref, y_ref, tail_ref, hl_ref, buf_ref, a_ref, b_ref, h_ref, hc_ref):
    t = pl.program_id(1)
    tt = xr_ref.shape[1]
    halo = SUBLANES

    @pl.when(t == 0)
    def _():
        buf_ref[0:halo, :] = cp_ref[0]
        hc_ref[...] = h0_ref[0]

    x = xr_ref[0]
    buf_ref[halo:, :] = x
    taps = [buf_ref[pl.ds(halo - (CONV_WIDTH - 1) + j, tt), :] * cw_ref[j:j + 1, :]
            for j in range(CONV_WIDTH - 1)]
    taps.append(x * cw_ref[CONV_WIDTH - 1:CONV_WIDTH, :])
    acc = taps[0]
    for term in taps[1:]:
        acc = acc + term
    xc = cb_ref[...] + acc

    def store(cs, a, b):
        a_ref[:, cs] = a
        b_ref[:, cs] = b

    _lru_gates(xc, wrg_ref, brg_ref, wig_ref, big_ref, lam_ref, store)

    grouped = (tt // SUBLANES, SUBLANES, a_ref.shape[1])
    a = a_ref[...].reshape(grouped)
    b = b_ref[...].reshape(grouped)
    in_group = lax.broadcasted_iota(jnp.int32, grouped, 1)
    shift = 1
    while shift < SUBLANES:
        a_prev = pltpu.roll(a, shift, axis=1)
        b_prev = pltpu.roll(b, shift, axis=1)
        take = in_group >= shift
        b = jnp.where(take, a * b_prev + b, b)
        a = jnp.where(take, a * a_prev, a)
        shift *= 2
    a_ref[...] = a.reshape(a_ref.shape)
    b_ref[...] = b.reshape(b_ref.shape)
    h_in = jnp.broadcast_to(hc_ref[...], (SUBLANES, a_ref.shape[1]))
    for g in range(tt // SUBLANES):
        rows = slice(g * SUBLANES, (g + 1) * SUBLANES)
        h_grp = a_ref[rows, :] * h_in + b_ref[rows, :]
        h_ref[rows, :] = h_grp
        h_in = jnp.broadcast_to(h_grp[SUBLANES - 1:, :], h_grp.shape)
    h_last = h_in[:1, :]
    hc_ref[...] = h_last
    hl_ref[0] = h_last
    y_ref[0] = (h_ref[...] * jax.nn.gelu(gr_ref[0])).astype(y_ref.dtype)
    tail = buf_ref[tt:tt + halo, :]
    tail_ref[0] = tail
    buf_ref[0:halo, :] = tail


def _lru_prompt(xr, gr, conv_prev8, h0, cw, cb, wrg, brg, wig, big, lam):
    b, t, d = xr.shape
    tt = min(LRU_TILE, t)
    seq = pl.BlockSpec((1, tt, d), lambda i, j: (i, j, 0))
    per_b = lambda r: pl.BlockSpec((1, r, d), lambda i, j: (i, 0, 0))
    return pl.pallas_call(
        _lru_prompt_kernel,
        grid=(b, t // tt),
        in_specs=[seq, seq, per_b(SUBLANES), per_b(1)] + [_full(v.shape) for v in (cw, cb, wrg, brg, wig, big, lam)],
        out_specs=[seq, per_b(SUBLANES), per_b(1)],
        out_shape=[jax.ShapeDtypeStruct((b, t, d), BF16), jax.ShapeDtypeStruct((b, SUBLANES, d), F32),
                   jax.ShapeDtypeStruct((b, 1, d), F32)],
        scratch_shapes=[pltpu.VMEM((tt + SUBLANES, d), F32), pltpu.VMEM((tt, d), F32),
                        pltpu.VMEM((tt, d), F32), pltpu.VMEM((tt, d), F32), pltpu.VMEM((1, d), F32)],
        compiler_params=_cparams("arbitrary", "arbitrary"),
        name="lru_prompt",
    )(xr, gr, conv_prev8, h0, cw, cb, wrg, brg, wig, big, lam)


def _lru_sample_kernel(xr_ref, gr_ref, c0_ref, c1_ref, c2_ref, h0_ref, cw_ref, cb_ref, wrg_ref, brg_ref,
                       wig_ref, big_ref, lam_ref, y_ref, hn_ref):
    x = xr_ref[...]
    acc = c0_ref[...] * cw_ref[0:1, :]
    acc = acc + c1_ref[...] * cw_ref[1:2, :]
    acc = acc + c2_ref[...] * cw_ref[2:3, :]
    acc = acc + x * cw_ref[3:4, :]
    xc = cb_ref[...] + acc

    def store(cs, a, b):
        hn_ref[:, cs] = a * h0_ref[:, cs] + b

    _lru_gates(xc, wrg_ref, brg_ref, wig_ref, big_ref, lam_ref, store)
    y_ref[...] = (hn_ref[...] * jax.nn.gelu(gr_ref[...])).astype(y_ref.dtype)


def _lru_sample(xr, gr, c0, c1, c2, h0, cw, cb, wrg, brg, wig, big, lam):
    m, d = xr.shape
    args = (xr, gr, c0, c1, c2, h0, cw, cb, wrg, brg, wig, big, lam)
    return pl.pallas_call(
        _lru_sample_kernel,
        grid=(1,),
        in_specs=[_full(v.shape) for v in args],
        out_specs=[_full((m, d)), _full((m, d))],
        out_shape=[jax.ShapeDtypeStruct((m, d), BF16), jax.ShapeDtypeStruct((m, d), F32)],
        compiler_params=_cparams("arbitrary"),
        name="lru_sample",
    )(*args)


def _merge_kernel(attn_ref, y_ref, ga_ref, gb_ref, x_ref, wa_ref, wb_ref, wo_ref, g_ref, x1_ref, xn_ref):
    merged = (jax.nn.sigmoid(ga_ref[...]) * _dot(attn_ref[...].astype(BF16), wa_ref[...])
              + jax.nn.sigmoid(gb_ref[...]) * _dot(y_ref[...], wb_ref[...]))
    x1 = x_ref[...] + _dot(merged.astype(BF16), wo_ref[...])
    x1_ref[...] = x1
    xn_ref[...] = _rmsnorm(x1, g_ref[...]).astype(xn_ref.dtype)


def _merge(attn, y, ga, gb, x, wa, wb, wo, g):
    m, d = x.shape
    tm = min(ROW_TILE, m)
    row = lambda w: pl.BlockSpec((tm, w), lambda i: (i, 0))
    return pl.pallas_call(
        _merge_kernel,
        grid=(m // tm,),
        in_specs=[row(attn.shape[1]), row(d), row(d), row(d), row(d)] + [_full(v.shape) for v in (wa, wb, wo, g)],
        out_specs=[row(d), row(d)],
        out_shape=[jax.ShapeDtypeStruct((m, d), F32), jax.ShapeDtypeStruct((m, d), BF16)],
        compiler_params=_cparams("arbitrary"),
        name="merge",
    )(attn, y, ga, gb, x, wa, wb, wo, g)


def _top2(logits):
    lane = lax.broadcasted_iota(jnp.int32, logits.shape, 1)
    logits = jnp.where(lane < N_EXPERTS, logits, -jnp.inf)
    m1 = jnp.max(logits, axis=-1, keepdims=True)
    i1 = jnp.min(jnp.where(logits == m1, lane, LANES), axis=-1, keepdims=True)
    rest = jnp.where(lane == i1, -jnp.inf, logits)
    m2 = jnp.max(rest, axis=-1, keepdims=True)
    i2 = jnp.min(jnp.where(rest == m2, lane, LANES), axis=-1, keepdims=True)
    e2 = jnp.exp(m2 - m1)
    den = 1.0 + e2
    return lane, i1, i2, 1.0 / den, e2 / den


def _to_row_tiles(dst_ref, x):
    rows = x.shape[0]
    for s in range(x.shape[1] // LANES):
        dst_ref[pl.ds(s, rows, stride=SUBLANES), :] = x[:, s * LANES:(s + 1) * LANES]


def _from_row_tiles(src_ref, rows, s):
    return src_ref[pl.ds(s, rows, stride=SUBLANES), :]


def _merge_route_kernel(attn_ref, y_ref, ga_ref, gb_ref, x_ref, wa_ref, wb_ref, wo_ref, g_ref, wr_ref, tri_ref,
                        x1_ref, xrt_ref, meta_ref, metat_ref, cnt_ref, carry_ref):
    i = pl.program_id(0)

    @pl.when(i == 0)
    def _():
        carry_ref[...] = jnp.zeros_like(carry_ref)

    merged = (jax.nn.sigmoid(ga_ref[...]) * _dot(attn_ref[...].astype(BF16), wa_ref[...])
              + jax.nn.sigmoid(gb_ref[...]) * _dot(y_ref[...], wb_ref[...]))
    x1 = x_ref[...] + _dot(merged.astype(BF16), wo_ref[...])
    x1_ref[...] = x1
    xn = _rmsnorm(x1, g_ref[...])
    _to_row_tiles(xrt_ref, xn)
    lane, i1, i2, w1, w2 = _top2(_dot(xn.astype(BF16), wr_ref[...]))
    oh1 = lane == i1
    oh2 = lane == i2
    tri = tri_ref[...]
    before1 = _dot(tri, oh1.astype(BF16))
    before2 = _dot(tri, oh2.astype(BF16))
    cnt1 = jnp.sum(oh1.astype(F32), axis=0, keepdims=True)
    cnt2 = jnp.sum(oh2.astype(F32), axis=0, keepdims=True)
    carry = carry_ref[0:1, :]
    rank1 = jnp.sum(jnp.where(oh1, carry + before1, 0.0), axis=-1, keepdims=True)
    rank2 = jnp.sum(jnp.where(oh2, carry + cnt1 + before2, 0.0), axis=-1, keepdims=True)
    total = carry + cnt1 + cnt2
    carry_ref[...] = jnp.broadcast_to(total, carry_ref.shape)
    cnt_ref[...] = jnp.broadcast_to(total, cnt_ref.shape)
    cols = (i1.astype(F32), i2.astype(F32), rank1, rank2, w1, w2)
    meta = jnp.zeros(meta_ref.shape, F32)
    for k, v in enumerate(cols):
        meta = jnp.where(lane == k, v, meta)
    meta_ref[...] = meta
    sel = (lax.broadcasted_iota(jnp.int32, (SUBLANES, LANES), 0)
           == lax.broadcasted_iota(jnp.int32, (SUBLANES, LANES), 1)).astype(BF16)
    meta_t = jnp.zeros(metat_ref.shape, F32)
    for piece in _split3(meta):
        meta_t = meta_t + _dot_nt(sel, piece.astype(BF16))
    metat_ref[...] = meta_t


def _merge_route(attn, y, ga, gb, x, wa, wb, wo, g, wr_pad):
    m, d = x.shape
    tm = min(ROW_TILE, m)
    tri = (lax.broadcasted_iota(jnp.int32, (tm, tm), 0)
           > lax.broadcasted_iota(jnp.int32, (tm, tm), 1)).astype(BF16)
    row = lambda w: pl.BlockSpec((tm, w), lambda i: (i, 0))
    return pl.pallas_call(
        _merge_route_kernel,
        grid=(m // tm,),
        in_specs=[row(attn.shape[1]), row(d), row(d), row(d), row(d)]
        + [_full(v.shape) for v in (wa, wb, wo, g, wr_pad, tri)],
        out_specs=[row(d), pl.BlockSpec((tm * SUBLANES, LANES), lambda i: (i, 0)), row(LANES),
                   pl.BlockSpec((SUBLANES, tm), lambda i: (0, i)),
                   pl.BlockSpec((SUBLANES, LANES), lambda i: (0, 0))],
        out_shape=[jax.ShapeDtypeStruct((m, d), F32), jax.ShapeDtypeStruct((m * SUBLANES, LANES), F32),
                   jax.ShapeDtypeStruct((m, LANES), F32), jax.ShapeDtypeStruct((SUBLANES, m), F32),
                   jax.ShapeDtypeStruct((SUBLANES, LANES), F32)],
        scratch_shapes=[pltpu.VMEM((SUBLANES, LANES), F32)],
        compiler_params=_cparams("arbitrary"),
        name="merge_route",
    )(attn, y, ga, gb, x, wa, wb, wo, g, wr_pad, tri)


SC_CORES = 2
SC_SUBCORES = 16
SC_CHUNK_ROWS = 32


def _sc_mesh():
    return plsc.VectorSubcoreMesh(core_axis_name="c", subcore_axis_name="s",
                                  num_cores=SC_CORES, num_subcores=SC_SUBCORES)


def _sc_worker_base(per_worker):
    return (lax.axis_index("s") * SC_CORES + lax.axis_index("c")) * per_worker


def _sc_scatter_rows(rows, dest1, dest2, n_out):
    n = rows.shape[0]
    per_worker = n // (SC_CORES * SC_SUBCORES)
    chunk = SC_CHUNK_ROWS

    def body(rows_hbm, d1_hbm, d2_hbm, out_hbm, i1_v, i2_v, rows_v, sem):
        base = _sc_worker_base(per_worker)

        @pl.loop(0, per_worker // chunk)
        def _(j):
            src = pl.ds(base + pl.multiple_of(j * chunk, chunk), chunk)
            pltpu.sync_copy(d1_hbm.at[src], i1_v)
            pltpu.sync_copy(d2_hbm.at[src], i2_v)
            pltpu.sync_copy(rows_hbm.at[src], rows_v)
            first = pltpu.async_copy(rows_v, out_hbm.at[i1_v], sem)
            second = pltpu.async_copy(rows_v, out_hbm.at[i2_v], sem)
            first.wait()
            second.wait()

    return pl.kernel(
        body,
        out_type=jax.ShapeDtypeStruct((n_out,) + rows.shape[1:], rows.dtype),
        mesh=_sc_mesh(),
        scratch_types=[pltpu.VMEM((chunk,), jnp.int32), pltpu.VMEM((chunk,), jnp.int32),
                       pltpu.VMEM((chunk,) + rows.shape[1:], rows.dtype), pltpu.SemaphoreType.DMA],
        name="sc_row_scatter",
    )(rows, dest1, dest2)


def _moe_ffn_kernel(te_ref, tv_ref, nu_ref, xs_ref, wg_ref, wu_ref, wd_ref, ys_ref, xb_ref, acc_ref):
    i = pl.program_id(0)
    f = pl.program_id(1)
    rows, d = xb_ref.shape

    @pl.when(i < nu_ref[0])
    def _():
        @pl.when(f == 0)
        def _():
            live = lax.broadcasted_iota(jnp.int32, (rows, LANES), 0) < tv_ref[i]
            for s in range(d // LANES):
                slab = jnp.where(live, _from_row_tiles(xs_ref, rows, s), 0.0)
                xb_ref[:, s * LANES:(s + 1) * LANES] = slab.astype(BF16)
            acc_ref[...] = jnp.zeros_like(acc_ref)

        xb = xb_ref[...]
        hid = jax.nn.silu(_dot(xb, wg_ref[0])) * _dot(xb, wu_ref[0])
        acc_ref[...] += _dot(hid.astype(BF16), wd_ref[0])

        @pl.when(f == pl.num_programs(1) - 1)
        def _():
            _to_row_tiles(ys_ref, acc_ref[...])

    @pl.when((i >= nu_ref[0]) & (f == pl.num_programs(1) - 1))
    def _():
        ys_ref[...] = jnp.zeros_like(ys_ref)


def _moe_ffn(tile_expert, tile_valid, n_used, xs, wg, wu, wd):
    n_e, d, ff = wg.shape
    tm = MOE_ROW_TILE
    tf = MOE_COL_TILE
    nf = ff // tf
    n_tiles = tile_expert.shape[0]

    def tile(i, f, te, tv, nu):
        return (i, 0)

    def col(i, f, nu):
        return jnp.where(i < nu[0], f, nf - 1)

    grid_spec = pltpu.PrefetchScalarGridSpec(
        num_scalar_prefetch=3,
        grid=(n_tiles, nf),
        in_specs=[pl.BlockSpec((tm * SUBLANES, LANES), tile),
                  pl.BlockSpec((1, d, tf), lambda i, f, te, tv, nu: (te[i], 0, col(i, f, nu))),
                  pl.BlockSpec((1, d, tf), lambda i, f, te, tv, nu: (te[i], 0, col(i, f, nu))),
                  pl.BlockSpec((1, tf, d), lambda i, f, te, tv, nu: (te[i], col(i, f, nu), 0))],
        out_specs=pl.BlockSpec((tm * SUBLANES, LANES), tile),
        scratch_shapes=[pltpu.VMEM((tm, d), BF16), pltpu.VMEM((tm, d), F32)],
    )
    return pl.pallas_call(
        _moe_ffn_kernel,
        grid_spec=grid_spec,
        out_shape=jax.ShapeDtypeStruct(xs.shape, F32),
        compiler_params=_cparams("arbitrary", "arbitrary"),
        name="moe_ffn",
    )(tile_expert, tile_valid, n_used, xs, wg, wu, wd)


def _sc_gather_rows(table, idx):
    n = idx.shape[0]
    per_worker = n // (SC_CORES * SC_SUBCORES)
    chunk = SC_CHUNK_ROWS
    mesh = _sc_mesh()

    def body(table_hbm, idx_hbm, out_hbm, idx_v, rows_v, sem):
        base = _sc_worker_base(per_worker)
        pltpu.sync_copy(idx_hbm.at[pl.ds(base, per_worker)], idx_v)

        @pl.loop(0, per_worker // chunk)
        def _(j):
            off = pl.multiple_of(j * chunk, chunk)
            pltpu.async_copy(table_hbm.at[idx_v.at[pl.ds(off, chunk)]], rows_v, sem).wait()
            pltpu.sync_copy(rows_v, out_hbm.at[pl.ds(base + off, chunk)])

    return pl.kernel(
        body,
        out_type=jax.ShapeDtypeStruct((n,) + table.shape[1:], table.dtype),
        mesh=mesh,
        scratch_types=[pltpu.VMEM((per_worker,), jnp.int32), pltpu.VMEM((chunk,) + table.shape[1:], table.dtype),
                       pltpu.SemaphoreType.DMA],
        name="sc_row_gather",
    )(table, idx)


def _combine_dense_kernel(x1_ref, meta_ref, g1_ref, g2_ref, o_ref):
    tm, d = x1_ref.shape
    w1 = meta_ref[:, 4:5]
    w2 = meta_ref[:, 5:6]
    for s in range(d // LANES):
        cs = slice(s * LANES, (s + 1) * LANES)
        o_ref[:, cs] = (x1_ref[:, cs] + w1 * _from_row_tiles(g1_ref, tm, s)
                        + w2 * _from_row_tiles(g2_ref, tm, s))


def _combine_dense(x1, meta, gathered):
    m, d = x1.shape
    tm = min(ROW_TILE, m)
    nt = m // tm
    row = lambda w: pl.BlockSpec((tm, w), lambda i: (i, 0))
    return pl.pallas_call(
        _combine_dense_kernel,
        grid=(nt,),
        in_specs=[row(d), row(LANES), pl.BlockSpec((tm * SUBLANES, LANES), lambda i: (i, 0)),
                  pl.BlockSpec((tm * SUBLANES, LANES), lambda i: (nt + i, 0))],
        out_specs=row(d),
        out_shape=jax.ShapeDtypeStruct((m, d), F32),
        compiler_params=_cparams("arbitrary"),
        name="moe_combine",
    )(x1, meta, gathered, gathered)


def _moe_sorted(x1, x_rt, meta, meta_t, counts, wg, wu, wd):
    m = x1.shape[0]
    tm = MOE_ROW_TILE
    n_tiles = (2 * m) // tm + N_EXPERTS
    cnt = counts[0, :N_EXPERTS].astype(jnp.int32)
    padded = ((cnt + tm - 1) // tm) * tm
    ends = jnp.cumsum(padded)
    starts = ends - padded
    experts = jnp.arange(N_EXPERTS, dtype=jnp.int32)[:, None]
    fields = meta_t.astype(jnp.int32)

    def dest(expert_row, rank_row):
        return jnp.sum(jnp.where(fields[expert_row][None, :] == experts, starts[:, None], 0), axis=0) + fields[rank_row]

    dest1 = dest(0, 2)
    dest2 = dest(1, 3)
    n_used = (ends[-1] // tm).reshape(1)
    tile_start = jnp.minimum(jnp.arange(n_tiles, dtype=jnp.int32), n_used - 1) * tm
    tile_expert = jnp.sum(ends[None, :] <= tile_start[:, None], axis=1).astype(jnp.int32)
    group_end = jnp.sum(jnp.where(tile_expert[:, None] == experts.T, (starts + cnt)[None, :], 0), axis=1)
    tile_valid = jnp.clip(group_end - tile_start, 0, tm).astype(jnp.int32)
    xs = _sc_scatter_rows(x_rt.reshape(-1, SUBLANES, LANES), dest1, dest2, n_tiles * tm).reshape(-1, LANES)
    ys = _moe_ffn(tile_expert, tile_valid, n_used.astype(jnp.int32), xs, wg, wu, wd)
    gathered = _sc_gather_rows(ys.reshape(-1, SUBLANES, LANES), jnp.concatenate([dest1, dest2]))
    return _combine_dense(x1, meta, gathered.reshape(-1, LANES))


def _ffn_kernel(xn_ref, x1_ref, wg_ref, wu_ref, wd_ref, o_ref, acc_ref):
    f = pl.program_id(1)

    @pl.when(f == 0)
    def _():
        acc_ref[...] = jnp.zeros_like(acc_ref)

    xn = xn_ref[...]
    hid = jax.nn.silu(_dot(xn, wg_ref[...])) * _dot(xn, wu_ref[...])
    acc_ref[...] += _dot(hid.astype(BF16), wd_ref[...])

    @pl.when(f == pl.num_programs(1) - 1)
    def _():
        o_ref[...] = x1_ref[...] + acc_ref[...]


def _ffn(xn, x1, wg, wu, wd):
    m, d = x1.shape
    ff = wg.shape[1]
    tm = min(FFN_ROW_TILE, m)
    tf = FFN_COL_TILE
    row = pl.BlockSpec((tm, d), lambda i, f: (i, 0))
    return pl.pallas_call(
        _ffn_kernel,
        grid=(m // tm, ff // tf),
        in_specs=[row, row, pl.BlockSpec((d, tf), lambda i, f: (0, f)), pl.BlockSpec((d, tf), lambda i, f: (0, f)),
                  pl.BlockSpec((tf, d), lambda i, f: (f, 0))],
        out_specs=row,
        out_shape=jax.ShapeDtypeStruct((m, d), F32),
        scratch_shapes=[pltpu.VMEM((tm, d), F32)],
        compiler_params=_cparams("arbitrary", "arbitrary"),
        name="ffn_dense",
    )(xn, x1, wg, wu, wd)


def _router_kernel(xn_ref, wr_ref, gates_ref):
    lane, i1, i2, w1, w2 = _top2(_dot(xn_ref[...], wr_ref[...]))
    gates_ref[...] = jnp.where(lane == i1, w1, 0.0) + jnp.where(lane == i2, w2, 0.0)


def _router(xn, wr_pad):
    m, d = xn.shape
    tm = min(ROW_TILE, m)
    return pl.pallas_call(
        _router_kernel,
        grid=(m // tm,),
        in_specs=[pl.BlockSpec((tm, d), lambda i: (i, 0)), _full(wr_pad.shape)],
        out_specs=pl.BlockSpec((tm, LANES), lambda i: (i, 0)),
        out_shape=jax.ShapeDtypeStruct((m, LANES), F32),
        compiler_params=_cparams("arbitrary"),
        name="router",
    )(xn, wr_pad)


def _moe_kernel(xn_ref, x1_ref, gates_ref, wg_ref, wu_ref, wd_ref, o_ref, acc_ref):
    e = pl.program_id(1)
    f = pl.program_id(2)

    @pl.when((e == 0) & (f == 0))
    def _():
        acc_ref[...] = jnp.zeros_like(acc_ref)

    xn = xn_ref[...]
    hid = jax.nn.silu(_dot(xn, wg_ref[0])) * _dot(xn, wu_ref[0])
    gates = gates_ref[...]
    lane = lax.broadcasted_iota(jnp.int32, gates.shape, 1)
    gate = jnp.sum(jnp.where(lane == e, gates, 0.0), axis=-1, keepdims=True)
    acc_ref[...] += gate * _dot(hid.astype(BF16), wd_ref[0])

    @pl.when((e == pl.num_programs(1) - 1) & (f == pl.num_programs(2) - 1))
    def _():
        o_ref[...] = x1_ref[...] + acc_ref[...]


def _moe(xn, x1, gates, wg, wu, wd):
    m, d = x1.shape
    n_e, _, ff = wg.shape
    tm = min(FFN_ROW_TILE, m)
    tf = MOE_COL_TILE
    row = lambda w: pl.BlockSpec((tm, w), lambda i, e, f: (i, 0))
    return pl.pallas_call(
        _moe_kernel,
        grid=(m // tm, n_e, ff // tf),
        in_specs=[row(d), row(d), row(LANES),
                  pl.BlockSpec((1, d, tf), lambda i, e, f: (e, 0, f)),
                  pl.BlockSpec((1, d, tf), lambda i, e, f: (e, 0, f)),
                  pl.BlockSpec((1, tf, d), lambda i, e, f: (e, f, 0))],
        out_specs=row(d),
        out_shape=jax.ShapeDtypeStruct((m, d), F32),
        scratch_shapes=[pltpu.VMEM((tm, d), F32)],
        compiler_params=_cparams("arbitrary", "arbitrary", "arbitrary"),
        name="moe_dense",
    )(xn, x1, gates, wg, wu, wd)


def _ple_kernel(final, x_ref, p_ref, g_ref, wgate_ref, wproj_ref, gf_ref, o_ref):
    x = x_ref[...]
    gate = jax.nn.sigmoid(_dot(_rmsnorm(x, g_ref[...]).astype(BF16), wgate_ref[...]))
    x3 = x + gate * _dot(p_ref[0].astype(BF16), wproj_ref[...])
    o_ref[...] = _rmsnorm(x3, gf_ref[...]) if final else x3


def _ple(x, p_all, layer, g, wgate, wproj, g_final, final):
    m, d = x.shape
    tm = min(ROW_TILE, m)
    row = lambda w: pl.BlockSpec((tm, w), lambda i: (i, 0))
    return pl.pallas_call(
        functools.partial(_ple_kernel, final),
        grid=(m // tm,),
        in_specs=[row(d), pl.BlockSpec((1, tm, p_all.shape[2]), lambda i: (layer, i, 0))]
        + [_full(v.shape) for v in (g, wgate, wproj, g_final)],
        out_specs=row(d),
        out_shape=jax.ShapeDtypeStruct((m, d), F32),
        compiler_params=_cparams("arbitrary"),
        name="ple",
    )(x, p_all, g, wgate, wproj, g_final)


def _row(v):
    return v.reshape(1, -1)


def kernel(x_prompt, x_sample, p_prompt, p_sample, cache_k, cache_v, cache_logf, state_conv, state_h, page_table, g_mix, w_in, b_f, conv_w, conv_b, w_rg, b_rg, w_ig, b_ig, lru_lambda, w_a_up, w_b_up, w_out, g_ffn, dense_wg, dense_wu, dense_wd, moe_router, moe_wg, moe_wu, moe_wd, g_ple, w_ple_gate, w_ple_proj, g_final):
    depth = g_mix.shape[0]
    b, t, d = x_prompt.shape
    db = x_sample.shape[0]
    n_pool, page = cache_k.shape[1], cache_k.shape[2]
    a = ATTN_WIDTH
    kt_pool = jnp.transpose(cache_k, (0, 1, 3, 4, 2)).reshape(depth * n_pool * ATTN_WIDTH, page)
    vt_pool = jnp.transpose(cache_v, (0, 1, 3, 4, 2)).reshape(depth * n_pool * ATTN_WIDTH, page)
    lt_pool = jnp.transpose(cache_logf, (0, 1, 3, 2)).reshape(depth * n_pool, N_HEADS, page)

    xp = x_prompt
    xs = x_sample.reshape(db, d)
    m = b * t
    pp_all = p_prompt.reshape(depth, m, -1)
    ps_all = p_sample.reshape(depth, db, -1)
    kt_all = jnp.zeros((depth, b, a, t), F32)
    vt_all = jnp.zeros((depth, b, a, t), F32)
    outs = {name: [] for name in ("lp", "cp", "hp", "ks", "vs", "ls", "cs", "hs")}
    for l in range(depth):
        wl = w_in[l]
        wq = wl[:, 0:a].astype(BF16)
        wk = wl[:, a:2 * a].astype(BF16)
        wv = wl[:, 2 * a:3 * a].astype(BF16)
        wf = jnp.pad(wl[:, 3 * a:3 * a + N_HEADS], ((0, 0), (0, LANES - N_HEADS))).astype(BF16)
        w4 = wl[:, 3 * a + N_HEADS:].astype(BF16)
        bf_row = jnp.pad(b_f[l], (0, LANES - N_HEADS)).reshape(1, LANES)
        g1 = _row(g_mix[l])
        lru_w = (conv_w[l], _row(conv_b[l]), w_rg[l].astype(BF16), _row(b_rg[l]), w_ig[l].astype(BF16),
                 _row(b_ig[l]), _row(lru_lambda[l]))
        wa = w_a_up[l].astype(BF16)
        wb = w_b_up[l].astype(BF16)
        wo = w_out[l].astype(BF16)
        g2 = _row(g_ffn[l])
        g3 = _row(g_ple[l])
        wgate = w_ple_gate[l].astype(BF16)
        wproj = w_ple_proj[l].astype(BF16)
        gf = _row(g_final)
        final = l == depth - 1
        mi = l // 2
        if l % 2 == 0:
            ffn_w = (dense_wg[mi].astype(BF16), dense_wu[mi].astype(BF16), dense_wd[mi].astype(BF16))
        else:
            wr_pad = jnp.pad(moe_router[mi], ((0, 0), (0, LANES - N_EXPERTS))).astype(BF16)
            ffn_w = (moe_wg[mi].astype(BF16), moe_wu[mi].astype(BF16), moe_wd[mi].astype(BF16))

        qa, kta, kt_all, vt_all, vtb, lft, xr, gr, ga, gb = _proj_prompt(
            xp, g1, wq, wk.T, wv.T, wf, bf_row, w4, l, kt_all, vt_all)
        attn = _attn_prompt(qa, kta, vtb)
        y, tail, hl = _lru_prompt(xr, gr, jnp.zeros((b, SUBLANES, d), F32), jnp.zeros((b, 1, d), F32), *lru_w)
        merge_in = (attn.reshape(m, a), y.reshape(m, d), ga.reshape(m, d), gb.reshape(m, d),
                    xp.reshape(m, d), wa, wb, wo, g2)
        if l % 2 == 0:
            x1, xn = _merge(*merge_in)
            x2 = _ffn(xn, x1, *ffn_w)
        else:
            x1, x_rt, meta, meta_t, counts = _merge_route(*merge_in, wr_pad)
            x2 = _moe_sorted(x1, x_rt, meta, meta_t, counts, *ffn_w)
        xp = _ple(x2, pp_all, l, g3, wgate, wproj, gf, final).reshape(b, t, d)
        outs["lp"].append(jnp.transpose(lft, (0, 2, 1)))
        outs["cp"].append(tail[:, SUBLANES - (CONV_WIDTH - 1):, :])
        outs["hp"].append(hl.reshape(b, d))

        qs, ks, vs, lfs, xrs, grs, gas, gbs = _proj_sample(xs, g1, wq, wk, wv, wf, bf_row, w4)
        lf8 = lfs[:, :N_HEADS]
        attn_s = _attn_decode(l, page_table, qs.reshape(db, 1, a), ks.reshape(db, 1, a), vs.reshape(db, 1, a),
                              lf8.reshape(db, N_HEADS, 1), kt_pool, vt_pool, lt_pool, n_pool).reshape(db, a)
        sc = state_conv[l]
        ys, hn = _lru_sample(xrs, grs, sc[:, 0], sc[:, 1], sc[:, 2], state_h[l], *lru_w)
        x1s, xns = _merge(attn_s, ys, gas, gbs, xs, wa, wb, wo, g2)
        if l % 2 == 0:
            x2s = _ffn(xns, x1s, *ffn_w)
        else:
            x2s = _moe(xns, x1s, _router(xns, wr_pad), *ffn_w)
        xs = _ple(x2s, ps_all, l, g3, wgate, wproj, gf, final)
        outs["ks"].append(ks.reshape(db, 1, N_HEADS, HEAD_DIM))
        outs["vs"].append(vs.reshape(db, 1, N_HEADS, HEAD_DIM))
        outs["ls"].append(lf8.reshape(db, 1, N_HEADS))
        outs["cs"].append(jnp.stack([sc[:, 1], sc[:, 2], xrs], axis=1))
        outs["hs"].append(hn)

    st = lambda name: jnp.stack(outs[name])
    heads_last = lambda v: jnp.transpose(v.reshape(depth, b, N_HEADS, HEAD_DIM, t), (0, 1, 4, 2, 3))
    return (xp, xs.reshape(db, 1, d), heads_last(kt_all), heads_last(vt_all), st("lp"), st("cp"), st("hp"),
            st("ks"), st("vs"), st("ls"), st("cs"), st("hs"))
```

```python
import functools

import jax
import jax.numpy as jnp
from jax import lax
from jax.experimental import pallas as pl
from jax.experimental.pallas import tpu as pltpu
from jax.experimental.pallas import tpu_sc as plsc

BF16 = jnp.bfloat16
F32 = jnp.float32

EPS = 1e-6
NEG_INF = -1e30
LRU_C = 8.0
N_HEADS = 8
HEAD_DIM = 64
ATTN_WIDTH = N_HEADS * HEAD_DIM
N_LRU_BLOCKS = 8
CONV_WIDTH = 4
N_EXPERTS = 8

LANES = 128
SUBLANES = 8
VMEM_LIMIT_BYTES = 56 * 1024 * 1024

PROJ_TILE = 512
ATTN_TILE = 512
LRU_TILE = 256
ROW_TILE = 512
FFN_ROW_TILE = 1024
FFN_COL_TILE = 512
MOE_ROW_TILE = 512
MOE_COL_TILE = 1792
DECODE_PAGES = 16


def _cparams(*sem):
    return pltpu.CompilerParams(dimension_semantics=sem, vmem_limit_bytes=VMEM_LIMIT_BYTES)


def _full(shape):
    zeros = (0,) * len(shape)
    return pl.BlockSpec(shape, lambda *_: zeros, pipeline_mode=pl.Buffered(1))


def _rmsnorm(x, g):
    return x * lax.rsqrt(jnp.mean(x * x, axis=-1, keepdims=True) + EPS) * g


def _log_sigmoid(z):
    return jnp.minimum(z, 0.0) - jnp.log1p(jnp.exp(-jnp.abs(z)))


def _softplus(z):
    return jnp.maximum(z, 0.0) + jnp.log1p(jnp.exp(-jnp.abs(z)))


def _dot(a, b):
    return jnp.dot(a, b, preferred_element_type=F32)


def _dot_nt(a, b):
    return lax.dot_general(a, b, (((1,), (1,)), ((), ())), preferred_element_type=F32)


def _exact_dot(x, w01):
    hi = x.astype(BF16)
    r1 = x - hi.astype(F32)
    mid = r1.astype(BF16)
    lo = (r1 - mid.astype(F32)).astype(BF16)
    return _dot(hi, w01) + _dot(mid, w01) + _dot(lo, w01)


def _split3(x):
    p1 = x.astype(BF16).astype(F32)
    r1 = x - p1
    p2 = r1.astype(BF16).astype(F32)
    p3 = (r1 - p2).astype(BF16).astype(F32)
    return p1, p2, p3


def _proj_prompt_kernel(x_ref, g_ref, wq_ref, wkt_ref, wvt_ref, wf_ref, wft_ref, bfr_ref, bfc_ref, w4_ref,
                        triu_ref, tril_ref, kt_all_ref, vt_all_ref,
                        qa_ref, kta_ref, kt_ref, vt_ref, vtb_ref, lft_ref,
                        xr_ref, gr_ref, ga_ref, gb_ref, carry_t_ref, carry_c_ref):
    del kt_all_ref, vt_all_ref
    t = pl.program_id(1)

    @pl.when(t == 0)
    def _():
        carry_t_ref[...] = jnp.zeros_like(carry_t_ref)
        carry_c_ref[...] = jnp.zeros_like(carry_c_ref)

    tm = x_ref.shape[1]
    xn = _rmsnorm(x_ref[0], g_ref[...]).astype(BF16)
    lft = _log_sigmoid(_dot_nt(wft_ref[...], xn) + bfc_ref[...])[:N_HEADS]
    lft_ref[0] = lft
    lfc = _log_sigmoid(_dot(xn, wf_ref[...]) + bfr_ref[...])
    ct = _exact_dot(lft, triu_ref[...]) + carry_t_ref[:, :1]
    carry_t_ref[...] = jnp.broadcast_to(ct[:, -1:], carry_t_ref.shape)
    tril = tril_ref[...]
    cc = carry_c_ref[0:1, :]
    for piece in _split3(lfc):
        cc = cc + _dot(tril, piece.astype(BF16))
    carry_c_ref[...] = jnp.broadcast_to(cc[-1:, :], carry_c_ref.shape)

    q = _dot(xn, wq_ref[...]) * (HEAD_DIM ** -0.5)
    kt = _dot_nt(wkt_ref[...], xn)
    kt_ref[0, 0] = kt
    lane = lax.broadcasted_iota(jnp.int32, (tm, HEAD_DIM), 1)
    sub = lax.broadcasted_iota(jnp.int32, (HEAD_DIM, tm), 0)
    for h in range(N_HEADS):
        hs = slice(h * HEAD_DIM, (h + 1) * HEAD_DIM)
        q1, q2, q3 = _split3(cc[:, h:h + 1])
        q_extra = jnp.where(lane == 0, q1, jnp.where(lane == 1, q2, jnp.where(lane == 2, q3,
                            jnp.where(lane < 6, 1.0, 0.0))))
        qa_ref[0, h] = jnp.concatenate([q[:, hs], q_extra], axis=1).astype(BF16)
        k1, k2, k3 = _split3(ct[h:h + 1, :])
        k_extra = jnp.where(sub < 3, 1.0, jnp.where(sub == 3, -k1, jnp.where(sub == 4, -k2,
                            jnp.where(sub == 5, -k3, 0.0))))
        kta_ref[0, 2 * h * HEAD_DIM:2 * (h + 1) * HEAD_DIM, :] = jnp.concatenate(
            [kt[hs, :], k_extra], axis=0).astype(BF16)
    vt = _dot_nt(wvt_ref[...], xn)
    vt_ref[0, 0] = vt
    vtb_ref[0] = vt.astype(BF16)
    d = xr_ref.shape[-1]
    xr_ref[0] = _dot(xn, w4_ref[:, 0 * d:1 * d])
    gr_ref[0] = _dot(xn, w4_ref[:, 1 * d:2 * d])
    ga_ref[0] = _dot(xn, w4_ref[:, 2 * d:3 * d])
    gb_ref[0] = _dot(xn, w4_ref[:, 3 * d:4 * d])


def _proj_prompt(x, g, wq, wkt, wvt, wf, bf_row, w4, layer, kt_all, vt_all):
    b, t, d = x.shape
    tm = min(PROJ_TILE, t)
    nt = t // tm
    ri = lax.broadcasted_iota(jnp.int32, (tm, tm), 0)
    ci = lax.broadcasted_iota(jnp.int32, (tm, tm), 1)
    triu = (ri <= ci).astype(BF16)
    tril = (ri >= ci).astype(BF16)
    a = ATTN_WIDTH
    seq = lambda w: pl.BlockSpec((1, tm, w), lambda i, j: (i, j, 0))
    tr = lambda r: pl.BlockSpec((1, r, tm), lambda i, j: (i, 0, j))
    out_shape = [
        jax.ShapeDtypeStruct((b, N_HEADS, t, 2 * HEAD_DIM), BF16),
        jax.ShapeDtypeStruct((b, 2 * a, t), BF16),
        jax.ShapeDtypeStruct(kt_all.shape, F32), jax.ShapeDtypeStruct(vt_all.shape, F32),
        jax.ShapeDtypeStruct((b, a, t), BF16),
        jax.ShapeDtypeStruct((b, N_HEADS, t), F32),
    ] + [jax.ShapeDtypeStruct((b, t, d), F32)] * 4
    stacked = pl.BlockSpec((1, 1, a, tm), lambda i, j: (layer, i, 0, j))
    out_specs = [
        pl.BlockSpec((1, N_HEADS, tm, 2 * HEAD_DIM), lambda i, j: (i, 0, j, 0)),
        tr(2 * a), stacked, stacked, tr(a), tr(N_HEADS),
        seq(d), seq(d), seq(d), seq(d),
    ]
    args = (x, g, wq, wkt, wvt, wf, wf.T, bf_row, bf_row.reshape(LANES, 1), w4, triu, tril)
    return pl.pallas_call(
        _proj_prompt_kernel,
        grid=(b, nt),
        in_specs=[seq(d)] + [_full(v.shape) for v in args[1:]] + [pl.BlockSpec(memory_space=pl.ANY)] * 2,
        out_specs=out_specs,
        out_shape=out_shape,
        input_output_aliases={len(args): 2, len(args) + 1: 3},
        scratch_shapes=[pltpu.VMEM((N_HEADS, LANES), F32), pltpu.VMEM((N_HEADS, LANES), F32)],
        compiler_params=_cparams("arbitrary", "arbitrary"),
        name="proj_prompt",
    )(*args, kt_all, vt_all)


def _proj_sample_kernel(x_ref, g_ref, wq_ref, wk_ref, wv_ref, wf_ref, bf_ref, w4_ref,
                        q_ref, k_ref, v_ref, lf_ref, xr_ref, gr_ref, ga_ref, gb_ref):
    xn = _rmsnorm(x_ref[...], g_ref[...]).astype(BF16)
    q_ref[...] = _dot(xn, wq_ref[...])
    k_ref[...] = _dot(xn, wk_ref[...])
    v_ref[...] = _dot(xn, wv_ref[...])
    lf_ref[...] = _log_sigmoid(_dot(xn, wf_ref[...]) + bf_ref[...])
    d = xr_ref.shape[-1]
    xr_ref[...] = _dot(xn, w4_ref[:, 0 * d:1 * d])
    gr_ref[...] = _dot(xn, w4_ref[:, 1 * d:2 * d])
    ga_ref[...] = _dot(xn, w4_ref[:, 2 * d:3 * d])
    gb_ref[...] = _dot(xn, w4_ref[:, 3 * d:4 * d])


def _proj_sample(x, g, wq, wk, wv, wf, bf_row, w4):
    m, d = x.shape
    a = ATTN_WIDTH
    shapes = [(m, a), (m, a), (m, a), (m, LANES), (m, d), (m, d), (m, d), (m, d)]
    return pl.pallas_call(
        _proj_sample_kernel,
        grid=(1,),
        in_specs=[_full(v.shape) for v in (x, g, wq, wk, wv, wf, bf_row, w4)],
        out_specs=[_full(s) for s in shapes],
        out_shape=[jax.ShapeDtypeStruct(s, F32) for s in shapes],
        compiler_params=_cparams("arbitrary"),
        name="proj_sample",
    )(x, g, wq, wk, wv, wf, bf_row, w4)


def _attn_prompt_kernel(qi_ref, ki_ref, q_ref, kta_ref, vt_ref, o_ref, m_ref, l_ref, acc_ref):
    step = pl.program_id(1)
    qi = qi_ref[step]
    ki = ki_ref[step]
    ta = q_ref.shape[2]

    @pl.when(ki == 0)
    def _():
        m_ref[...] = jnp.full_like(m_ref, NEG_INF)
        l_ref[...] = jnp.zeros_like(l_ref)
        acc_ref[...] = jnp.zeros_like(acc_ref)

    def update(on_diagonal):
        if on_diagonal:
            mask = (lax.broadcasted_iota(jnp.int32, (ta, ta), 1)
                    <= lax.broadcasted_iota(jnp.int32, (ta, ta), 0))
        for h in range(N_HEADS):
            s = _dot(q_ref[0, h], kta_ref[0, 2 * h * HEAD_DIM:2 * (h + 1) * HEAD_DIM, :])
            if on_diagonal:
                s = jnp.where(mask, s, NEG_INF)
            m_old = m_ref[h]
            m_new = jnp.maximum(m_old, jnp.max(s, axis=-1, keepdims=True))
            p = jnp.exp(s - jnp.concatenate([m_new] * (ta // LANES), axis=1))
            alpha = jnp.exp(m_old - m_new)
            l_ref[h] = alpha * l_ref[h] + jnp.sum(p, axis=-1, keepdims=True)
            pv = _dot_nt(p.astype(BF16), vt_ref[0, h * HEAD_DIM:(h + 1) * HEAD_DIM, :])
            acc_ref[h] = alpha[:, :HEAD_DIM] * acc_ref[h] + pv
            m_ref[h] = m_new

    @pl.when(ki < qi)
    def _():
        update(False)

    @pl.when(ki == qi)
    def _():
        update(True)
        for h in range(0, N_HEADS, 2):
            pair = [acc_ref[h + j] / l_ref[h + j][:, :HEAD_DIM] for j in range(2)]
            o_ref[0, :, h * HEAD_DIM:(h + 2) * HEAD_DIM] = jnp.concatenate(pair, axis=1).astype(o_ref.dtype)


def _attn_prompt(qa, kta, vtb):
    b, _, t, _ = qa.shape
    ta = min(ATTN_TILE, t)
    nq = t // ta
    pairs = [(i, j) for i in range(nq) for j in range(i + 1)]
    qi_tab = jnp.asarray([p[0] for p in pairs], jnp.int32)
    ki_tab = jnp.asarray([p[1] for p in pairs], jnp.int32)
    a = ATTN_WIDTH
    grid_spec = pltpu.PrefetchScalarGridSpec(
        num_scalar_prefetch=2,
        grid=(b, len(pairs)),
        in_specs=[
            pl.BlockSpec((1, N_HEADS, ta, 2 * HEAD_DIM), lambda i, s, qt, kt: (i, 0, qt[s], 0)),
            pl.BlockSpec((1, 2 * a, ta), lambda i, s, qt, kt: (i, 0, kt[s])),
            pl.BlockSpec((1, a, ta), lambda i, s, qt, kt: (i, 0, kt[s])),
        ],
        out_specs=pl.BlockSpec((1, ta, a), lambda i, s, qt, kt: (i, qt[s], 0)),
        scratch_shapes=[pltpu.VMEM((N_HEADS, ta, LANES), F32), pltpu.VMEM((N_HEADS, ta, LANES), F32),
                        pltpu.VMEM((N_HEADS, ta, HEAD_DIM), F32)],
    )
    return pl.pallas_call(
        _attn_prompt_kernel,
        grid_spec=grid_spec,
        out_shape=jax.ShapeDtypeStruct((b, t, a), BF16),
        compiler_params=_cparams("arbitrary", "arbitrary"),
        name="attn_prompt",
    )(qi_tab, ki_tab, qa, kta, vtb)


def _attn_decode_kernel(n_pg, pt_ref, q_ref, kn_ref, vn_ref, lfn_ref, w_ref, *rest):
    k_refs = rest[0:n_pg]
    v_refs = rest[n_pg:2 * n_pg]
    lt_refs = rest[2 * n_pg:3 * n_pg]
    o_ref = rest[3 * n_pg]
    m_ref, l_ref, c_ref, acc_ref = rest[3 * n_pg + 1:]
    c_step = pl.program_id(1)
    width = acc_ref.shape[1]
    own = (lax.broadcasted_iota(jnp.int32, (N_HEADS, width), 1) // HEAD_DIM
           == lax.broadcasted_iota(jnp.int32, (N_HEADS, width), 0))
    q_rows = jnp.where(own, q_ref[0] * (HEAD_DIM ** -0.5), 0.0)

    @pl.when(c_step == 0)
    def _():
        lane = lax.broadcasted_iota(jnp.int32, (N_HEADS, LANES), 1)
        s0 = jnp.sum(q_rows * kn_ref[0], axis=1, keepdims=True)
        m_ref[...] = jnp.broadcast_to(s0, m_ref.shape)
        l_ref[...] = jnp.where(lane == 0, 1.0, 0.0)
        acc_ref[...] = jnp.broadcast_to(vn_ref[0], acc_ref.shape)
        c_ref[...] = jnp.broadcast_to(lfn_ref[0], c_ref.shape)

    qb = q_rows.astype(BF16)
    carry = c_ref[...]
    s = []
    for i in range(n_pg):
        r = _exact_dot(lt_refs[i][0], w_ref[...])
        s.append(_dot(qb, k_refs[i][...].astype(BF16)) + (r[:, :LANES] + carry))
        carry = carry + r[:, LANES:]
    c_ref[...] = carry

    m_tile = s[0]
    for i in range(1, n_pg):
        m_tile = jnp.maximum(m_tile, s[i])
    m_old = m_ref[...]
    m_new = jnp.maximum(m_old, jnp.max(m_tile, axis=1, keepdims=True))
    alpha = jnp.exp(m_old - m_new)
    l_new = alpha * l_ref[...]
    pv = jnp.zeros(acc_ref.shape, F32)
    for i in range(n_pg):
        p = jnp.exp(s[i] - m_new)
        l_new = l_new + p
        pv = pv + _dot_nt(p.astype(BF16), v_refs[i][...].astype(BF16))
    m_ref[...] = m_new
    l_ref[...] = l_new
    acc_ref[...] = jnp.concatenate([alpha] * (width // LANES), axis=1) * acc_ref[...] + pv

    @pl.when(c_step == pl.num_programs(1) - 1)
    def _():
        den = jnp.sum(l_ref[...], axis=1, keepdims=True)
        o_ref[0] = jnp.sum(jnp.where(own, acc_ref[...] / den, 0.0), axis=0, keepdims=True)


def _attn_decode(layer, page_table, q, kn, vn, lfn_col, kt_pool, vt_pool, lt_pool, n_pool):
    db, n_pages = page_table.shape
    n_pg = min(DECODE_PAGES, n_pages)
    n_chunks = n_pages // n_pg
    page = kt_pool.shape[-1]
    base = layer * n_pool
    rowi = lax.broadcasted_iota(jnp.int32, (page, 2 * page), 0)
    coli = lax.broadcasted_iota(jnp.int32, (page, 2 * page), 1)
    w = ((rowi > coli) | (coli >= page)).astype(BF16)

    def page_of(slot):
        def index(b, c, pt):
            return base + pt[b * n_pages + (n_pages - 1 - (c * n_pg + slot))]
        return index

    per_seq = lambda r, c: pl.BlockSpec((1, r, c), lambda b, ch, pt: (b, 0, 0))

    def kv_specs():
        return [pl.BlockSpec((ATTN_WIDTH, page), (lambda b, ch, pt, f=page_of(i): (f(b, ch, pt), 0)))
                for i in range(n_pg)]

    lt_specs = [pl.BlockSpec((1, N_HEADS, page), (lambda b, ch, pt, f=page_of(i): (f(b, ch, pt), 0, 0)))
                for i in range(n_pg)]
    grid_spec = pltpu.PrefetchScalarGridSpec(
        num_scalar_prefetch=1,
        grid=(db, n_chunks),
        in_specs=[per_seq(1, ATTN_WIDTH), per_seq(1, ATTN_WIDTH), per_seq(1, ATTN_WIDTH), per_seq(N_HEADS, 1),
                  pl.BlockSpec(w.shape, lambda b, ch, pt: (0, 0))] + kv_specs() + kv_specs() + lt_specs,
        out_specs=per_seq(1, ATTN_WIDTH),
        scratch_shapes=[pltpu.VMEM((N_HEADS, LANES), F32), pltpu.VMEM((N_HEADS, LANES), F32),
                        pltpu.VMEM((N_HEADS, LANES), F32), pltpu.VMEM((N_HEADS, ATTN_WIDTH), F32)],
    )
    return pl.pallas_call(
        functools.partial(_attn_decode_kernel, n_pg),
        grid_spec=grid_spec,
        out_shape=jax.ShapeDtypeStruct((db, 1, ATTN_WIDTH), F32),
        compiler_params=_cparams("arbitrary", "arbitrary"),
        name="attn_decode",
    )(page_table.reshape(-1), q, kn, vn, lfn_col, w,
      *([kt_pool] * n_pg), *([vt_pool] * n_pg), *([lt_pool] * n_pg))


def _lru_gates(xc, wrg_ref, brg_ref, wig_ref, big_ref, lam_ref, store):
    xcb = xc.astype(BF16)
    blk = xc.shape[-1] // N_LRU_BLOCKS
    for n in range(N_LRU_BLOCKS):
        cs = slice(n * blk, (n + 1) * blk)
        r = jax.nn.sigmoid(_dot(xcb[:, cs], wrg_ref[n]) + brg_ref[:, cs])
        i = jax.nn.sigmoid(_dot(xcb[:, cs], wig_ref[n]) + big_ref[:, cs])
        log_a = -LRU_C * r * _softplus(-lam_ref[:, cs])
        a = jnp.exp(log_a)
        th = jnp.tanh(log_a)
        mult = jnp.sqrt(-2.0 * th / (1.0 - th))
        store(cs, a, mult * (i * xc[:, cs]))


def _lru_prompt_kernel(xr_ref, gr_ref, cp_ref, h0_ref, cw_ref, cb_ref, wrg_ref, brg_ref, wig_ref,
                       big_ref, lam_ref, y_ref, tail_ref, hl_ref, buf_ref, a_ref, b_ref, h_ref, hc_ref):
    t = pl.program_id(1)
    tt = xr_ref.shape[1]
    halo = SUBLANES

    @pl.when(t == 0)
    def _():
        buf_ref[0:halo, :] = cp_ref[0]
        hc_ref[...] = h0_ref[0]

    x = xr_ref[0]
    buf_ref[halo:, :] = x
    taps = [buf_ref[pl.ds(halo - (CONV_WIDTH - 1) + j, tt), :] * cw_ref[j:j + 1, :]
            for j in range(CONV_WIDTH - 1)]
    taps.append(x * cw_ref[CONV_WIDTH - 1:CONV_WIDTH, :])
    acc = taps[0]
    for term in taps[1:]:
        acc = acc + term
    xc = cb_ref[...] + acc

    def store(cs, a, b):
        a_ref[:, cs] = a
        b_ref[:, cs] = b

    _lru_gates(xc, wrg_ref, brg_ref, wig_ref, big_ref, lam_ref, store)

    grouped = (tt // SUBLANES, SUBLANES, a_ref.shape[1])
    a = a_ref[...].reshape(grouped)
    b = b_ref[...].reshape(grouped)
    in_group = lax.broadcasted_iota(jnp.int32, grouped, 1)
    shift = 1
    while shift < SUBLANES:
        a_prev = pltpu.roll(a, shift, axis=1)
        b_prev = pltpu.roll(b, shift, axis=1)
        take = in_group >= shift
        b = jnp.where(take, a * b_prev + b, b)
        a = jnp.where(take, a * a_prev, a)
        shift *= 2
    a_ref[...] = a.reshape(a_ref.shape)
    b_ref[...] = b.reshape(b_ref.shape)
    h_in = jnp.broadcast_to(hc_ref[...], (SUBLANES, a_ref.shape[1]))
    for g in range(tt // SUBLANES):
        rows = slice(g * SUBLANES, (g + 1) * SUBLANES)
        h_grp = a_ref[rows, :] * h_in + b_ref[rows, :]
        h_ref[rows, :] = h_grp
        h_in = jnp.broadcast_to(h_grp[SUBLANES - 1:, :], h_grp.shape)
    h_last = h_in[:1, :]
    hc_ref[...] = h_last
    hl_ref[0] = h_last
    y_ref[0] = (h_ref[...] * jax.nn.gelu(gr_ref[0])).astype(y_ref.dtype)
    tail = buf_ref[tt:tt + halo, :]
    tail_ref[0] = tail
    buf_ref[0:halo, :] = tail


def _lru_prompt(xr, gr, conv_prev8, h0, cw, cb, wrg, brg, wig, big, lam):
    b, t, d = xr.shape
    tt = min(LRU_TILE, t)
    seq = pl.BlockSpec((1, tt, d), lambda i, j: (i, j, 0))
    per_b = lambda r: pl.BlockSpec((1, r, d), lambda i, j: (i, 0, 0))
    return pl.pallas_call(
        _lru_prompt_kernel,
        grid=(b, t // tt),
        in_specs=[seq, seq, per_b(SUBLANES), per_b(1)] + [_full(v.shape) for v in (cw, cb, wrg, brg, wig, big, lam)],
        out_specs=[seq, per_b(SUBLANES), per_b(1)],
        out_shape=[jax.ShapeDtypeStruct((b, t, d), BF16), jax.ShapeDtypeStruct((b, SUBLANES, d), F32),
                   jax.ShapeDtypeStruct((b, 1, d), F32)],
        scratch_shapes=[pltpu.VMEM((tt + SUBLANES, d), F32), pltpu.VMEM((tt, d), F32),
                        pltpu.VMEM((tt, d), F32), pltpu.VMEM((tt, d), F32), pltpu.VMEM((1, d), F32)],
        compiler_params=_cparams("arbitrary", "arbitrary"),
        name="lru_prompt",
    )(xr, gr, conv_prev8, h0, cw, cb, wrg, brg, wig, big, lam)


def _lru_sample_kernel(xr_ref, gr_ref, c0_ref, c1_ref, c2_ref, h0_ref, cw_ref, cb_ref, wrg_ref, brg_ref,
                       wig_ref, big_ref, lam_ref, y_ref, hn_ref):
    x = xr_ref[...]
    acc = c0_ref[...] * cw_ref[0:1, :]
    acc = acc + c1_ref[...] * cw_ref[1:2, :]
    acc = acc + c2_ref[...] * cw_ref[2:3, :]
    acc = acc + x * cw_ref[3:4, :]
    xc = cb_ref[...] + acc

    def store(cs, a, b):
        hn_ref[:, cs] = a * h0_ref[:, cs] + b

    _lru_gates(xc, wrg_ref, brg_ref, wig_ref, big_ref, lam_ref, store)
    y_ref[...] = (hn_ref[...] * jax.nn.gelu(gr_ref[...])).astype(y_ref.dtype)


def _lru_sample(xr, gr, c0, c1, c2, h0, cw, cb, wrg, brg, wig, big, lam):
    m, d = xr.shape
    args = (xr, gr, c0, c1, c2, h0, cw, cb, wrg, brg, wig, big, lam)
    return pl.pallas_call(
        _lru_sample_kernel,
        grid=(1,),
        in_specs=[_full(v.shape) for v in args],
        out_specs=[_full((m, d)), _full((m, d))],
        out_shape=[jax.ShapeDtypeStruct((m, d), BF16), jax.ShapeDtypeStruct((m, d), F32)],
        compiler_params=_cparams("arbitrary"),
        name="lru_sample",
    )(*args)


def _merge_kernel(attn_ref, y_ref, ga_ref, gb_ref, x_ref, wa_ref, wb_ref, wo_ref, g_ref, x1_ref, xn_ref):
    merged = (jax.nn.sigmoid(ga_ref[...]) * _dot(attn_ref[...].astype(BF16), wa_ref[...])
              + jax.nn.sigmoid(gb_ref[...]) * _dot(y_ref[...], wb_ref[...]))
    x1 = x_ref[...] + _dot(merged.astype(BF16), wo_ref[...])
    x1_ref[...] = x1
    xn_ref[...] = _rmsnorm(x1, g_ref[...]).astype(xn_ref.dtype)


def _merge(attn, y, ga, gb, x, wa, wb, wo, g):
    m, d = x.shape
    tm = min(ROW_TILE, m)
    row = lambda w: pl.BlockSpec((tm, w), lambda i: (i, 0))
    return pl.pallas_call(
        _merge_kernel,
        grid=(m // tm,),
        in_specs=[row(attn.shape[1]), row(d), row(d), row(d), row(d)] + [_full(v.shape) for v in (wa, wb, wo, g)],
        out_specs=[row(d), row(d)],
        out_shape=[jax.ShapeDtypeStruct((m, d), F32), jax.ShapeDtypeStruct((m, d), BF16)],
        compiler_params=_cparams("arbitrary"),
        name="merge",
    )(attn, y, ga, gb, x, wa, wb, wo, g)


def _top2(logits):
    lane = lax.broadcasted_iota(jnp.int32, logits.shape, 1)
    logits = jnp.where(lane < N_EXPERTS, logits, -jnp.inf)
    m1 = jnp.max(logits, axis=-1, keepdims=True)
    i1 = jnp.min(jnp.where(logits == m1, lane, LANES), axis=-1, keepdims=True)
    rest = jnp.where(lane == i1, -jnp.inf, logits)
    m2 = jnp.max(rest, axis=-1, keepdims=True)
    i2 = jnp.min(jnp.where(rest == m2, lane, LANES), axis=-1, keepdims=True)
    e2 = jnp.exp(m2 - m1)
    den = 1.0 + e2
    return lane, i1, i2, 1.0 / den, e2 / den


def _to_row_tiles(dst_ref, x):
    rows = x.shape[0]
    for s in range(x.shape[1] // LANES):
        dst_ref[pl.ds(s, rows, stride=SUBLANES), :] = x[:, s * LANES:(s + 1) * LANES]


def _from_row_tiles(src_ref, rows, s):
    return src_ref[pl.ds(s, rows, stride=SUBLANES), :]


def _merge_route_kernel(attn_ref, y_ref, ga_ref, gb_ref, x_ref, wa_ref, wb_ref, wo_ref, g_ref, wr_ref, tri_ref,
                        x1_ref, xrt_ref, meta_ref, metat_ref, cnt_ref, carry_ref):
    i = pl.program_id(0)

    @pl.when(i == 0)
    def _():
        carry_ref[...] = jnp.zeros_like(carry_ref)

    merged = (jax.nn.sigmoid(ga_ref[...]) * _dot(attn_ref[...].astype(BF16), wa_ref[...])
              + jax.nn.sigmoid(gb_ref[...]) * _dot(y_ref[...], wb_ref[...]))
    x1 = x_ref[...] + _dot(merged.astype(BF16), wo_ref[...])
    x1_ref[...] = x1
    xn = _rmsnorm(x1, g_ref[...])
    _to_row_tiles(xrt_ref, xn)
    lane, i1, i2, w1, w2 = _top2(_dot(xn.astype(BF16), wr_ref[...]))
    oh1 = lane == i1
    oh2 = lane == i2
    tri = tri_ref[...]
    before1 = _dot(tri, oh1.astype(BF16))
    before2 = _dot(tri, oh2.astype(BF16))
    cnt1 = jnp.sum(oh1.astype(F32), axis=0, keepdims=True)
    cnt2 = jnp.sum(oh2.astype(F32), axis=0, keepdims=True)
    carry = carry_ref[0:1, :]
    rank1 = jnp.sum(jnp.where(oh1, carry + before1, 0.0), axis=-1, keepdims=True)
    rank2 = jnp.sum(jnp.where(oh2, carry + cnt1 + before2, 0.0), axis=-1, keepdims=True)
    total = carry + cnt1 + cnt2
    carry_ref[...] = jnp.broadcast_to(total, carry_ref.shape)
    cnt_ref[...] = jnp.broadcast_to(total, cnt_ref.shape)
    cols = (i1.astype(F32), i2.astype(F32), rank1, rank2, w1, w2)
    meta = jnp.zeros(meta_ref.shape, F32)
    for k, v in enumerate(cols):
        meta = jnp.where(lane == k, v, meta)
    meta_ref[...] = meta
    sel = (lax.broadcasted_iota(jnp.int32, (SUBLANES, LANES), 0)
           == lax.broadcasted_iota(jnp.int32, (SUBLANES, LANES), 1)).astype(BF16)
    meta_t = jnp.zeros(metat_ref.shape, F32)
    for piece in _split3(meta):
        meta_t = meta_t + _dot_nt(sel, piece.astype(BF16))
    metat_ref[...] = meta_t


def _merge_route(attn, y, ga, gb, x, wa, wb, wo, g, wr_pad):
    m, d = x.shape
    tm = min(ROW_TILE, m)
    tri = (lax.broadcasted_iota(jnp.int32, (tm, tm), 0)
           > lax.broadcasted_iota(jnp.int32, (tm, tm), 1)).astype(BF16)
    row = lambda w: pl.BlockSpec((tm, w), lambda i: (i, 0))
    return pl.pallas_call(
        _merge_route_kernel,
        grid=(m // tm,),
        in_specs=[row(attn.shape[1]), row(d), row(d), row(d), row(d)]
        + [_full(v.shape) for v in (wa, wb, wo, g, wr_pad, tri)],
        out_specs=[row(d), pl.BlockSpec((tm * SUBLANES, LANES), lambda i: (i, 0)), row(LANES),
                   pl.BlockSpec((SUBLANES, tm), lambda i: (0, i)),
                   pl.BlockSpec((SUBLANES, LANES), lambda i: (0, 0))],
        out_shape=[jax.ShapeDtypeStruct((m, d), F32), jax.ShapeDtypeStruct((m * SUBLANES, LANES), F32),
                   jax.ShapeDtypeStruct((m, LANES), F32), jax.ShapeDtypeStruct((SUBLANES, m), F32),
                   jax.ShapeDtypeStruct((SUBLANES, LANES), F32)],
        scratch_shapes=[pltpu.VMEM((SUBLANES, LANES), F32)],
        compiler_params=_cparams("arbitrary"),
        name="merge_route",
    )(attn, y, ga, gb, x, wa, wb, wo, g, wr_pad, tri)


SC_CORES = 2
SC_SUBCORES = 16
SC_CHUNK_ROWS = 32


def _sc_mesh():
    return plsc.VectorSubcoreMesh(core_axis_name="c", subcore_axis_name="s",
                                  num_cores=SC_CORES, num_subcores=SC_SUBCORES)


def _sc_worker_base(per_worker):
    return (lax.axis_index("s") * SC_CORES + lax.axis_index("c")) * per_worker


def _sc_scatter_rows(rows, dest1, dest2, n_out):
    n = rows.shape[0]
    per_worker = n // (SC_CORES * SC_SUBCORES)
    chunk = SC_CHUNK_ROWS

    def body(rows_hbm, d1_hbm, d2_hbm, out_hbm, i1_v, i2_v, rows_v, sem):
        base = _sc_worker_base(per_worker)

        @pl.loop(0, per_worker // chunk)
        def _(j):
            src = pl.ds(base + pl.multiple_of(j * chunk, chunk), chunk)
            pltpu.sync_copy(d1_hbm.at[src], i1_v)
            pltpu.sync_copy(d2_hbm.at[src], i2_v)
            pltpu.sync_copy(rows_hbm.at[src], rows_v)
            first = pltpu.async_copy(rows_v, out_hbm.at[i1_v], sem)
            second = pltpu.async_copy(rows_v, out_hbm.at[i2_v], sem)
            first.wait()
            second.wait()

    return pl.kernel(
        body,
        out_type=jax.ShapeDtypeStruct((n_out,) + rows.shape[1:], rows.dtype),
        mesh=_sc_mesh(),
        scratch_types=[pltpu.VMEM((chunk,), jnp.int32), pltpu.VMEM((chunk,), jnp.int32),
                       pltpu.VMEM((chunk,) + rows.shape[1:], rows.dtype), pltpu.SemaphoreType.DMA],
        name="sc_row_scatter",
    )(rows, dest1, dest2)


def _moe_ffn_kernel(te_ref, tv_ref, nu_ref, xs_ref, wg_ref, wu_ref, wd_ref, ys_ref, xb_ref, acc_ref):
    i = pl.program_id(0)
    f = pl.program_id(1)
    rows, d = xb_ref.shape

    @pl.when(i < nu_ref[0])
    def _():
        @pl.when(f == 0)
        def _():
            live = lax.broadcasted_iota(jnp.int32, (rows, LANES), 0) < tv_ref[i]
            for s in range(d // LANES):
                slab = jnp.where(live, _from_row_tiles(xs_ref, rows, s), 0.0)
                xb_ref[:, s * LANES:(s + 1) * LANES] = slab.astype(BF16)
            acc_ref[...] = jnp.zeros_like(acc_ref)

        xb = xb_ref[...]
        hid = jax.nn.silu(_dot(xb, wg_ref[0])) * _dot(xb, wu_ref[0])
        acc_ref[...] += _dot(hid.astype(BF16), wd_ref[0])

        @pl.when(f == pl.num_programs(1) - 1)
        def _():
            _to_row_tiles(ys_ref, acc_ref[...])

    @pl.when((i >= nu_ref[0]) & (f == pl.num_programs(1) - 1))
    def _():
        ys_ref[...] = jnp.zeros_like(ys_ref)


def _moe_ffn(tile_expert, tile_valid, n_used, xs, wg, wu, wd):
    n_e, d, ff = wg.shape
    tm = MOE_ROW_TILE
    tf = MOE_COL_TILE
    nf = ff // tf
    n_tiles = tile_expert.shape[0]

    def tile(i, f, te, tv, nu):
        return (i, 0)

    def col(i, f, nu):
        return jnp.where(i < nu[0], f, nf - 1)

    grid_spec = pltpu.PrefetchScalarGridSpec(
        num_scalar_prefetch=3,
        grid=(n_tiles, nf),
        in_specs=[pl.BlockSpec((tm * SUBLANES, LANES), tile),
                  pl.BlockSpec((1, d, tf), lambda i, f, te, tv, nu: (te[i], 0, col(i, f, nu))),
                  pl.BlockSpec((1, d, tf), lambda i, f, te, tv, nu: (te[i], 0, col(i, f, nu))),
                  pl.BlockSpec((1, tf, d), lambda i, f, te, tv, nu: (te[i], col(i, f, nu), 0))],
        out_specs=pl.BlockSpec((tm * SUBLANES, LANES), tile),
        scratch_shapes=[pltpu.VMEM((tm, d), BF16), pltpu.VMEM((tm, d), F32)],
    )
    return pl.pallas_call(
        _moe_ffn_kernel,
        grid_spec=grid_spec,
        out_shape=jax.ShapeDtypeStruct(xs.shape, F32),
        compiler_params=_cparams("arbitrary", "arbitrary"),
        name="moe_ffn",
    )(tile_expert, tile_valid, n_used, xs, wg, wu, wd)


def _sc_gather_rows(table, idx):
    n = idx.shape[0]
    per_worker = n // (SC_CORES * SC_SUBCORES)
    chunk = SC_CHUNK_ROWS
    mesh = _sc_mesh()

    def body(table_hbm, idx_hbm, out_hbm, idx_v, rows_v, sem):
        base = _sc_worker_base(per_worker)
        pltpu.sync_copy(idx_hbm.at[pl.ds(base, per_worker)], idx_v)

        @pl.loop(0, per_worker // chunk)
        def _(j):
            off = pl.multiple_of(j * chunk, chunk)
            pltpu.async_copy(table_hbm.at[idx_v.at[pl.ds(off, chunk)]], rows_v, sem).wait()
            pltpu.sync_copy(rows_v, out_hbm.at[pl.ds(base + off, chunk)])

    return pl.kernel(
        body,
        out_type=jax.ShapeDtypeStruct((n,) + table.shape[1:], table.dtype),
        mesh=mesh,
        scratch_types=[pltpu.VMEM((per_worker,), jnp.int32), pltpu.VMEM((chunk,) + table.shape[1:], table.dtype),
                       pltpu.SemaphoreType.DMA],
        name="sc_row_gather",
    )(table, idx)


def _combine_dense_kernel(x1_ref, meta_ref, g1_ref, g2_ref, o_ref):
    tm, d = x1_ref.shape
    w1 = meta_ref[:, 4:5]
    w2 = meta_ref[:, 5:6]
    for s in range(d // LANES):
        cs = slice(s * LANES, (s + 1) * LANES)
        o_ref[:, cs] = (x1_ref[:, cs] + w1 * _from_row_tiles(g1_ref, tm, s)
                        + w2 * _from_row_tiles(g2_ref, tm, s))


def _combine_dense(x1, meta, gathered):
    m, d = x1.shape
    tm = min(ROW_TILE, m)
    nt = m // tm
    row = lambda w: pl.BlockSpec((tm, w), lambda i: (i, 0))
    return pl.pallas_call(
        _combine_dense_kernel,
        grid=(nt,),
        in_specs=[row(d), row(LANES), pl.BlockSpec((tm * SUBLANES, LANES), lambda i: (i, 0)),
                  pl.BlockSpec((tm * SUBLANES, LANES), lambda i: (nt + i, 0))],
        out_specs=row(d),
        out_shape=jax.ShapeDtypeStruct((m, d), F32),
        compiler_params=_cparams("arbitrary"),
        name="moe_combine",
    )(x1, meta, gathered, gathered)


def _moe_sorted(x1, x_rt, meta, meta_t, counts, wg, wu, wd):
    m = x1.shape[0]
    tm = MOE_ROW_TILE
    n_tiles = (2 * m) // tm + N_EXPERTS
    cnt = counts[0, :N_EXPERTS].astype(jnp.int32)
    padded = ((cnt + tm - 1) // tm) * tm
    ends = jnp.cumsum(padded)
    starts = ends - padded
    experts = jnp.arange(N_EXPERTS, dtype=jnp.int32)[:, None]
    fields = meta_t.astype(jnp.int32)

    def dest(expert_row, rank_row):
        return jnp.sum(jnp.where(fields[expert_row][None, :] == experts, starts[:, None], 0), axis=0) + fields[rank_row]

    dest1 = dest(0, 2)
    dest2 = dest(1, 3)
    n_used = (ends[-1] // tm).reshape(1)
    tile_start = jnp.minimum(jnp.arange(n_tiles, dtype=jnp.int32), n_used - 1) * tm
    tile_expert = jnp.sum(ends[None, :] <= tile_start[:, None], axis=1).astype(jnp.int32)
    group_end = jnp.sum(jnp.where(tile_expert[:, None] == experts.T, (starts + cnt)[None, :], 0), axis=1)
    tile_valid = jnp.clip(group_end - tile_start, 0, tm).astype(jnp.int32)
    xs = _sc_scatter_rows(x_rt.reshape(-1, SUBLANES, LANES), dest1, dest2, n_tiles * tm).reshape(-1, LANES)
    ys = _moe_ffn(tile_expert, tile_valid, n_used.astype(jnp.int32), xs, wg, wu, wd)
    gathered = _sc_gather_rows(ys.reshape(-1, SUBLANES, LANES), jnp.concatenate([dest1, dest2]))
    return _combine_dense(x1, meta, gathered.reshape(-1, LANES))


def _ffn_kernel(xn_ref, x1_ref, wg_ref, wu_ref, wd_ref, o_ref, acc_ref):
    f = pl.program_id(1)

    @pl.when(f == 0)
    def _():
        acc_ref[...] = jnp.zeros_like(acc_ref)

    xn = xn_ref[...]
    hid = jax.nn.silu(_dot(xn, wg_ref[...].astype(BF16))) * _dot(xn, wu_ref[...].astype(BF16))
    acc_ref[...] += _dot(hid.astype(BF16), wd_ref[...].astype(BF16))

    @pl.when(f == pl.num_programs(1) - 1)
    def _():
        o_ref[...] = x1_ref[...] + acc_ref[...]


def _ffn(xn, x1, wg, wu, wd):
    m, d = x1.shape
    ff = wg.shape[1]
    tm = min(FFN_ROW_TILE, m)
    tf = FFN_COL_TILE
    row = pl.BlockSpec((tm, d), lambda i, f: (i, 0))
    return pl.pallas_call(
        _ffn_kernel,
        grid=(m // tm, ff // tf),
        in_specs=[row, row, pl.BlockSpec((d, tf), lambda i, f: (0, f)), pl.BlockSpec((d, tf), lambda i, f: (0, f)),
                  pl.BlockSpec((tf, d), lambda i, f: (f, 0))],
        out_specs=row,
        out_shape=jax.ShapeDtypeStruct((m, d), F32),
        scratch_shapes=[pltpu.VMEM((tm, d), F32)],
        compiler_params=_cparams("arbitrary", "arbitrary"),
        name="ffn_dense",
    )(xn, x1, wg, wu, wd)


def _router_kernel(xn_ref, wr_ref, gates_ref):
    lane, i1, i2, w1, w2 = _top2(_dot(xn_ref[...], wr_ref[...]))
    gates_ref[...] = jnp.where(lane == i1, w1, 0.0) + jnp.where(lane == i2, w2, 0.0)


def _router(xn, wr_pad):
    m, d = xn.shape
    tm = min(ROW_TILE, m)
    return pl.pallas_call(
        _router_kernel,
        grid=(m // tm,),
        in_specs=[pl.BlockSpec((tm, d), lambda i: (i, 0)), _full(wr_pad.shape)],
        out_specs=pl.BlockSpec((tm, LANES), lambda i: (i, 0)),
        out_shape=jax.ShapeDtypeStruct((m, LANES), F32),
        compiler_params=_cparams("arbitrary"),
        name="router",
    )(xn, wr_pad)


def _moe_kernel(xn_ref, x1_ref, gates_ref, wg_ref, wu_ref, wd_ref, o_ref, acc_ref):
    e = pl.program_id(1)
    f = pl.program_id(2)

    @pl.when((e == 0) & (f == 0))
    def _():
        acc_ref[...] = jnp.zeros_like(acc_ref)

    xn = xn_ref[...]
    hid = jax.nn.silu(_dot(xn, wg_ref[0])) * _dot(xn, wu_ref[0])
    gates = gates_ref[...]
    lane = lax.broadcasted_iota(jnp.int32, gates.shape, 1)
    gate = jnp.sum(jnp.where(lane == e, gates, 0.0), axis=-1, keepdims=True)
    acc_ref[...] += gate * _dot(hid.astype(BF16), wd_ref[0])

    @pl.when((e == pl.num_programs(1) - 1) & (f == pl.num_programs(2) - 1))
    def _():
        o_ref[...] = x1_ref[...] + acc_ref[...]


def _moe(xn, x1, gates, wg, wu, wd):
    m, d = x1.shape
    n_e, _, ff = wg.shape
    tm = min(FFN_ROW_TILE, m)
    tf = MOE_COL_TILE
    row = lambda w: pl.BlockSpec((tm, w), lambda i, e, f: (i, 0))
    return pl.pallas_call(
        _moe_kernel,
        grid=(m // tm, n_e, ff // tf),
        in_specs=[row(d), row(d), row(LANES),
                  pl.BlockSpec((1, d, tf), lambda i, e, f: (e, 0, f)),
                  pl.BlockSpec((1, d, tf), lambda i, e, f: (e, 0, f)),
                  pl.BlockSpec((1, tf, d), lambda i, e, f: (e, f, 0))],
        out_specs=row(d),
        out_shape=jax.ShapeDtypeStruct((m, d), F32),
        scratch_shapes=[pltpu.VMEM((tm, d), F32)],
        compiler_params=_cparams("arbitrary", "arbitrary", "arbitrary"),
        name="moe_dense",
    )(xn, x1, gates, wg, wu, wd)


def _ple_kernel(final, x_ref, p_ref, g_ref, wgate_ref, wproj_ref, gf_ref, o_ref):
    x = x_ref[...]
    gate = jax.nn.sigmoid(_dot(_rmsnorm(x, g_ref[...]).astype(BF16), wgate_ref[...]))
    x3 = x + gate * _dot(p_ref[0].astype(BF16), wproj_ref[...])
    o_ref[...] = _rmsnorm(x3, gf_ref[...]) if final else x3


def _ple(x, p_all, layer, g, wgate, wproj, g_final, final):
    m, d = x.shape
    tm = min(ROW_TILE, m)
    row = lambda w: pl.BlockSpec((tm, w), lambda i: (i, 0))
    return pl.pallas_call(
        functools.partial(_ple_kernel, final),
        grid=(m // tm,),
        in_specs=[row(d), pl.BlockSpec((1, tm, p_all.shape[2]), lambda i: (layer, i, 0))]
        + [_full(v.shape) for v in (g, wgate, wproj, g_final)],
        out_specs=row(d),
        out_shape=jax.ShapeDtypeStruct((m, d), F32),
        compiler_params=_cparams("arbitrary"),
        name="ple",
    )(x, p_all, g, wgate, wproj, g_final)


def _row(v):
    return v.reshape(1, -1)


def kernel(x_prompt, x_sample, p_prompt, p_sample, cache_k, cache_v, cache_logf, state_conv, state_h, page_table, g_mix, w_in, b_f, conv_w, conv_b, w_rg, b_rg, w_ig, b_ig, lru_lambda, w_a_up, w_b_up, w_out, g_ffn, dense_wg, dense_wu, dense_wd, moe_router, moe_wg, moe_wu, moe_wd, g_ple, w_ple_gate, w_ple_proj, g_final):
    depth = g_mix.shape[0]
    b, t, d = x_prompt.shape
    db = x_sample.shape[0]
    n_pool, page = cache_k.shape[1], cache_k.shape[2]
    a = ATTN_WIDTH
    kt_pool = jnp.transpose(cache_k, (0, 1, 3, 4, 2)).reshape(depth * n_pool * ATTN_WIDTH, page)
    vt_pool = jnp.transpose(cache_v, (0, 1, 3, 4, 2)).reshape(depth * n_pool * ATTN_WIDTH, page)
    lt_pool = jnp.transpose(cache_logf, (0, 1, 3, 2)).reshape(depth * n_pool, N_HEADS, page)

    xp = x_prompt
    xs = x_sample.reshape(db, d)
    m = b * t
    pp_all = p_prompt.reshape(depth, m, -1)
    ps_all = p_sample.reshape(depth, db, -1)
    kt_all = jnp.zeros((depth, b, a, t), F32)
    vt_all = jnp.zeros((depth, b, a, t), F32)
    outs = {name: [] for name in ("lp", "cp", "hp", "ks", "vs", "ls", "cs", "hs")}
    for l in range(depth):
        wl = w_in[l]
        wq = wl[:, 0:a].astype(BF16)
        wk = wl[:, a:2 * a].astype(BF16)
        wv = wl[:, 2 * a:3 * a].astype(BF16)
        wf = jnp.pad(wl[:, 3 * a:3 * a + N_HEADS], ((0, 0), (0, LANES - N_HEADS))).astype(BF16)
        w4 = wl[:, 3 * a + N_HEADS:].astype(BF16)
        bf_row = jnp.pad(b_f[l], (0, LANES - N_HEADS)).reshape(1, LANES)
        g1 = _row(g_mix[l])
        lru_w = (conv_w[l], _row(conv_b[l]), w_rg[l].astype(BF16), _row(b_rg[l]), w_ig[l].astype(BF16),
                 _row(b_ig[l]), _row(lru_lambda[l]))
        wa = w_a_up[l].astype(BF16)
        wb = w_b_up[l].astype(BF16)
        wo = w_out[l].astype(BF16)
        g2 = _row(g_ffn[l])
        g3 = _row(g_ple[l])
        wgate = w_ple_gate[l].astype(BF16)
        wproj = w_ple_proj[l].astype(BF16)
        gf = _row(g_final)
        final = l == depth - 1
        mi = l // 2
        if l % 2 == 0:
            ffn_w = (dense_wg[mi], dense_wu[mi], dense_wd[mi])
        else:
            wr_pad = jnp.pad(moe_router[mi], ((0, 0), (0, LANES - N_EXPERTS))).astype(BF16)
            ffn_w = (moe_wg[mi].astype(BF16), moe_wu[mi].astype(BF16), moe_wd[mi].astype(BF16))

        qa, kta, kt_all, vt_all, vtb, lft, xr, gr, ga, gb = _proj_prompt(
            xp, g1, wq, wk.T, wv.T, wf, bf_row, w4, l, kt_all, vt_all)
        attn = _attn_prompt(qa, kta, vtb)
        y, tail, hl = _lru_prompt(xr, gr, jnp.zeros((b, SUBLANES, d), F32), jnp.zeros((b, 1, d), F32), *lru_w)
        merge_in = (attn.reshape(m, a), y.reshape(m, d), ga.reshape(m, d), gb.reshape(m, d),
                    xp.reshape(m, d), wa, wb, wo, g2)
        if l % 2 == 0:
            x1, xn = _merge(*merge_in)
            x2 = _ffn(xn, x1, *ffn_w)
        else:
            x1, x_rt, meta, meta_t, counts = _merge_route(*merge_in, wr_pad)
            x2 = _moe_sorted(x1, x_rt, meta, meta_t, counts, *ffn_w)
        xp = _ple(x2, pp_all, l, g3, wgate, wproj, gf, final).reshape(b, t, d)
        outs["lp"].append(jnp.transpose(lft, (0, 2, 1)))
        outs["cp"].append(tail[:, SUBLANES - (CONV_WIDTH - 1):, :])
        outs["hp"].append(hl.reshape(b, d))

        qs, ks, vs, lfs, xrs, grs, gas, gbs = _proj_sample(xs, g1, wq, wk, wv, wf, bf_row, w4)
        lf8 = lfs[:, :N_HEADS]
        attn_s = _attn_decode(l, page_table, qs.reshape(db, 1, a), ks.reshape(db, 1, a), vs.reshape(db, 1, a),
                              lf8.reshape(db, N_HEADS, 1), kt_pool, vt_pool, lt_pool, n_pool).reshape(db, a)
        sc = state_conv[l]
        ys, hn = _lru_sample(xrs, grs, sc[:, 0], sc[:, 1], sc[:, 2], state_h[l], *lru_w)
        x1s, xns = _merge(attn_s, ys, gas, gbs, xs, wa, wb, wo, g2)
        if l % 2 == 0:
            x2s = _ffn(xns, x1s, *ffn_w)
        else:
            x2s = _moe(xns, x1s, _router(xns, wr_pad), *ffn_w)
        xs = _ple(x2s, ps_all, l, g3, wgate, wproj, gf, final)
        outs["ks"].append(ks.reshape(db, 1, N_HEADS, HEAD_DIM))
        outs["vs"].append(vs.reshape(db, 1, N_HEADS, HEAD_DIM))
        outs["ls"].append(lf8.reshape(db, 1, N_HEADS))
        outs["cs"].append(jnp.stack([sc[:, 1], sc[:, 2], xrs], axis=1))
        outs["hs"].append(hn)

    st = lambda name: jnp.stack(outs[name])
    heads_last = lambda v: jnp.transpose(v.reshape(depth, b, N_HEADS, HEAD_DIM, t), (0, 1, 4, 2, 3))
    return (xp, xs.reshape(db, 1, d), heads_last(kt_all), heads_last(vt_all), st("lp"), st("cp"), st("hp"),
            st("ks"), st("vs"), st("ls"), st("cs"), st("hs"))
```

```python
import functools

import jax
import jax.numpy as jnp
from jax import lax
from jax.experimental import pallas as pl
from jax.experimental.pallas import tpu as pltpu
from jax.experimental.pallas import tpu_sc as plsc

BF16 = jnp.bfloat16
F32 = jnp.float32

EPS = 1e-6
NEG_INF = -1e30
LRU_C = 8.0
N_HEADS = 8
HEAD_DIM = 64
ATTN_WIDTH = N_HEADS * HEAD_DIM
N_LRU_BLOCKS = 8
CONV_WIDTH = 4
N_EXPERTS = 8

LANES = 128
SUBLANES = 8
VMEM_LIMIT_BYTES = 56 * 1024 * 1024

PROJ_TILE = 512
ATTN_TILE = 512
LRU_TILE = 256
ROW_TILE = 512
FFN_ROW_TILE = 1024
FFN_COL_TILE = 512
MOE_ROW_TILE = 512
MOE_COL_TILE = 1792
DECODE_PAGES = 16


def _cparams(*sem):
    return pltpu.CompilerParams(dimension_semantics=sem, vmem_limit_bytes=VMEM_LIMIT_BYTES)


def _full(shape):
    zeros = (0,) * len(shape)
    return pl.BlockSpec(shape, lambda *_: zeros, pipeline_mode=pl.Buffered(1))


def _rmsnorm(x, g):
    return x * lax.rsqrt(jnp.mean(x * x, axis=-1, keepdims=True) + EPS) * g


def _log_sigmoid(z):
    return jnp.minimum(z, 0.0) - jnp.log1p(jnp.exp(-jnp.abs(z)))


def _softplus(z):
    return jnp.maximum(z, 0.0) + jnp.log1p(jnp.exp(-jnp.abs(z)))


def _dot(a, b):
    return jnp.dot(a, b, preferred_element_type=F32)


def _dot_nt(a, b):
    return lax.dot_general(a, b, (((1,), (1,)), ((), ())), preferred_element_type=F32)


def _exact_dot(x, w01):
    hi = x.astype(BF16)
    r1 = x - hi.astype(F32)
    mid = r1.astype(BF16)
    lo = (r1 - mid.astype(F32)).astype(BF16)
    return _dot(hi, w01) + _dot(mid, w01) + _dot(lo, w01)


def _split3(x):
    p1 = x.astype(BF16).astype(F32)
    r1 = x - p1
    p2 = r1.astype(BF16).astype(F32)
    p3 = (r1 - p2).astype(BF16).astype(F32)
    return p1, p2, p3


def _proj_prompt_kernel(x_ref, g_ref, wq_ref, wkt_ref, wvt_ref, wf_ref, wft_ref, bfr_ref, bfc_ref, w4_ref,
                        triu_ref, tril_ref, kt_all_ref, vt_all_ref,
                        qa_ref, kta_ref, kt_ref, vt_ref, vtb_ref, lft_ref,
                        xr_ref, gr_ref, ga_ref, gb_ref, carry_t_ref, carry_c_ref):
    del kt_all_ref, vt_all_ref
    t = pl.program_id(1)

    @pl.when(t == 0)
    def _():
        carry_t_ref[...] = jnp.zeros_like(carry_t_ref)
        carry_c_ref[...] = jnp.zeros_like(carry_c_ref)

    tm = x_ref.shape[1]
    xn = _rmsnorm(x_ref[0], g_ref[...]).astype(BF16)
    lft = _log_sigmoid(_dot_nt(wft_ref[...], xn) + bfc_ref[...])[:N_HEADS]
    lft_ref[0] = lft
    lfc = _log_sigmoid(_dot(xn, wf_ref[...]) + bfr_ref[...])
    ct = _exact_dot(lft, triu_ref[...]) + carry_t_ref[:, :1]
    carry_t_ref[...] = jnp.broadcast_to(ct[:, -1:], carry_t_ref.shape)
    tril = tril_ref[...]
    cc = carry_c_ref[0:1, :]
    for piece in _split3(lfc):
        cc = cc + _dot(tril, piece.astype(BF16))
    carry_c_ref[...] = jnp.broadcast_to(cc[-1:, :], carry_c_ref.shape)

    q = _dot(xn, wq_ref[...]) * (HEAD_DIM ** -0.5)
    kt = _dot_nt(wkt_ref[...], xn)
    kt_ref[0, 0] = kt
    lane = lax.broadcasted_iota(jnp.int32, (tm, HEAD_DIM), 1)
    sub = lax.broadcasted_iota(jnp.int32, (HEAD_DIM, tm), 0)
    for h in range(N_HEADS):
        hs = slice(h * HEAD_DIM, (h + 1) * HEAD_DIM)
        q1, q2, q3 = _split3(cc[:, h:h + 1])
        q_extra = jnp.where(lane == 0, q1, jnp.where(lane == 1, q2, jnp.where(lane == 2, q3,
                            jnp.where(lane < 6, 1.0, 0.0))))
        qa_ref[0, h] = jnp.concatenate([q[:, hs], q_extra], axis=1).astype(BF16)
        k1, k2, k3 = _split3(ct[h:h + 1, :])
        k_extra = jnp.where(sub < 3, 1.0, jnp.where(sub == 3, -k1, jnp.where(sub == 4, -k2,
                            jnp.where(sub == 5, -k3, 0.0))))
        kta_ref[0, 2 * h * HEAD_DIM:2 * (h + 1) * HEAD_DIM, :] = jnp.concatenate(
            [kt[hs, :], k_extra], axis=0).astype(BF16)
    vt = _dot_nt(wvt_ref[...], xn)
    vt_ref[0, 0] = vt
    vtb_ref[0] = vt.astype(BF16)
    d = xr_ref.shape[-1]
    xr_ref[0] = _dot(xn, w4_ref[:, 0 * d:1 * d])
    gr_ref[0] = _dot(xn, w4_ref[:, 1 * d:2 * d])
    ga_ref[0] = _dot(xn, w4_ref[:, 2 * d:3 * d])
    gb_ref[0] = _dot(xn, w4_ref[:, 3 * d:4 * d])


def _proj_prompt(x, g, wq, wkt, wvt, wf, bf_row, w4, layer, kt_all, vt_all):
    b, t, d = x.shape
    tm = min(PROJ_TILE, t)
    nt = t // tm
    ri = lax.broadcasted_iota(jnp.int32, (tm, tm), 0)
    ci = lax.broadcasted_iota(jnp.int32, (tm, tm), 1)
    triu = (ri <= ci).astype(BF16)
    tril = (ri >= ci).astype(BF16)
    a = ATTN_WIDTH
    seq = lambda w: pl.BlockSpec((1, tm, w), lambda i, j: (i, j, 0))
    tr = lambda r: pl.BlockSpec((1, r, tm), lambda i, j: (i, 0, j))
    out_shape = [
        jax.ShapeDtypeStruct((b, N_HEADS, t, 2 * HEAD_DIM), BF16),
        jax.ShapeDtypeStruct((b, 2 * a, t), BF16),
        jax.ShapeDtypeStruct(kt_all.shape, F32), jax.ShapeDtypeStruct(vt_all.shape, F32),
        jax.ShapeDtypeStruct((b, a, t), BF16),
        jax.ShapeDtypeStruct((b, N_HEADS, t), F32),
    ] + [jax.ShapeDtypeStruct((b, t, d), F32)] * 4
    stacked = pl.BlockSpec((1, 1, a, tm), lambda i, j: (layer, i, 0, j))
    out_specs = [
        pl.BlockSpec((1, N_HEADS, tm, 2 * HEAD_DIM), lambda i, j: (i, 0, j, 0)),
        tr(2 * a), stacked, stacked, tr(a), tr(N_HEADS),
        seq(d), seq(d), seq(d), seq(d),
    ]
    args = (x, g, wq, wkt, wvt, wf, wf.T, bf_row, bf_row.reshape(LANES, 1), w4, triu, tril)
    return pl.pallas_call(
        _proj_prompt_kernel,
        grid=(b, nt),
        in_specs=[seq(d)] + [_full(v.shape) for v in args[1:]] + [pl.BlockSpec(memory_space=pl.ANY)] * 2,
        out_specs=out_specs,
        out_shape=out_shape,
        input_output_aliases={len(args): 2, len(args) + 1: 3},
        scratch_shapes=[pltpu.VMEM((N_HEADS, LANES), F32), pltpu.VMEM((N_HEADS, LANES), F32)],
        compiler_params=_cparams("arbitrary", "arbitrary"),
        name="proj_prompt",
    )(*args, kt_all, vt_all)


def _proj_sample_kernel(x_ref, g_ref, wq_ref, wk_ref, wv_ref, wf_ref, bf_ref, w4_ref,
                        q_ref, k_ref, v_ref, lf_ref, xr_ref, gr_ref, ga_ref, gb_ref):
    xn = _rmsnorm(x_ref[...], g_ref[...]).astype(BF16)
    q_ref[...] = _dot(xn, wq_ref[...])
    k_ref[...] = _dot(xn, wk_ref[...])
    v_ref[...] = _dot(xn, wv_ref[...])
    lf_ref[...] = _log_sigmoid(_dot(xn, wf_ref[...]) + bf_ref[...])
    d = xr_ref.shape[-1]
    xr_ref[...] = _dot(xn, w4_ref[:, 0 * d:1 * d])
    gr_ref[...] = _dot(xn, w4_ref[:, 1 * d:2 * d])
    ga_ref[...] = _dot(xn, w4_ref[:, 2 * d:3 * d])
    gb_ref[...] = _dot(xn, w4_ref[:, 3 * d:4 * d])


def _proj_sample(x, g, wq, wk, wv, wf, bf_row, w4):
    m, d = x.shape
    a = ATTN_WIDTH
    shapes = [(m, a), (m, a), (m, a), (m, LANES), (m, d), (m, d), (m, d), (m, d)]
    return pl.pallas_call(
        _proj_sample_kernel,
        grid=(1,),
        in_specs=[_full(v.shape) for v in (x, g, wq, wk, wv, wf, bf_row, w4)],
        out_specs=[_full(s) for s in shapes],
        out_shape=[jax.ShapeDtypeStruct(s, F32) for s in shapes],
        compiler_params=_cparams("arbitrary"),
        name="proj_sample",
    )(x, g, wq, wk, wv, wf, bf_row, w4)


def _attn_prompt_kernel(qi_ref, ki_ref, q_ref, kta_ref, vt_ref, o_ref, m_ref, l_ref, acc_ref):
    step = pl.program_id(1)
    qi = qi_ref[step]
    ki = ki_ref[step]
    ta = q_ref.shape[2]

    @pl.when(ki == 0)
    def _():
        m_ref[...] = jnp.full_like(m_ref, NEG_INF)
        l_ref[...] = jnp.zeros_like(l_ref)
        acc_ref[...] = jnp.zeros_like(acc_ref)

    def update(on_diagonal):
        if on_diagonal:
            mask = (lax.broadcasted_iota(jnp.int32, (ta, ta), 1)
                    <= lax.broadcasted_iota(jnp.int32, (ta, ta), 0))
        for h in range(N_HEADS):
            s = _dot(q_ref[0, h], kta_ref[0, 2 * h * HEAD_DIM:2 * (h + 1) * HEAD_DIM, :])
            if on_diagonal:
                s = jnp.where(mask, s, NEG_INF)
            m_old = m_ref[h]
            m_new = jnp.maximum(m_old, jnp.max(s, axis=-1, keepdims=True))
            p = jnp.exp(s - jnp.concatenate([m_new] * (ta // LANES), axis=1))
            alpha = jnp.exp(m_old - m_new)
            l_ref[h] = alpha * l_ref[h] + jnp.sum(p, axis=-1, keepdims=True)
            pv = _dot_nt(p.astype(BF16), vt_ref[0, h * HEAD_DIM:(h + 1) * HEAD_DIM, :])
            acc_ref[h] = alpha[:, :HEAD_DIM] * acc_ref[h] + pv
            m_ref[h] = m_new

    @pl.when(ki < qi)
    def _():
        update(False)

    @pl.when(ki == qi)
    def _():
        update(True)
        for h in range(0, N_HEADS, 2):
            pair = [acc_ref[h + j] / l_ref[h + j][:, :HEAD_DIM] for j in range(2)]
            o_ref[0, :, h * HEAD_DIM:(h + 2) * HEAD_DIM] = jnp.concatenate(pair, axis=1).astype(o_ref.dtype)


def _attn_prompt(qa, kta, vtb):
    b, _, t, _ = qa.shape
    ta = min(ATTN_TILE, t)
    nq = t // ta
    pairs = [(i, j) for i in range(nq) for j in range(i + 1)]
    qi_tab = jnp.asarray([p[0] for p in pairs], jnp.int32)
    ki_tab = jnp.asarray([p[1] for p in pairs], jnp.int32)
    a = ATTN_WIDTH
    grid_spec = pltpu.PrefetchScalarGridSpec(
        num_scalar_prefetch=2,
        grid=(b, len(pairs)),
        in_specs=[
            pl.BlockSpec((1, N_HEADS, ta, 2 * HEAD_DIM), lambda i, s, qt, kt: (i, 0, qt[s], 0)),
            pl.BlockSpec((1, 2 * a, ta), lambda i, s, qt, kt: (i, 0, kt[s])),
            pl.BlockSpec((1, a, ta), lambda i, s, qt, kt: (i, 0, kt[s])),
        ],
        out_specs=pl.BlockSpec((1, ta, a), lambda i, s, qt, kt: (i, qt[s], 0)),
        scratch_shapes=[pltpu.VMEM((N_HEADS, ta, LANES), F32), pltpu.VMEM((N_HEADS, ta, LANES), F32),
                        pltpu.VMEM((N_HEADS, ta, HEAD_DIM), F32)],
    )
    return pl.pallas_call(
        _attn_prompt_kernel,
        grid_spec=grid_spec,
        out_shape=jax.ShapeDtypeStruct((b, t, a), BF16),
        compiler_params=_cparams("arbitrary", "arbitrary"),
        name="attn_prompt",
    )(qi_tab, ki_tab, qa, kta, vtb)


def _attn_decode_kernel(n_pg, pt_ref, q_ref, kn_ref, vn_ref, lfn_ref, w_ref, *rest):
    k_refs = rest[0:n_pg]
    v_refs = rest[n_pg:2 * n_pg]
    lt_refs = rest[2 * n_pg:3 * n_pg]
    o_ref = rest[3 * n_pg]
    m_ref, l_ref, c_ref, acc_ref = rest[3 * n_pg + 1:]
    c_step = pl.program_id(1)
    width = acc_ref.shape[1]
    own = (lax.broadcasted_iota(jnp.int32, (N_HEADS, width), 1) // HEAD_DIM
           == lax.broadcasted_iota(jnp.int32, (N_HEADS, width), 0))
    q_rows = jnp.where(own, q_ref[0] * (HEAD_DIM ** -0.5), 0.0)

    @pl.when(c_step == 0)
    def _():
        lane = lax.broadcasted_iota(jnp.int32, (N_HEADS, LANES), 1)
        s0 = jnp.sum(q_rows * kn_ref[0], axis=1, keepdims=True)
        m_ref[...] = jnp.broadcast_to(s0, m_ref.shape)
        l_ref[...] = jnp.where(lane == 0, 1.0, 0.0)
        acc_ref[...] = jnp.broadcast_to(vn_ref[0], acc_ref.shape)
        c_ref[...] = jnp.broadcast_to(lfn_ref[0], c_ref.shape)

    qb = q_rows.astype(BF16)
    carry = c_ref[...]
    s = []
    for i in range(n_pg):
        r = _exact_dot(lt_refs[i][0], w_ref[...])
        s.append(_dot(qb, k_refs[i][...].astype(BF16)) + (r[:, :LANES] + carry))
        carry = carry + r[:, LANES:]
    c_ref[...] = carry

    m_tile = s[0]
    for i in range(1, n_pg):
        m_tile = jnp.maximum(m_tile, s[i])
    m_old = m_ref[...]
    m_new = jnp.maximum(m_old, jnp.max(m_tile, axis=1, keepdims=True))
    alpha = jnp.exp(m_old - m_new)
    l_new = alpha * l_ref[...]
    pv = jnp.zeros(acc_ref.shape, F32)
    for i in range(n_pg):
        p = jnp.exp(s[i] - m_new)
        l_new = l_new + p
        pv = pv + _dot_nt(p.astype(BF16), v_refs[i][...].astype(BF16))
    m_ref[...] = m_new
    l_ref[...] = l_new
    acc_ref[...] = jnp.concatenate([alpha] * (width // LANES), axis=1) * acc_ref[...] + pv

    @pl.when(c_step == pl.num_programs(1) - 1)
    def _():
        den = jnp.sum(l_ref[...], axis=1, keepdims=True)
        o_ref[0] = jnp.sum(jnp.where(own, acc_ref[...] / den, 0.0), axis=0, keepdims=True)


def _attn_decode(layer, page_table, q, kn, vn, lfn_col, kt_pool, vt_pool, lt_pool, n_pool):
    db, n_pages = page_table.shape
    n_pg = min(DECODE_PAGES, n_pages)
    n_chunks = n_pages // n_pg
    page = kt_pool.shape[-1]
    base = layer * n_pool
    rowi = lax.broadcasted_iota(jnp.int32, (page, 2 * page), 0)
    coli = lax.broadcasted_iota(jnp.int32, (page, 2 * page), 1)
    w = ((rowi > coli) | (coli >= page)).astype(BF16)

    def page_of(slot):
        def index(b, c, pt):
            return base + pt[b * n_pages + (n_pages - 1 - (c * n_pg + slot))]
        return index

    per_seq = lambda r, c: pl.BlockSpec((1, r, c), lambda b, ch, pt: (b, 0, 0))

    def kv_specs():
        return [pl.BlockSpec((ATTN_WIDTH, page), (lambda b, ch, pt, f=page_of(i): (f(b, ch, pt), 0)))
                for i in range(n_pg)]

    lt_specs = [pl.BlockSpec((1, N_HEADS, page), (lambda b, ch, pt, f=page_of(i): (f(b, ch, pt), 0, 0)))
                for i in range(n_pg)]
    grid_spec = pltpu.PrefetchScalarGridSpec(
        num_scalar_prefetch=1,
        grid=(db, n_chunks),
        in_specs=[per_seq(1, ATTN_WIDTH), per_seq(1, ATTN_WIDTH), per_seq(1, ATTN_WIDTH), per_seq(N_HEADS, 1),
                  pl.BlockSpec(w.shape, lambda b, ch, pt: (0, 0))] + kv_specs() + kv_specs() + lt_specs,
        out_specs=per_seq(1, ATTN_WIDTH),
        scratch_shapes=[pltpu.VMEM((N_HEADS, LANES), F32), pltpu.VMEM((N_HEADS, LANES), F32),
                        pltpu.VMEM((N_HEADS, LANES), F32), pltpu.VMEM((N_HEADS, ATTN_WIDTH), F32)],
    )
    return pl.pallas_call(
        functools.partial(_attn_decode_kernel, n_pg),
        grid_spec=grid_spec,
        out_shape=jax.ShapeDtypeStruct((db, 1, ATTN_WIDTH), F32),
        compiler_params=_cparams("arbitrary", "arbitrary"),
        name="attn_decode",
    )(page_table.reshape(-1), q, kn, vn, lfn_col, w,
      *([kt_pool] * n_pg), *([vt_pool] * n_pg), *([lt_pool] * n_pg))


def _lru_gates(xc, wrg_ref, brg_ref, wig_ref, big_ref, lam_ref, store):
    xcb = xc.astype(BF16)
    blk = xc.shape[-1] // N_LRU_BLOCKS
    for n in range(N_LRU_BLOCKS):
        cs = slice(n * blk, (n + 1) * blk)
        r = jax.nn.sigmoid(_dot(xcb[:, cs], wrg_ref[n]) + brg_ref[:, cs])
        i = jax.nn.sigmoid(_dot(xcb[:, cs], wig_ref[n]) + big_ref[:, cs])
        log_a = -LRU_C * r * _softplus(-lam_ref[:, cs])
        a = jnp.exp(log_a)
        th = jnp.tanh(log_a)
        mult = jnp.sqrt(-2.0 * th / (1.0 - th))
        store(cs, a, mult * (i * xc[:, cs]))


def _lru_prompt_kernel(xr_ref, gr_ref, cp_ref, h0_ref, cw_ref, cb_ref, wrg_ref, brg_ref, wig_ref,
                       big_ref, lam_ref, y_ref, tail_ref, hl_ref, buf_ref, a_ref, b_ref, h_ref, hc_ref):
    t = pl.program_id(1)
    tt = xr_ref.shape[1]
    halo = SUBLANES

    @pl.when(t == 0)
    def _():
        buf_ref[0:halo, :] = cp_ref[0]
        hc_ref[...] = h0_ref[0]

    x = xr_ref[0]
    buf_ref[halo:, :] = x
    taps = [buf_ref[pl.ds(halo - (CONV_WIDTH - 1) + j, tt), :] * cw_ref[j:j + 1, :]
            for j in range(CONV_WIDTH - 1)]
    taps.append(x * cw_ref[CONV_WIDTH - 1:CONV_WIDTH, :])
    acc = taps[0]
    for term in taps[1:]:
        acc = acc + term
    xc = cb_ref[...] + acc

    def store(cs, a, b):
        a_ref[:, cs] = a
        b_ref[:, cs] = b

    _lru_gates(xc, wrg_ref, brg_ref, wig_ref, big_ref, lam_ref, store)

    grouped = (tt // SUBLANES, SUBLANES, a_ref.shape[1])
    a = a_ref[...].reshape(grouped)
    b = b_ref[...].reshape(grouped)
    in_group = lax.broadcasted_iota(jnp.int32, grouped, 1)
    shift = 1
    while shift < SUBLANES:
        a_prev = pltpu.roll(a, shift, axis=1)
        b_prev = pltpu.roll(b, shift, axis=1)
        take = in_group >= shift
        b = jnp.where(take, a * b_prev + b, b)
        a = jnp.where(take, a * a_prev, a)
        shift *= 2
    a_ref[...] = a.reshape(a_ref.shape)
    b_ref[...] = b.reshape(b_ref.shape)
    h_in = jnp.broadcast_to(hc_ref[...], (SUBLANES, a_ref.shape[1]))
    for g in range(tt // SUBLANES):
        rows = slice(g * SUBLANES, (g + 1) * SUBLANES)
        h_grp = a_ref[rows, :] * h_in + b_ref[rows, :]
        h_ref[rows, :] = h_grp
        h_in = jnp.broadcast_to(h_grp[SUBLANES - 1:, :], h_grp.shape)
    h_last = h_in[:1, :]
    hc_ref[...] = h_last
    hl_ref[0] = h_last
    y_ref[0] = (h_ref[...] * jax.nn.gelu(gr_ref[0])).astype(y_ref.dtype)
    tail = buf_ref[tt:tt + halo, :]
    tail_ref[0] = tail
    buf_ref[0:halo, :] = tail


def _lru_prompt(xr, gr, conv_prev8, h0, cw, cb, wrg, brg, wig, big, lam):
    b, t, d = xr.shape
    tt = min(LRU_TILE, t)
    seq = pl.BlockSpec((1, tt, d), lambda i, j: (i, j, 0))
    per_b = lambda r: pl.BlockSpec((1, r, d), lambda i, j: (i, 0, 0))
    return pl.pallas_call(
        _lru_prompt_kernel,
        grid=(b, t // tt),
        in_specs=[seq, seq, per_b(SUBLANES), per_b(1)] + [_full(v.shape) for v in (cw, cb, wrg, brg, wig, big, lam)],
        out_specs=[seq, per_b(SUBLANES), per_b(1)],
        out_shape=[jax.ShapeDtypeStruct((b, t, d), BF16), jax.ShapeDtypeStruct((b, SUBLANES, d), F32),
                   jax.ShapeDtypeStruct((b, 1, d), F32)],
        scratch_shapes=[pltpu.VMEM((tt + SUBLANES, d), F32), pltpu.VMEM((tt, d), F32),
                        pltpu.VMEM((tt, d), F32), pltpu.VMEM((tt, d), F32), pltpu.VMEM((1, d), F32)],
        compiler_params=_cparams("arbitrary", "arbitrary"),
        name="lru_prompt",
    )(xr, gr, conv_prev8, h0, cw, cb, wrg, brg, wig, big, lam)


def _lru_sample_kernel(xr_ref, gr_ref, c0_ref, c1_ref, c2_ref, h0_ref, cw_ref, cb_ref, wrg_ref, brg_ref,
                       wig_ref, big_ref, lam_ref, y_ref, hn_ref):
    x = xr_ref[...]
    acc = c0_ref[...] * cw_ref[0:1, :]
    acc = acc + c1_ref[...] * cw_ref[1:2, :]
    acc = acc + c2_ref[...] * cw_ref[2:3, :]
    acc = acc + x * cw_ref[3:4, :]
    xc = cb_ref[...] + acc

    def store(cs, a, b):
        hn_ref[:, cs] = a * h0_ref[:, cs] + b

    _lru_gates(xc, wrg_ref, brg_ref, wig_ref, big_ref, lam_ref, store)
    y_ref[...] = (hn_ref[...] * jax.nn.gelu(gr_ref[...])).astype(y_ref.dtype)


def _lru_sample(xr, gr, c0, c1, c2, h0, cw, cb, wrg, brg, wig, big, lam):
    m, d = xr.shape
    args = (xr, gr, c0, c1, c2, h0, cw, cb, wrg, brg, wig, big, lam)
    return pl.pallas_call(
        _lru_sample_kernel,
        grid=(1,),
        in_specs=[_full(v.shape) for v in args],
        out_specs=[_full((m, d)), _full((m, d))],
        out_shape=[jax.ShapeDtypeStruct((m, d), BF16), jax.ShapeDtypeStruct((m, d), F32)],
        compiler_params=_cparams("arbitrary"),
        name="lru_sample",
    )(*args)


def _merge_kernel(attn_ref, y_ref, ga_ref, gb_ref, x_ref, wa_ref, wb_ref, wo_ref, g_ref, x1_ref, xn_ref):
    merged = (jax.nn.sigmoid(ga_ref[...]) * _dot(attn_ref[...].astype(BF16), wa_ref[...])
              + jax.nn.sigmoid(gb_ref[...]) * _dot(y_ref[...], wb_ref[...]))
    x1 = x_ref[...] + _dot(merged.astype(BF16), wo_ref[...])
    x1_ref[...] = x1
    xn_ref[...] = _rmsnorm(x1, g_ref[...]).astype(xn_ref.dtype)


def _merge(attn, y, ga, gb, x, wa, wb, wo, g):
    m, d = x.shape
    tm = min(ROW_TILE, m)
    row = lambda w: pl.BlockSpec((tm, w), lambda i: (i, 0))
    return pl.pallas_call(
        _merge_kernel,
        grid=(m // tm,),
        in_specs=[row(attn.shape[1]), row(d), row(d), row(d), row(d)] + [_full(v.shape) for v in (wa, wb, wo, g)],
        out_specs=[row(d), row(d)],
        out_shape=[jax.ShapeDtypeStruct((m, d), F32), jax.ShapeDtypeStruct((m, d), BF16)],
        compiler_params=_cparams("arbitrary"),
        name="merge",
    )(attn, y, ga, gb, x, wa, wb, wo, g)


def _top2(logits):
    lane = lax.broadcasted_iota(jnp.int32, logits.shape, 1)
    logits = jnp.where(lane < N_EXPERTS, logits, -jnp.inf)
    m1 = jnp.max(logits, axis=-1, keepdims=True)
    i1 = jnp.min(jnp.where(logits == m1, lane, LANES), axis=-1, keepdims=True)
    rest = jnp.where(lane == i1, -jnp.inf, logits)
    m2 = jnp.max(rest, axis=-1, keepdims=True)
    i2 = jnp.min(jnp.where(rest == m2, lane, LANES), axis=-1, keepdims=True)
    e2 = jnp.exp(m2 - m1)
    den = 1.0 + e2
    return lane, i1, i2, 1.0 / den, e2 / den


def _to_row_tiles(dst_ref, x):
    rows = x.shape[0]
    for s in range(x.shape[1] // LANES):
        dst_ref[pl.ds(s, rows, stride=SUBLANES), :] = x[:, s * LANES:(s + 1) * LANES]


def _from_row_tiles(src_ref, rows, s):
    return src_ref[pl.ds(s, rows, stride=SUBLANES), :]


def _merge_route_kernel(attn_ref, y_ref, ga_ref, gb_ref, x_ref, wa_ref, wb_ref, wo_ref, g_ref, wr_ref, tri_ref,
                        x1_ref, xrt_ref, meta_ref, metat_ref, cnt_ref, carry_ref):
    i = pl.program_id(0)

    @pl.when(i == 0)
    def _():
        carry_ref[...] = jnp.zeros_like(carry_ref)

    merged = (jax.nn.sigmoid(ga_ref[...]) * _dot(attn_ref[...].astype(BF16), wa_ref[...])
              + jax.nn.sigmoid(gb_ref[...]) * _dot(y_ref[...], wb_ref[...]))
    x1 = x_ref[...] + _dot(merged.astype(BF16), wo_ref[...])
    x1_ref[...] = x1
    xn = _rmsnorm(x1, g_ref[...])
    _to_row_tiles(xrt_ref, xn)
    lane, i1, i2, w1, w2 = _top2(_dot(xn.astype(BF16), wr_ref[...]))
    oh1 = lane == i1
    oh2 = lane == i2
    tri = tri_ref[...]
    before1 = _dot(tri, oh1.astype(BF16))
    before2 = _dot(tri, oh2.astype(BF16))
    cnt1 = jnp.sum(oh1.astype(F32), axis=0, keepdims=True)
    cnt2 = jnp.sum(oh2.astype(F32), axis=0, keepdims=True)
    carry = carry_ref[0:1, :]
    rank1 = jnp.sum(jnp.where(oh1, carry + before1, 0.0), axis=-1, keepdims=True)
    rank2 = jnp.sum(jnp.where(oh2, carry + cnt1 + before2, 0.0), axis=-1, keepdims=True)
    total = carry + cnt1 + cnt2
    carry_ref[...] = jnp.broadcast_to(total, carry_ref.shape)
    cnt_ref[...] = jnp.broadcast_to(total, cnt_ref.shape)
    cols = (i1.astype(F32), i2.astype(F32), rank1, rank2, w1, w2)
    meta = jnp.zeros(meta_ref.shape, F32)
    for k, v in enumerate(cols):
        meta = jnp.where(lane == k, v, meta)
    meta_ref[...] = meta
    sel = (lax.broadcasted_iota(jnp.int32, (SUBLANES, LANES), 0)
           == lax.broadcasted_iota(jnp.int32, (SUBLANES, LANES), 1)).astype(BF16)
    meta_t = jnp.zeros(metat_ref.shape, F32)
    for piece in _split3(meta):
        meta_t = meta_t + _dot_nt(sel, piece.astype(BF16))
    metat_ref[...] = meta_t


def _merge_route(attn, y, ga, gb, x, wa, wb, wo, g, wr_pad):
    m, d = x.shape
    tm = min(ROW_TILE, m)
    tri = (lax.broadcasted_iota(jnp.int32, (tm, tm), 0)
           > lax.broadcasted_iota(jnp.int32, (tm, tm), 1)).astype(BF16)
    row = lambda w: pl.BlockSpec((tm, w), lambda i: (i, 0))
    return pl.pallas_call(
        _merge_route_kernel,
        grid=(m // tm,),
        in_specs=[row(attn.shape[1]), row(d), row(d), row(d), row(d)]
        + [_full(v.shape) for v in (wa, wb, wo, g, wr_pad, tri)],
        out_specs=[row(d), pl.BlockSpec((tm * SUBLANES, LANES), lambda i: (i, 0)), row(LANES),
                   pl.BlockSpec((SUBLANES, tm), lambda i: (0, i)),
                   pl.BlockSpec((SUBLANES, LANES), lambda i: (0, 0))],
        out_shape=[jax.ShapeDtypeStruct((m, d), F32), jax.ShapeDtypeStruct((m * SUBLANES, LANES), F32),
                   jax.ShapeDtypeStruct((m, LANES), F32), jax.ShapeDtypeStruct((SUBLANES, m), F32),
                   jax.ShapeDtypeStruct((SUBLANES, LANES), F32)],
        scratch_shapes=[pltpu.VMEM((SUBLANES, LANES), F32)],
        compiler_params=_cparams("arbitrary"),
        name="merge_route",
    )(attn, y, ga, gb, x, wa, wb, wo, g, wr_pad, tri)


SC_CORES = 2
SC_SUBCORES = 16
SC_CHUNK_ROWS = 32


def _sc_mesh():
    return plsc.VectorSubcoreMesh(core_axis_name="c", subcore_axis_name="s",
                                  num_cores=SC_CORES, num_subcores=SC_SUBCORES)


def _sc_worker_base(per_worker):
    return (lax.axis_index("s") * SC_CORES + lax.axis_index("c")) * per_worker


def _sc_scatter_rows(rows, dest1, dest2, n_out):
    n = rows.shape[0]
    per_worker = n // (SC_CORES * SC_SUBCORES)
    chunk = SC_CHUNK_ROWS

    def body(rows_hbm, d1_hbm, d2_hbm, out_hbm, i1_v, i2_v, rows_v, sem):
        base = _sc_worker_base(per_worker)

        @pl.loop(0, per_worker // chunk)
        def _(j):
            src = pl.ds(base + pl.multiple_of(j * chunk, chunk), chunk)
            loads = [pltpu.async_copy(d1_hbm.at[src], i1_v, sem), pltpu.async_copy(d2_hbm.at[src], i2_v, sem),
                     pltpu.async_copy(rows_hbm.at[src], rows_v, sem)]
            for load in loads:
                load.wait()
            first = pltpu.async_copy(rows_v, out_hbm.at[i1_v], sem)
            second = pltpu.async_copy(rows_v, out_hbm.at[i2_v], sem)
            first.wait()
            second.wait()

    return pl.kernel(
        body,
        out_type=jax.ShapeDtypeStruct((n_out,) + rows.shape[1:], rows.dtype),
        mesh=_sc_mesh(),
        scratch_types=[pltpu.VMEM((chunk,), jnp.int32), pltpu.VMEM((chunk,), jnp.int32),
                       pltpu.VMEM((chunk,) + rows.shape[1:], rows.dtype), pltpu.SemaphoreType.DMA],
        name="sc_row_scatter",
    )(rows, dest1, dest2)


def _moe_ffn_kernel(te_ref, tv_ref, nu_ref, xs_ref, wg_ref, wu_ref, wd_ref, ys_ref, xb_ref, acc_ref):
    i = pl.program_id(0)
    f = pl.program_id(1)
    rows, d = xb_ref.shape

    @pl.when(i < nu_ref[0])
    def _():
        @pl.when(f == 0)
        def _():
            live = lax.broadcasted_iota(jnp.int32, (rows, LANES), 0) < tv_ref[i]
            for s in range(d // LANES):
                slab = jnp.where(live, _from_row_tiles(xs_ref, rows, s), 0.0)
                xb_ref[:, s * LANES:(s + 1) * LANES] = slab.astype(BF16)
            acc_ref[...] = jnp.zeros_like(acc_ref)

        xb = xb_ref[...]
        hid = jax.nn.silu(_dot(xb, wg_ref[0])) * _dot(xb, wu_ref[0])
        acc_ref[...] += _dot(hid.astype(BF16), wd_ref[0])

        @pl.when(f == pl.num_programs(1) - 1)
        def _():
            _to_row_tiles(ys_ref, acc_ref[...])

    @pl.when((i >= nu_ref[0]) & (f == pl.num_programs(1) - 1))
    def _():
        ys_ref[...] = jnp.zeros_like(ys_ref)


def _moe_ffn(tile_expert, tile_valid, n_used, xs, wg, wu, wd):
    n_e, d, ff = wg.shape
    tm = MOE_ROW_TILE
    tf = MOE_COL_TILE
    nf = ff // tf
    n_tiles = tile_expert.shape[0]

    def tile(i, f, te, tv, nu):
        return (i, 0)

    def col(i, f, nu):
        return jnp.where(i < nu[0], f, nf - 1)

    grid_spec = pltpu.PrefetchScalarGridSpec(
        num_scalar_prefetch=3,
        grid=(n_tiles, nf),
        in_specs=[pl.BlockSpec((tm * SUBLANES, LANES), tile),
                  pl.BlockSpec((1, d, tf), lambda i, f, te, tv, nu: (te[i], 0, col(i, f, nu))),
                  pl.BlockSpec((1, d, tf), lambda i, f, te, tv, nu: (te[i], 0, col(i, f, nu))),
                  pl.BlockSpec((1, tf, d), lambda i, f, te, tv, nu: (te[i], col(i, f, nu), 0))],
        out_specs=pl.BlockSpec((tm * SUBLANES, LANES), tile),
        scratch_shapes=[pltpu.VMEM((tm, d), BF16), pltpu.VMEM((tm, d), F32)],
    )
    return pl.pallas_call(
        _moe_ffn_kernel,
        grid_spec=grid_spec,
        out_shape=jax.ShapeDtypeStruct(xs.shape, F32),
        compiler_params=_cparams("arbitrary", "arbitrary"),
        name="moe_ffn",
    )(tile_expert, tile_valid, n_used, xs, wg, wu, wd)


def _sc_gather_rows(table, idx):
    n = idx.shape[0]
    per_worker = n // (SC_CORES * SC_SUBCORES)
    chunk = SC_CHUNK_ROWS
    mesh = _sc_mesh()

    def body(table_hbm, idx_hbm, out_hbm, idx_v, rows_v, sem):
        base = _sc_worker_base(per_worker)
        pltpu.sync_copy(idx_hbm.at[pl.ds(base, per_worker)], idx_v)

        @pl.loop(0, per_worker // chunk)
        def _(j):
            off = pl.multiple_of(j * chunk, chunk)
            pltpu.async_copy(table_hbm.at[idx_v.at[pl.ds(off, chunk)]], rows_v, sem).wait()
            pltpu.sync_copy(rows_v, out_hbm.at[pl.ds(base + off, chunk)])

    return pl.kernel(
        body,
        out_type=jax.ShapeDtypeStruct((n,) + table.shape[1:], table.dtype),
        mesh=mesh,
        scratch_types=[pltpu.VMEM((per_worker,), jnp.int32), pltpu.VMEM((chunk,) + table.shape[1:], table.dtype),
                       pltpu.SemaphoreType.DMA],
        name="sc_row_gather",
    )(table, idx)


def _combine_dense_kernel(x1_ref, meta_ref, g1_ref, g2_ref, o_ref):
    tm, d = x1_ref.shape
    w1 = meta_ref[:, 4:5]
    w2 = meta_ref[:, 5:6]
    for s in range(d // LANES):
        cs = slice(s * LANES, (s + 1) * LANES)
        o_ref[:, cs] = (x1_ref[:, cs] + w1 * _from_row_tiles(g1_ref, tm, s)
                        + w2 * _from_row_tiles(g2_ref, tm, s))


def _combine_dense(x1, meta, gathered):
    m, d = x1.shape
    tm = min(ROW_TILE, m)
    nt = m // tm
    row = lambda w: pl.BlockSpec((tm, w), lambda i: (i, 0))
    return pl.pallas_call(
        _combine_dense_kernel,
        grid=(nt,),
        in_specs=[row(d), row(LANES), pl.BlockSpec((tm * SUBLANES, LANES), lambda i: (i, 0)),
                  pl.BlockSpec((tm * SUBLANES, LANES), lambda i: (nt + i, 0))],
        out_specs=row(d),
        out_shape=jax.ShapeDtypeStruct((m, d), F32),
        compiler_params=_cparams("arbitrary"),
        name="moe_combine",
    )(x1, meta, gathered, gathered)


def _moe_sorted(x1, x_rt, meta, meta_t, counts, wg, wu, wd):
    m = x1.shape[0]
    tm = MOE_ROW_TILE
    n_tiles = (2 * m) // tm + N_EXPERTS
    cnt = counts[0, :N_EXPERTS].astype(jnp.int32)
    padded = ((cnt + tm - 1) // tm) * tm
    ends = jnp.cumsum(padded)
    starts = ends - padded
    experts = jnp.arange(N_EXPERTS, dtype=jnp.int32)[:, None]
    fields = meta_t.astype(jnp.int32)

    def dest(expert_row, rank_row):
        return jnp.sum(jnp.where(fields[expert_row][None, :] == experts, starts[:, None], 0), axis=0) + fields[rank_row]

    dest1 = dest(0, 2)
    dest2 = dest(1, 3)
    n_used = (ends[-1] // tm).reshape(1)
    tile_start = jnp.minimum(jnp.arange(n_tiles, dtype=jnp.int32), n_used - 1) * tm
    tile_expert = jnp.sum(ends[None, :] <= tile_start[:, None], axis=1).astype(jnp.int32)
    group_end = jnp.sum(jnp.where(tile_expert[:, None] == experts.T, (starts + cnt)[None, :], 0), axis=1)
    tile_valid = jnp.clip(group_end - tile_start, 0, tm).astype(jnp.int32)
    xs = _sc_scatter_rows(x_rt.reshape(-1, SUBLANES, LANES), dest1, dest2, n_tiles * tm).reshape(-1, LANES)
    ys = _moe_ffn(tile_expert, tile_valid, n_used.astype(jnp.int32), xs, wg, wu, wd)
    gathered = _sc_gather_rows(ys.reshape(-1, SUBLANES, LANES), jnp.concatenate([dest1, dest2]))
    return _combine_dense(x1, meta, gathered.reshape(-1, LANES))


def _ffn_kernel(xn_ref, x1_ref, wg_ref, wu_ref, wd_ref, o_ref, acc_ref):
    f = pl.program_id(1)

    @pl.when(f == 0)
    def _():
        acc_ref[...] = jnp.zeros_like(acc_ref)

    xn = xn_ref[...]
    hid = jax.nn.silu(_dot(xn, wg_ref[...].astype(BF16))) * _dot(xn, wu_ref[...].astype(BF16))
    acc_ref[...] += _dot(hid.astype(BF16), wd_ref[...].astype(BF16))

    @pl.when(f == pl.num_programs(1) - 1)
    def _():
        o_ref[...] = x1_ref[...] + acc_ref[...]


def _ffn(xn, x1, wg, wu, wd):
    m, d = x1.shape
    ff = wg.shape[1]
    tm = min(FFN_ROW_TILE, m)
    tf = FFN_COL_TILE
    row = pl.BlockSpec((tm, d), lambda i, f: (i, 0))
    return pl.pallas_call(
        _ffn_kernel,
        grid=(m // tm, ff // tf),
        in_specs=[row, row, pl.BlockSpec((d, tf), lambda i, f: (0, f)), pl.BlockSpec((d, tf), lambda i, f: (0, f)),
                  pl.BlockSpec((tf, d), lambda i, f: (f, 0))],
        out_specs=row,
        out_shape=jax.ShapeDtypeStruct((m, d), F32),
        scratch_shapes=[pltpu.VMEM((tm, d), F32)],
        compiler_params=_cparams("arbitrary", "arbitrary"),
        name="ffn_dense",
    )(xn, x1, wg, wu, wd)


def _router_kernel(xn_ref, wr_ref, gates_ref):
    lane, i1, i2, w1, w2 = _top2(_dot(xn_ref[...], wr_ref[...]))
    gates_ref[...] = jnp.where(lane == i1, w1, 0.0) + jnp.where(lane == i2, w2, 0.0)


def _router(xn, wr_pad):
    m, d = xn.shape
    tm = min(ROW_TILE, m)
    return pl.pallas_call(
        _router_kernel,
        grid=(m // tm,),
        in_specs=[pl.BlockSpec((tm, d), lambda i: (i, 0)), _full(wr_pad.shape)],
        out_specs=pl.BlockSpec((tm, LANES), lambda i: (i, 0)),
        out_shape=jax.ShapeDtypeStruct((m, LANES), F32),
        compiler_params=_cparams("arbitrary"),
        name="router",
    )(xn, wr_pad)


def _moe_kernel(xn_ref, x1_ref, gates_ref, wg_ref, wu_ref, wd_ref, o_ref, acc_ref):
    e = pl.program_id(1)
    f = pl.program_id(2)

    @pl.when((e == 0) & (f == 0))
    def _():
        acc_ref[...] = jnp.zeros_like(acc_ref)

    xn = xn_ref[...]
    hid = jax.nn.silu(_dot(xn, wg_ref[0])) * _dot(xn, wu_ref[0])
    gates = gates_ref[...]
    lane = lax.broadcasted_iota(jnp.int32, gates.shape, 1)
    gate = jnp.sum(jnp.where(lane == e, gates, 0.0), axis=-1, keepdims=True)
    acc_ref[...] += gate * _dot(hid.astype(BF16), wd_ref[0])

    @pl.when((e == pl.num_programs(1) - 1) & (f == pl.num_programs(2) - 1))
    def _():
        o_ref[...] = x1_ref[...] + acc_ref[...]


def _moe(xn, x1, gates, wg, wu, wd):
    m, d = x1.shape
    n_e, _, ff = wg.shape
    tm = min(FFN_ROW_TILE, m)
    tf = MOE_COL_TILE
    row = lambda w: pl.BlockSpec((tm, w), lambda i, e, f: (i, 0))
    return pl.pallas_call(
        _moe_kernel,
        grid=(m // tm, n_e, ff // tf),
        in_specs=[row(d), row(d), row(LANES),
                  pl.BlockSpec((1, d, tf), lambda i, e, f: (e, 0, f)),
                  pl.BlockSpec((1, d, tf), lambda i, e, f: (e, 0, f)),
                  pl.BlockSpec((1, tf, d), lambda i, e, f: (e, f, 0))],
        out_specs=row(d),
        out_shape=jax.ShapeDtypeStruct((m, d), F32),
        scratch_shapes=[pltpu.VMEM((tm, d), F32)],
        compiler_params=_cparams("arbitrary", "arbitrary", "arbitrary"),
        name="moe_dense",
    )(xn, x1, gates, wg, wu, wd)


def _ple_kernel(final, x_ref, p_ref, g_ref, wgate_ref, wproj_ref, gf_ref, o_ref):
    x = x_ref[...]
    gate = jax.nn.sigmoid(_dot(_rmsnorm(x, g_ref[...]).astype(BF16), wgate_ref[...]))
    x3 = x + gate * _dot(p_ref[0].astype(BF16), wproj_ref[...])
    o_ref[...] = _rmsnorm(x3, gf_ref[...]) if final else x3


def _ple(x, p_all, layer, g, wgate, wproj, g_final, final):
    m, d = x.shape
    tm = min(ROW_TILE, m)
    row = lambda w: pl.BlockSpec((tm, w), lambda i: (i, 0))
    return pl.pallas_call(
        functools.partial(_ple_kernel, final),
        grid=(m // tm,),
        in_specs=[row(d), pl.BlockSpec((1, tm, p_all.shape[2]), lambda i: (layer, i, 0))]
        + [_full(v.shape) for v in (g, wgate, wproj, g_final)],
        out_specs=row(d),
        out_shape=jax.ShapeDtypeStruct((m, d), F32),
        compiler_params=_cparams("arbitrary"),
        name="ple",
    )(x, p_all, g, wgate, wproj, g_final)


def _row(v):
    return v.reshape(1, -1)


def kernel(x_prompt, x_sample, p_prompt, p_sample, cache_k, cache_v, cache_logf, state_conv, state_h, page_table, g_mix, w_in, b_f, conv_w, conv_b, w_rg, b_rg, w_ig, b_ig, lru_lambda, w_a_up, w_b_up, w_out, g_ffn, dense_wg, dense_wu, dense_wd, moe_router, moe_wg, moe_wu, moe_wd, g_ple, w_ple_gate, w_ple_proj, g_final):
    depth = g_mix.shape[0]
    b, t, d = x_prompt.shape
    db = x_sample.shape[0]
    n_pool, page = cache_k.shape[1], cache_k.shape[2]
    a = ATTN_WIDTH
    kt_pool = jnp.transpose(cache_k, (0, 1, 3, 4, 2)).reshape(depth * n_pool * ATTN_WIDTH, page)
    vt_pool = jnp.transpose(cache_v, (0, 1, 3, 4, 2)).reshape(depth * n_pool * ATTN_WIDTH, page)
    lt_pool = jnp.transpose(cache_logf, (0, 1, 3, 2)).reshape(depth * n_pool, N_HEADS, page)

    xp = x_prompt
    xs = x_sample.reshape(db, d)
    m = b * t
    pp_all = p_prompt.reshape(depth, m, -1)
    ps_all = p_sample.reshape(depth, db, -1)
    kt_all = jnp.zeros((depth, b, a, t), F32)
    vt_all = jnp.zeros((depth, b, a, t), F32)
    outs = {name: [] for name in ("lp", "cp", "hp", "ks", "vs", "ls", "cs", "hs")}
    for l in range(depth):
        wl = w_in[l]
        wq = wl[:, 0:a].astype(BF16)
        wk = wl[:, a:2 * a].astype(BF16)
        wv = wl[:, 2 * a:3 * a].astype(BF16)
        wf = jnp.pad(wl[:, 3 * a:3 * a + N_HEADS], ((0, 0), (0, LANES - N_HEADS))).astype(BF16)
        w4 = wl[:, 3 * a + N_HEADS:].astype(BF16)
        bf_row = jnp.pad(b_f[l], (0, LANES - N_HEADS)).reshape(1, LANES)
        g1 = _row(g_mix[l])
        lru_w = (conv_w[l], _row(conv_b[l]), w_rg[l].astype(BF16), _row(b_rg[l]), w_ig[l].astype(BF16),
                 _row(b_ig[l]), _row(lru_lambda[l]))
        wa = w_a_up[l].astype(BF16)
        wb = w_b_up[l].astype(BF16)
        wo = w_out[l].astype(BF16)
        g2 = _row(g_ffn[l])
        g3 = _row(g_ple[l])
        wgate = w_ple_gate[l].astype(BF16)
        wproj = w_ple_proj[l].astype(BF16)
        gf = _row(g_final)
        final = l == depth - 1
        mi = l // 2
        if l % 2 == 0:
            ffn_w = (dense_wg[mi], dense_wu[mi], dense_wd[mi])
        else:
            wr_pad = jnp.pad(moe_router[mi], ((0, 0), (0, LANES - N_EXPERTS))).astype(BF16)
            ffn_w = (moe_wg[mi].astype(BF16), moe_wu[mi].astype(BF16), moe_wd[mi].astype(BF16))

        qa, kta, kt_all, vt_all, vtb, lft, xr, gr, ga, gb = _proj_prompt(
            xp, g1, wq, wk.T, wv.T, wf, bf_row, w4, l, kt_all, vt_all)
        attn = _attn_prompt(qa, kta, vtb)
        y, tail, hl = _lru_prompt(xr, gr, jnp.zeros((b, SUBLANES, d), F32), jnp.zeros((b, 1, d), F32), *lru_w)
        merge_in = (attn.reshape(m, a), y.reshape(m, d), ga.reshape(m, d), gb.reshape(m, d),
                    xp.reshape(m, d), wa, wb, wo, g2)
        if l % 2 == 0:
            x1, xn = _merge(*merge_in)
            x2 = _ffn(xn, x1, *ffn_w)
        else:
            x1, x_rt, meta, meta_t, counts = _merge_route(*merge_in, wr_pad)
            x2 = _moe_sorted(x1, x_rt, meta, meta_t, counts, *ffn_w)
        xp = _ple(x2, pp_all, l, g3, wgate, wproj, gf, final).reshape(b, t, d)
        outs["lp"].append(jnp.transpose(lft, (0, 2, 1)))
        outs["cp"].append(tail[:, SUBLANES - (CONV_WIDTH - 1):, :])
        outs["hp"].append(hl.reshape(b, d))

        qs, ks, vs, lfs, xrs, grs, gas, gbs = _proj_sample(xs, g1, wq, wk, wv, wf, bf_row, w4)
        lf8 = lfs[:, :N_HEADS]
        attn_s = _attn_decode(l, page_table, qs.reshape(db, 1, a), ks.reshape(db, 1, a), vs.reshape(db, 1, a),
                              lf8.reshape(db, N_HEADS, 1), kt_pool, vt_pool, lt_pool, n_pool).reshape(db, a)
        sc = state_conv[l]
        ys, hn = _lru_sample(xrs, grs, sc[:, 0], sc[:, 1], sc[:, 2], state_h[l], *lru_w)
        x1s, xns = _merge(attn_s, ys, gas, gbs, xs, wa, wb, wo, g2)
        if l % 2 == 0:
            x2s = _ffn(xns, x1s, *ffn_w)
        else:
            x2s = _moe(xns, x1s, _router(xns, wr_pad), *ffn_w)
        xs = _ple(x2s, ps_all, l, g3, wgate, wproj, gf, final)
        outs["ks"].append(ks.reshape(db, 1, N_HEADS, HEAD_DIM))
        outs["vs"].append(vs.reshape(db, 1, N_HEADS, HEAD_DIM))
        outs["ls"].append(lf8.reshape(db, 1, N_HEADS))
        outs["cs"].append(jnp.stack([sc[:, 1], sc[:, 2], xrs], axis=1))
        outs["hs"].append(hn)

    st = lambda name: jnp.stack(outs[name])
    heads_last = lambda v: jnp.transpose(v.reshape(depth, b, N_HEADS, HEAD_DIM, t), (0, 1, 4, 2, 3))
    return (xp, xs.reshape(db, 1, d), heads_last(kt_all), heads_last(vt_all), st("lp"), st("cp"), st("hp"),
            st("ks"), st("vs"), st("ls"), st("cs"), st("hs"))
```
